```python
import math, functools
import jax, jax.numpy as jnp
from jax import lax
import numpy as np

D_MODEL = 1024
BATCH = 8
SEQ = 2048
DEPTH = 2

GRID_W = 64
CTX_LEN = 256
HEAD_DIM = 64
N_BRANCH = 4
BRANCH_W = D_MODEL // 2
NA_HEADS = BRANCH_W // HEAD_DIM
NA_WIN_R = 8
NA_WIN_C = 16
RET_HEADS = BRANCH_W // (2 * HEAD_DIM)
RET_DK = HEAD_DIM
RET_DV = 2 * HEAD_DIM
RET_CHUNK = 128
DIFF_HEADS = BRANCH_W // (2 * HEAD_DIM)
DIFF_DK = HEAD_DIM
DIFF_DV = 2 * HEAD_DIM
DIFF_BLOCK = 128
DN_HEADS = BRANCH_W // (2 * HEAD_DIM)
DN_DK = HEAD_DIM
DN_DV = 2 * HEAD_DIM
DN_CONV = 5
DN_CHUNK = 64
DN_CONV_CH = 2 * DN_HEADS * DN_DK + DN_HEADS * DN_DV
N_EXPERTS = 32
TOP_K = 4
D_FF = D_MODEL
SWIGLU_LIMIT = 7.0
SWIGLU_ALPHA = 1.702
MOE_BLOCK = 128
ROPE_BASE = 10000.0
EPS = 1e-6
IN_SPLITS = (
    NA_HEADS * HEAD_DIM, NA_HEADS * HEAD_DIM, NA_HEADS * HEAD_DIM,
    RET_HEADS * RET_DK, RET_HEADS * RET_DK, RET_HEADS * RET_DV, BRANCH_W,
    DIFF_HEADS * 2 * DIFF_DK, DIFF_HEADS * 2 * DIFF_DK, DIFF_HEADS * DIFF_DV,
    DN_HEADS * DN_DK, DN_HEADS * DN_DK, DN_HEADS * DN_DV, BRANCH_W, 2 * DN_HEADS, 2 * DN_HEADS,
)
D_IN = sum(IN_SPLITS)
F32 = jnp.float32

kernel_name = 'hybrid_na_ret_diff_gdn_moe_dit'


def rms_norm(x, g):
    xf = x.astype(F32)
    y = xf * lax.rsqrt(jnp.mean(xf * xf, axis=-1, keepdims=True) + EPS)
    return (y * g.astype(F32)).astype(x.dtype)


def head_layer_norm(x, g):
    xf = x.astype(F32)
    xc = xf - jnp.mean(xf, axis=-1, keepdims=True)
    return xc * lax.rsqrt(jnp.mean(xc * xc, axis=-1, keepdims=True) + EPS) * g.astype(F32)


def l2_normalize(t):
    tf = t.astype(F32)
    return tf * lax.rsqrt(jnp.sum(tf * tf, axis=-1, keepdims=True) + EPS)


def modulate(h, shift, scale):
    return h * (1.0 + scale) + shift


def rope_angles(pos, dim):
    inv = ROPE_BASE ** (-jnp.arange(0, dim, 2, dtype=F32) / dim)
    return pos.astype(F32)[:, None] * inv[None, :]


def apply_rope(x, ang):
    half = x.shape[-1] // 2
    shape = (x.shape[1],) + (1,) * (x.ndim - 3) + (half,)
    cos = jnp.cos(ang).reshape(shape).astype(x.dtype)
    sin = jnp.sin(ang).reshape(shape).astype(x.dtype)
    x1, x2 = x[..., :half], x[..., half:]
    return jnp.concatenate([x1 * cos - x2 * sin, x2 * cos + x1 * sin], axis=-1)


def axial_rope(x, ang_row, ang_col):
    half = x.shape[-1] // 2
    return jnp.concatenate([apply_rope(x[..., :half], ang_row), apply_rope(x[..., half:], ang_col)], axis=-1)


def prefix_scan(scan_fn, ctx_seq, lat_seq, s0, reverse):
    if reverse:
        ctx_seq = tuple(jnp.flip(t, 1) for t in ctx_seq)
        lat_seq = tuple(jnp.flip(t, 1) for t in lat_seq)
    o_ctx, s_ctx = scan_fn(*ctx_seq, s0)
    o_lat, _ = scan_fn(*lat_seq, s_ctx)
    if reverse:
        o_ctx, o_lat = jnp.flip(o_ctx, 1), jnp.flip(o_lat, 1)
    return jnp.concatenate([o_ctx, o_lat], axis=1)


def short_conv(t, w):
    return lax.conv_general_dilated(t, w[:, None, :].astype(t.dtype), window_strides=(1,),
                                    padding=[(DN_CONV // 2, DN_CONV // 2)],
                                    dimension_numbers=('NWC', 'WIO', 'NWC'),
                                    feature_group_count=t.shape[-1])


def neighbourhood_branch(q, k, v, n_ctx, qk_gain, rpb):
    B, L, _ = q.shape
    H, d = NA_HEADS, HEAD_DIM
    q = rms_norm(q.reshape(B, L, H, d), qk_gain[0])
    k = rms_norm(k.reshape(B, L, H, d), qk_gain[1])
    v = v.reshape(B, L, H, d)
    qc, kc, vc = q[:, :n_ctx], k[:, :n_ctx], v[:, :n_ctx]
    S = L - n_ctx
    rows = S // GRID_W
    win_r = min(NA_WIN_R, rows)
    n_nb = win_r * NA_WIN_C
    scale = d ** -0.5
    s_cc = jnp.einsum('bqhd,bkhd->bhqk', qc, kc).astype(F32) * scale
    o_ctx = jnp.einsum('bhqk,bkhd->bqhd', jax.nn.softmax(s_cc, axis=-1).astype(v.dtype), vc)
    qg = q[:, n_ctx:].reshape(B, rows, GRID_W, H, d)
    kg = k[:, n_ctx:].reshape(B, rows, GRID_W, H, d)
    vg = v[:, n_ctx:].reshape(B, rows, GRID_W, H, d)
    col = np.arange(GRID_W)
    cols = np.clip(col - NA_WIN_C // 2, 0, GRID_W - NA_WIN_C)[:, None] + np.arange(NA_WIN_C)[None, :]
    dc_idx = jnp.asarray(cols - col[:, None] + NA_WIN_C - 1)

    def row_block(r):
        r0 = jnp.clip(r - win_r // 2, 0, rows - win_r)
        kn = lax.dynamic_slice_in_dim(kg, r0, win_r, axis=1)[:, :, cols]
        vn = lax.dynamic_slice_in_dim(vg, r0, win_r, axis=1)[:, :, cols]
        qr = lax.dynamic_index_in_dim(qg, r, axis=1, keepdims=False)
        dr_idx = r0 + jnp.arange(win_r) - r + NA_WIN_R - 1
        bias = rpb[:, dr_idx[None, :, None], dc_idx[:, None, :]].astype(F32)
        s_nb = jnp.einsum('bqhd,biqjhd->bhqij', qr, kn).astype(F32) * scale + bias
        s_ctx = jnp.einsum('bqhd,bkhd->bhqk', qr, kc).astype(F32) * scale
        p = jax.nn.softmax(jnp.concatenate([s_nb.reshape(B, H, GRID_W, n_nb), s_ctx], axis=-1), axis=-1).astype(v.dtype)
        p_nb = p[..., :n_nb].reshape(B, H, GRID_W, win_r, NA_WIN_C)
        return (jnp.einsum('bhqij,biqjhd->bqhd', p_nb, vn)
                + jnp.einsum('bhqk,bkhd->bqhd', p[..., n_nb:], vc))

    o_lat = jnp.moveaxis(lax.map(row_block, jnp.arange(rows)), 0, 1).reshape(B, S, H, d)
    return jnp.concatenate([o_ctx, o_lat], axis=1).reshape(B, L, H * d)


def retention_chunked(q, k, v, s0, log_g):
    B, L, H, _ = q.shape
    dv = v.shape[-1]
    c = RET_CHUNK
    n = L // c
    pos = jnp.arange(c, dtype=F32)
    rel = pos[:, None] - pos[None, :]
    intra = jnp.where(rel >= 0, jnp.exp(log_g[:, None, None] * jnp.maximum(rel, 0.0)), 0.0)
    q_dec = jnp.exp(log_g[None, :] * (pos[:, None] + 1.0))
    k_dec = jnp.exp(log_g[None, :] * (c - 1.0 - pos[:, None]))
    c_dec = jnp.exp(log_g * c)

    def to_chunks(t):
        return jnp.moveaxis(t.astype(F32).reshape(B, n, c, H, t.shape[-1]), 1, 0)

    def step(s, inp):
        qi, ki, vi = inp
        att = jnp.einsum('bihd,bjhd->bhij', qi, ki) * intra
        o = (jnp.einsum('bhij,bjhe->bihe', att, vi)
             + jnp.einsum('bihd,bhde->bihe', qi * q_dec[:, :, None], s))
        s = s * c_dec[:, None, None] + jnp.einsum('bjhd,bjhe->bhde', ki * k_dec[:, :, None], vi)
        return s, o

    s, o = lax.scan(step, s0.astype(F32), (to_chunks(q), to_chunks(k), to_chunks(v)))
    return jnp.moveaxis(o, 0, 1).reshape(B, L, H, dv), s


def retention_branch(q, k, v, gate, n_ctx, ang_t, decay, gn_gain):
    B, L, _ = q.shape
    H = RET_HEADS
    q = q.reshape(B, L, H, RET_DK)
    k = k.reshape(B, L, H, RET_DK)
    v = v.reshape(B, L, H, RET_DV)
    q = jnp.concatenate([q[:, :n_ctx], apply_rope(q[:, n_ctx:], ang_t)], axis=1)
    k = jnp.concatenate([k[:, :n_ctx], apply_rope(k[:, n_ctx:], ang_t)], axis=1) * (RET_DK ** -0.5)
    log_g = jax.nn.log_sigmoid(decay.astype(F32))
    s0 = jnp.zeros((B, H, RET_DK, RET_DV), F32)
    ctx_seq = (q[:, :n_ctx], k[:, :n_ctx], v[:, :n_ctx])
    lat_seq = (q[:, n_ctx:], k[:, n_ctx:], v[:, n_ctx:])
    y_fwd = prefix_scan(functools.partial(retention_chunked, log_g=log_g[0]), ctx_seq, lat_seq, s0, False)
    y_bwd = prefix_scan(functools.partial(retention_chunked, log_g=log_g[1]), ctx_seq, lat_seq, s0, True)
    y = head_layer_norm(y_fwd + y_bwd, gn_gain) * jax.nn.silu(gate.astype(F32)).reshape(B, L, H, RET_DV)
    return y.reshape(B, L, H * RET_DV).astype(gate.dtype)


def diff_branch(q, k, v, n_ctx, ang_r, ang_c, qk_gain, lam_p, subln_gain, lam_init):
    B, L, _ = q.shape
    H = DIFF_HEADS
    q = rms_norm(q.reshape(B, L, H, 2, DIFF_DK), qk_gain[0])
    k = rms_norm(k.reshape(B, L, H, 2, DIFF_DK), qk_gain[1])
    q = jnp.concatenate([q[:, :n_ctx], axial_rope(q[:, n_ctx:], ang_r, ang_c)], axis=1)
    k = jnp.concatenate([k[:, :n_ctx], axial_rope(k[:, n_ctx:], ang_r, ang_c)], axis=1)
    v = v.reshape(B, L, H, DIFF_DV)
    lp = lam_p.astype(F32)
    lam = jnp.exp(jnp.sum(lp[0] * lp[1])) - jnp.exp(jnp.sum(lp[2] * lp[3])) + lam_init
    scale = DIFF_DK ** -0.5

    def attend(qb, kb, vb):
        s = jnp.einsum('bqhcd,bkhcd->bhcqk', qb, kb).astype(F32) * scale
        p = jax.nn.softmax(s, axis=-1)
        a = (p[:, :, 0] - lam * p[:, :, 1]).astype(vb.dtype)
        return jnp.einsum('bhqk,bkhe->bqhe', a, vb)

    o_ctx = attend(q[:, :n_ctx], k[:, :n_ctx], v[:, :n_ctx])
    S = L - n_ctx
    qb = jnp.moveaxis(q[:, n_ctx:].reshape(B, S // DIFF_BLOCK, DIFF_BLOCK, H, 2, DIFF_DK), 1, 0)
    o_lat = jnp.moveaxis(lax.map(lambda qq: attend(qq, k, v), qb), 0, 1).reshape(B, S, H, DIFF_DV)
    o = rms_norm(jnp.concatenate([o_ctx, o_lat], axis=1), subln_gain) * (1.0 - lam_init)
    return o.reshape(B, L, H * DIFF_DV)


def gated_delta_chunked(q, k, v, g, beta, s0):
    B, L, H, _ = q.shape
    dv = v.shape[-1]
    c = DN_CHUNK
    n = L // c

    def hc(t):
        return jnp.swapaxes(t.astype(F32), 1, 2).reshape((B, H, n, c) + t.shape[3:])

    q, k, v, g, beta = hc(q), hc(k), hc(v), hc(g), hc(beta)
    g = jnp.cumsum(g, axis=-1)
    idx = jnp.arange(c)
    incl = idx[:, None] >= idx[None, :]
    strict = idx[:, None] > idx[None, :]
    decay = jnp.where(incl, jnp.exp(jnp.where(incl, g[..., :, None] - g[..., None, :], 0.0)), 0.0)
    kb = k * beta[..., None]
    lower = jnp.where(strict, jnp.einsum('bhnid,bhnjd->bhnij', kb, k) * decay, 0.0)
    eye = jnp.eye(c, dtype=F32)
    tmat = lax.linalg.triangular_solve(eye + lower, jnp.broadcast_to(eye, lower.shape),
                                       left_side=True, lower=True, unit_diagonal=True)
    u = tmat @ (v * beta[..., None])
    w = tmat @ (kb * jnp.exp(g)[..., None])
    qk = jnp.einsum('bhnid,bhnjd->bhnij', q, k) * decay
    eg = jnp.exp(g)
    k_tail = k * jnp.exp(g[..., -1:] - g)[..., None]
    g_last = jnp.exp(g[..., -1])

    def step(s, inp):
        qi, ui, wi, qki, egi, kti, gli = inp
        v_new = ui - wi @ s
        o = (qi * egi[..., None]) @ s + qki @ v_new
        s = s * gli[..., None, None] + jnp.swapaxes(kti, -1, -2) @ v_new
        return s, o

    xs = tuple(jnp.moveaxis(t, 2, 0) for t in (q, u, w, qk, eg, k_tail, g_last))
    s, o = lax.scan(step, s0.astype(F32), xs)
    o = jnp.moveaxis(o, 0, 2).reshape(B, H, L, dv)
    return jnp.swapaxes(o, 1, 2), s


def deltanet_branch(q, k, v, gate, a, b, n_ctx, conv_w, a_log, dt_bias, norm_gain):
    B, L, _ = q.shape
    H = DN_HEADS
    qkv = jnp.concatenate([q, k, v], axis=-1)
    qkv = jax.nn.silu(jnp.concatenate([short_conv(qkv[:, :n_ctx], conv_w),
                                       short_conv(qkv[:, n_ctx:], conv_w)], axis=1))
    q, k, v = jnp.split(qkv, [H * DN_DK, 2 * H * DN_DK], axis=-1)
    q = l2_normalize(q.reshape(B, L, H, DN_DK)) * (DN_DK ** -0.5)
    k = l2_normalize(k.reshape(B, L, H, DN_DK))
    v = v.reshape(B, L, H, DN_DV)
    g = -jnp.exp(a_log.astype(F32)) * jax.nn.softplus(a.reshape(B, L, 2, H).astype(F32) + dt_bias.astype(F32))
    beta = jax.nn.sigmoid(b.reshape(B, L, 2, H).astype(F32))
    s0 = jnp.zeros((B, H, DN_DK, DN_DV), F32)

    def run(direction):
        seq = (q, k, v, g[:, :, direction], beta[:, :, direction])
        return prefix_scan(gated_delta_chunked, tuple(t[:, :n_ctx] for t in seq),
                           tuple(t[:, n_ctx:] for t in seq), s0, direction == 1)

    y = rms_norm(run(0) + run(1), norm_gain) * jax.nn.silu(gate.astype(F32)).reshape(B, L, H, DN_DV)
    return y.reshape(B, L, H * DN_DV).astype(gate.dtype)


def mixer_sublayer(hz, n_ctx, ang_r, ang_c, ang_t, lam_init, w_in, na_qk_gain, na_rpb, ret_decay, ret_gn,
                   diff_qk_gain, diff_lam, diff_subln, dn_conv, dn_a_log, dn_dt_bias, dn_norm,
                   w_branch, w_mgate, b_mgate, w_out):
    B, L, D = hz.shape
    p = hz @ w_in
    (na_q, na_k, na_v, r_q, r_k, r_v, r_g, df_q, df_k, df_v,
     dn_q, dn_k, dn_v, dn_g, dn_a, dn_b) = jnp.split(p, [int(i) for i in np.cumsum(IN_SPLITS)[:-1]], axis=-1)
    y_na = neighbourhood_branch(na_q, na_k, na_v, n_ctx, na_qk_gain, na_rpb)
    y_ret = retention_branch(r_q, r_k, r_v, r_g, n_ctx, ang_t, ret_decay, ret_gn)
    y_diff = diff_branch(df_q, df_k, df_v, n_ctx, ang_r, ang_c, diff_qk_gain, diff_lam, diff_subln, lam_init)
    y_dn = deltanet_branch(dn_q, dn_k, dn_v, dn_g, dn_a, dn_b, n_ctx, dn_conv, dn_a_log, dn_dt_bias, dn_norm)
    ys = jnp.stack([y_na, y_ret, y_diff, y_dn], axis=2)
    branches = jnp.einsum('blnf,nfd->blnd', ys, w_branch)
    gates = jax.nn.sigmoid((hz @ w_mgate + b_mgate).astype(F32)).astype(hz.dtype).reshape(B, L, N_BRANCH, D)
    return jnp.einsum('blnd,blnd->bld', gates, branches) @ w_out


def moe_ffn(h, router_w, router_b, w_gu, b_gu, w_down, b_down):
    T, D = h.shape
    logits = (h @ router_w + router_b).astype(F32)
    top_val, top_idx = lax.top_k(logits, TOP_K)
    gates = jax.nn.softmax(top_val, axis=-1)
    A = T * TOP_K
    flat_e = top_idx.reshape(A)
    order = jnp.argsort(flat_e)
    sorted_e = flat_e[order]
    counts = jnp.bincount(flat_e, length=N_EXPERTS)
    starts = jnp.cumsum(counts) - counts
    padded = (counts + MOE_BLOCK - 1) // MOE_BLOCK * MOE_BLOCK
    pad_end = jnp.cumsum(padded)
    pad_start = pad_end - padded
    dest = pad_start[sorted_e] + jnp.arange(A) - starts[sorted_e]
    n_blocks = -(-(A + N_EXPERTS * (MOE_BLOCK - 1)) // MOE_BLOCK)
    tok = order // TOP_K
    rows = jnp.zeros((n_blocks * MOE_BLOCK, D), h.dtype).at[dest].set(h[tok])
    block_e = jnp.minimum(jnp.searchsorted(pad_end, jnp.arange(n_blocks) * MOE_BLOCK, side='right'), N_EXPERTS - 1)

    def expert_block(args):
        xb, e = args
        gu = xb @ w_gu[e] + b_gu[e]
        gate, up = jnp.split(gu, 2, axis=-1)
        gate = jnp.minimum(gate, SWIGLU_LIMIT)
        up = jnp.clip(up, -SWIGLU_LIMIT, SWIGLU_LIMIT)
        y = (up + 1.0) * (gate * jax.nn.sigmoid(SWIGLU_ALPHA * gate))
        return y @ w_down[e] + b_down[e]

    out = lax.map(expert_block, (rows.reshape(n_blocks, MOE_BLOCK, D), block_e)).reshape(-1, D)
    y = out[dest] * gates.reshape(A)[order][:, None].astype(h.dtype)
    return jax.ops.segment_sum(y, tok, num_segments=T)


def setup_inputs(seed: int = 0) -> dict:
    key = jax.random.key(seed)
    ks = iter(jax.random.split(key, 40))

    def nrm(shape, scale):
        return jax.random.normal(next(ks), shape, F32) * scale

    def gain(shape):
        return 1.0 + nrm(shape, 0.02)

    Ly, D = DEPTH, D_MODEL
    gamma0 = 1.0 - 2.0 ** (-5.0 - np.arange(RET_HEADS, dtype=np.float32))
    ret_logit0 = jnp.asarray(np.log(gamma0 / (1.0 - gamma0)), F32)
    dt = jnp.exp(jax.random.uniform(next(ks), (Ly, 2, DN_HEADS), F32, math.log(1e-3), math.log(1e-1)))
    a0 = jax.random.uniform(next(ks), (Ly, 2, DN_HEADS), F32, 1.0, 16.0)
    return {
        'x': nrm((BATCH, SEQ, D), 1.0),
        'c': nrm((BATCH, D), 1.0),
        'ctx': nrm((BATCH, CTX_LEN, D), 1.0),
        'c_ctx': nrm((D,), 1.0),
        'w_ada': nrm((Ly, D, 6 * D), 0.5 * D ** -0.5),
        'b_ada': nrm((Ly, 6 * D), 0.02),
        'norm_mix': gain((Ly, D)),
        'norm_ffn': gain((Ly, D)),
        'w_in': nrm((Ly, D, D_IN), D ** -0.5),
        'na_qk_gain': gain((Ly, 2, HEAD_DIM)),
        'na_rpb': nrm((Ly, NA_HEADS, 2 * NA_WIN_R - 1, 2 * NA_WIN_C - 1), 0.1),
        'ret_decay': ret_logit0 + nrm((Ly, 2, RET_HEADS), 0.05),
        'ret_gn': gain((Ly, RET_DV)),
        'diff_qk_gain': gain((Ly, 2, DIFF_DK)),
        'diff_lam': nrm((Ly, 4, DIFF_DK), 0.1),
        'diff_subln': gain((Ly, DIFF_DV)),
        'dn_conv': nrm((Ly, DN_CONV, DN_CONV_CH), DN_CONV ** -0.5),
        'dn_a_log': jnp.log(a0),
        'dn_dt_bias': jnp.log(jnp.expm1(dt)),
        'dn_norm': gain((Ly, DN_DV)),
        'w_branch': nrm((Ly, N_BRANCH, BRANCH_W, D), BRANCH_W ** -0.5),
        'w_mgate': nrm((Ly, D, N_BRANCH * D), D ** -0.5),
        'b_mgate': nrm((Ly, N_BRANCH * D), 0.02),
        'w_out': nrm((Ly, D, D), D ** -0.5),
        'router_w': nrm((Ly, D, N_EXPERTS), D ** -0.5),
        'router_b': nrm((Ly, N_EXPERTS), 0.01),
        'w_gu': nrm((Ly, N_EXPERTS, D, 2 * D_FF), D ** -0.5),
        'b_gu': nrm((Ly, N_EXPERTS, 2 * D_FF), 0.01),
        'w_down': nrm((Ly, N_EXPERTS, D_FF, D), D_FF ** -0.5),
        'b_down': nrm((Ly, N_EXPERTS, D), 0.01),
    }


def reference(x, c, ctx, c_ctx, w_ada, b_ada, norm_mix, norm_ffn, w_in, na_qk_gain, na_rpb,
              ret_decay, ret_gn, diff_qk_gain, diff_lam, diff_subln, dn_conv, dn_a_log, dn_dt_bias,
              dn_norm, w_branch, w_mgate, b_mgate, w_out, router_w, router_b, w_gu, b_gu, w_down, b_down):
    B, S, D = x.shape
    n_ctx = ctx.shape[1]
    t = jnp.arange(S)
    ang_r = rope_angles(t // GRID_W, HEAD_DIM // 2)
    ang_c = rope_angles(t % GRID_W, HEAD_DIM // 2)
    ang_t = rope_angles(t, RET_DK)
    xc, xl = ctx, x
    for l in range(DEPTH):
        lam_init = 0.8 - 0.6 * math.exp(-0.3 * l)
        ml = jnp.split((jax.nn.silu(c) @ w_ada[l] + b_ada[l])[:, None, :], 6, axis=-1)
        mc = jnp.split(jax.nn.silu(c_ctx) @ w_ada[l] + b_ada[l], 6, axis=-1)
        ffn = functools.partial(moe_ffn, router_w=router_w[l], router_b=router_b[l], w_gu=w_gu[l],
                                b_gu=b_gu[l], w_down=w_down[l], b_down=b_down[l])
        hz = jnp.concatenate([modulate(rms_norm(xc, norm_mix[l]), mc[0], mc[1]),
                              modulate(rms_norm(xl, norm_mix[l]), ml[0], ml[1])], axis=1)
        mix = mixer_sublayer(hz, n_ctx, ang_r, ang_c, ang_t, lam_init, w_in[l], na_qk_gain[l], na_rpb[l],
                             ret_decay[l], ret_gn[l], diff_qk_gain[l], diff_lam[l], diff_subln[l],
                             dn_conv[l], dn_a_log[l], dn_dt_bias[l], dn_norm[l],
                             w_branch[l], w_mgate[l], b_mgate[l], w_out[l])
        xl = xl + ml[2] * mix[:, n_ctx:]
        hl = modulate(rms_norm(xl, norm_ffn[l]), ml[3], ml[4])
        if l == DEPTH - 1:
            xl = xl + ml[5] * ffn(hl.reshape(-1, D)).reshape(B, S, D)
        else:
            xc = xc + mc[2] * mix[:, :n_ctx]
            hc = modulate(rms_norm(xc, norm_ffn[l]), mc[3], mc[4])
            f = ffn(jnp.concatenate([hc, hl], axis=1).reshape(-1, D)).reshape(B, n_ctx + S, D)
            xc = xc + mc[5] * f[:, :n_ctx]
            xl = xl + ml[5] * f[:, n_ctx:]
    return xl
```

```python
import functools
import math

import numpy as np
import jax
import jax.numpy as jnp
from jax import lax
from jax.experimental import pallas as pl
from jax.experimental.pallas import tpu as pltpu

F32 = jnp.float32
BF16 = jnp.bfloat16

D_MODEL = 1024
GRID_W = 64
CTX_LEN = 256
HEAD_DIM = 64
N_BRANCH = 4
BRANCH_W = 512
NA_HEADS = 8
NA_WIN_R = 8
NA_WIN_C = 16
RET_HEADS = 4
RET_CHUNK = 128
DIFF_HEADS = 4
DN_HEADS = 4
DN_CONV = 5
DN_CHUNK = 64
N_EXPERTS = 32
TOP_K = 4
SWIGLU_LIMIT = 7.0
SWIGLU_ALPHA = 1.702
ROPE_BASE = 10000.0
EPS = 1e-6
D_IN = 6160
D_IN_PAD = 6272
NEG_BIG = -1e30

VMEM_LIMIT = 56 * 1024 * 1024

TM = 768
TN_IN = 896
TM_ROUTE = 256
TM_DMA = 256
MOE_BM = 256


def _dot(a, b):
    return jnp.dot(a, b, preferred_element_type=F32)


def _dot_nt(a, b):
    return lax.dot_general(a, b, (((1,), (1,)), ((), ())), preferred_element_type=F32)


def _dot_tn(a, b):
    return lax.dot_general(a, b, (((0,), (0,)), ((), ())), preferred_element_type=F32)


def _split_dot(x, w_bf16):
    hi = x.astype(BF16)
    lo = (x - hi.astype(F32)).astype(BF16)
    return _dot(hi, w_bf16) + _dot(lo, w_bf16)


def _group_ones(n, group):
    r = lax.broadcasted_iota(jnp.int32, (n, n), 0) // group
    c = lax.broadcasted_iota(jnp.int32, (n, n), 1) // group
    return jnp.where(r == c, 1.0, 0.0).astype(BF16)


def _sigmoid(x):
    return 1.0 / (1.0 + jnp.exp(-x))


def _silu(x):
    return x * _sigmoid(x)


def _ada_kernel(c_ref, w_ref, b_ref, o_ref):
    c = c_ref[...]
    a = _silu(c)
    o_ref[...] = _split_dot_w(a, w_ref[...]) + b_ref[...]


def _split_dot_w(a, w):
    ah = a.astype(BF16)
    al = (a - ah.astype(F32)).astype(BF16)
    wh = w.astype(BF16)
    wl = (w - wh.astype(F32)).astype(BF16)
    return _dot(ah, wh) + _dot(al, wh) + _dot(ah, wl)


def _ada(cc, w_ada, b_ada):
    depth = w_ada.shape[0]
    n = w_ada.shape[2]
    tn = 768
    return pl.pallas_call(
        _ada_kernel,
        out_shape=jax.ShapeDtypeStruct((depth, 16, n), F32),
        grid=(depth, n // tn),
        in_specs=[
            pl.BlockSpec((16, D_MODEL), lambda l, j: (0, 0)),
            pl.BlockSpec((None, D_MODEL, tn), lambda l, j: (l, 0, j)),
            pl.BlockSpec((None, 1, tn), lambda l, j: (l, 0, j)),
        ],
        out_specs=pl.BlockSpec((None, 16, tn), lambda l, j: (l, 0, j)),
        compiler_params=pltpu.CompilerParams(dimension_semantics=("arbitrary", "arbitrary")),
        name="ada_mod",
    )(cc, w_ada, b_ada.reshape(depth, 1, n))


def _mod_rows(mod_ref, tile, tiles_per_b, rows, which):
    b = tile // tiles_per_b
    lo = which * D_MODEL
    ml = mod_ref[pl.ds(b, 1), lo:lo + D_MODEL]
    mc = mod_ref[8:9, lo:lo + D_MODEL]
    rid = lax.broadcasted_iota(jnp.int32, (rows, 1), 0)
    is_ctx = jnp.logical_and(tile % tiles_per_b == 0, rid < CTX_LEN)
    return jnp.where(is_ctx, mc, ml)


def _rms_rows(x, g):
    ms = jnp.mean(x * x, axis=-1, keepdims=True)
    return x * lax.rsqrt(ms + EPS) * g


def _in_proj_kernel(x_ref, mod_ref, g_ref, w_ref, p_ref, hz_ref, hz_s, *, tiles_per_b):
    i = pl.program_id(0)
    j = pl.program_id(1)

    @pl.when(j == 0)
    def _():
        tm = x_ref.shape[0]
        y = _rms_rows(x_ref[...], g_ref[...])
        shift = _mod_rows(mod_ref, i, tiles_per_b, tm, 0)
        scale = _mod_rows(mod_ref, i, tiles_per_b, tm, 1)
        hz = (y * (1.0 + scale) + shift).astype(BF16)
        hz_s[...] = hz
        hz_ref[...] = hz

    p_ref[...] = _dot(hz_s[...], w_ref[...])


def _in_proj(xs, mod, g, w_in_p, seq_len):
    t = xs.shape[0]
    tiles_per_b = seq_len // TM
    return pl.pallas_call(
        functools.partial(_in_proj_kernel, tiles_per_b=tiles_per_b),
        out_shape=(jax.ShapeDtypeStruct((t, D_IN_PAD), F32),
                   jax.ShapeDtypeStruct((t, D_MODEL), BF16)),
        grid=(t // TM, D_IN_PAD // TN_IN),
        in_specs=[
            pl.BlockSpec((TM, D_MODEL), lambda i, j: (i, 0)),
            pl.BlockSpec((16, 6 * D_MODEL), lambda i, j: (0, 0)),
            pl.BlockSpec((1, D_MODEL), lambda i, j: (0, 0)),
            pl.BlockSpec((D_MODEL, TN_IN), lambda i, j: (0, j)),
        ],
        out_specs=(pl.BlockSpec((TM, TN_IN), lambda i, j: (i, j)),
                   pl.BlockSpec((TM, D_MODEL), lambda i, j: (i, 0))),
        scratch_shapes=[pltpu.VMEM((TM, D_MODEL), BF16)],
        compiler_params=pltpu.CompilerParams(
            dimension_semantics=("arbitrary", "arbitrary"), vmem_limit_bytes=VMEM_LIMIT),
        name="in_proj",
    )(xs, mod, g.reshape(1, D_MODEL), w_in_p)


def _group_rms(x, g, ones_bd, group):
    ss = _split_dot(x * x, ones_bd)
    return x * lax.rsqrt(ss * (1.0 / group) + EPS) * g


def _softmax_parts(parts):
    m = parts[0].max(axis=-1, keepdims=True)
    for s in parts[1:]:
        m = jnp.maximum(m, s.max(axis=-1, keepdims=True))
    es = [jnp.exp(s - m) for s in parts]
    l = es[0].sum(axis=-1, keepdims=True)
    for e in es[1:]:
        l = l + e.sum(axis=-1, keepdims=True)
    return es, l


def _na_kernel(q_ref, k_ref, v_ref, gain_ref, bias_ref, o_ref, qn_s, kn_s, v_s):
    seq = q_ref.shape[0]
    rows = (seq - CTX_LEN) // GRID_W
    ones_bd = _group_ones(128, HEAD_DIM)
    gq = gain_ref[0:1, :]
    gk = gain_ref[1:2, :]
    scale = HEAD_DIM ** -0.5
    blk = 256
    for c in range(seq // blk):
        sl = slice(c * blk, (c + 1) * blk)
        qn_s[sl, :] = (_group_rms(q_ref[sl, :], gq, ones_bd, HEAD_DIM) * scale).astype(BF16)
        kn_s[sl, :] = _group_rms(k_ref[sl, :], gk, ones_bd, HEAD_DIM).astype(BF16)
        v_s[sl, :] = v_ref[sl, :].astype(BF16)

    lane = lax.broadcasted_iota(jnp.int32, (1, 128), 1)
    head_mask = [lane < HEAD_DIM, lane >= HEAD_DIM]
    kc = kn_s[0:CTX_LEN, :]
    vc = v_s[0:CTX_LEN, :]

    qc = qn_s[0:CTX_LEN, :]
    outs = []
    for h in range(2):
        qm = jnp.where(head_mask[h], qc, jnp.zeros_like(qc))
        (e,), l = _softmax_parts([_dot_nt(qm, kc)])
        outs.append(_dot(e.astype(BF16), vc) / l)
    o_ref[0:CTX_LEN, :] = jnp.where(head_mask[0], outs[0], outs[1]).astype(o_ref.dtype)

    win = NA_WIN_R * GRID_W

    def row_step(r, carry):
        r0 = jnp.clip(r - NA_WIN_R // 2, 0, rows - NA_WIN_R)
        d0 = r0 - r + NA_WIN_R - 1
        q0 = pl.multiple_of(CTX_LEN + r * GRID_W, GRID_W)
        k0 = pl.multiple_of(CTX_LEN + r0 * GRID_W, GRID_W)
        qr = qn_s[pl.ds(q0, GRID_W), :]
        kw = kn_s[pl.ds(k0, win), :]
        vw = v_s[pl.ds(k0, win), :]
        res = []
        for h in range(2):
            qm = jnp.where(head_mask[h], qr, jnp.zeros_like(qr))
            s_nb = _dot_nt(qm, kw) + bias_ref[h, d0]
            s_cx = _dot_nt(qm, kc)
            (e_nb, e_cx), l = _softmax_parts([s_nb, s_cx])
            o = _dot(e_nb.astype(BF16), vw) + _dot(e_cx.astype(BF16), vc)
            res.append(o / l)
        o_ref[pl.ds(q0, GRID_W), :] = jnp.where(head_mask[0], res[0], res[1]).astype(o_ref.dtype)
        return carry

    lax.fori_loop(0, rows, row_step, 0)


def _na_bias_table(rpb):
    col = np.arange(GRID_W)
    c0 = np.clip(col - NA_WIN_C // 2, 0, GRID_W - NA_WIN_C)
    kc = np.arange(GRID_W)
    valid = (kc[None, :] >= c0[:, None]) & (kc[None, :] < c0[:, None] + NA_WIN_C)
    dc = np.clip(kc[None, :] - col[:, None] + NA_WIN_C - 1, 0, 2 * NA_WIN_C - 2)
    dr = np.arange(NA_WIN_R)[:, None] + np.arange(NA_WIN_R)[None, :]
    tab = rpb[:, dr[:, :, None, None], dc[None, None, :, :]]
    tab = jnp.where(valid[None, None, None], tab, NEG_BIG)
    tab = jnp.transpose(tab, (0, 1, 3, 2, 4))
    return tab.reshape(NA_HEADS, NA_WIN_R, GRID_W, NA_WIN_R * GRID_W).astype(F32)


def _na_branch(p3, qk_gain, rpb):
    b, seq, _ = p3.shape
    gain2 = jnp.tile(qk_gain, (1, 2))
    bias = _na_bias_table(rpb).reshape(NA_HEADS // 2, 2, NA_WIN_R, GRID_W, NA_WIN_R * GRID_W)
    nb = BRANCH_W // 128
    return pl.pallas_call(
        _na_kernel,
        out_shape=jax.ShapeDtypeStruct((b, seq, BRANCH_W), BF16),
        grid=(nb, b),
        in_specs=[
            pl.BlockSpec((None, seq, 128), lambda hp, i: (i, 0, hp)),
            pl.BlockSpec((None, seq, 128), lambda hp, i: (i, 0, nb + hp)),
            pl.BlockSpec((None, seq, 128), lambda hp, i: (i, 0, 2 * nb + hp)),
            pl.BlockSpec((2, 128), lambda hp, i: (0, 0)),
            pl.BlockSpec((None, 2, NA_WIN_R, GRID_W, NA_WIN_R * GRID_W), lambda hp, i: (hp, 0, 0, 0, 0)),
        ],
        out_specs=pl.BlockSpec((None, seq, 128), lambda hp, i: (i, 0, hp)),
        scratch_shapes=[pltpu.VMEM((seq, 128), BF16), pltpu.VMEM((seq, 128), BF16),
                        pltpu.VMEM((seq, 128), BF16)],
        compiler_params=pltpu.CompilerParams(
            dimension_semantics=("arbitrary", "arbitrary"), vmem_limit_bytes=VMEM_LIMIT),
        name="na_branch",
    )(p3, p3, p3, gain2, bias)


DIFF_TQ = 256


def _rope_lanes(x, cos, sin_signed, half, first_mask):
    n = x.shape[-1]
    fwd = pltpu.roll(x, n - half, 1)
    bwd = pltpu.roll(x, half, 1)
    return x * cos + jnp.where(first_mask, fwd, bwd) * sin_signed


def _diff_kernel(q_ref, k_ref, v_ref, gain_ref, cos_ref, sin_ref, lam_ref, sub_ref, o_ref,
                 qn_s, kn_s, v_s, *, lam_init):
    seq = q_ref.shape[0]
    ones_bd = _group_ones(128, HEAD_DIM)
    gq = gain_ref[0:1, :]
    gk = gain_ref[1:2, :]
    scale = HEAD_DIM ** -0.5
    lane = lax.broadcasted_iota(jnp.int32, (1, 128), 1)
    first = (lane % 32) < 16
    blk = 256
    for c in range(seq // blk):
        sl = slice(c * blk, (c + 1) * blk)
        qn = _group_rms(q_ref[sl, :], gq, ones_bd, HEAD_DIM)
        kn = _group_rms(k_ref[sl, :], gk, ones_bd, HEAD_DIM)
        if c * blk >= CTX_LEN:
            ps = slice(c * blk - CTX_LEN, (c + 1) * blk - CTX_LEN)
            qn = _rope_lanes(qn, cos_ref[ps, :], sin_ref[ps, :], 16, first)
            kn = _rope_lanes(kn, cos_ref[ps, :], sin_ref[ps, :], 16, first)
        qn_s[sl, :] = (qn * scale).astype(BF16)
        kn_s[sl, :] = kn.astype(BF16)
        v_s[sl, :] = v_ref[sl, :].astype(BF16)

    lp = lam_ref[...]
    lam = (jnp.exp(jnp.sum(lp[0:1, :] * lp[1:2, :], axis=-1, keepdims=True))
           - jnp.exp(jnp.sum(lp[2:3, :] * lp[3:4, :], axis=-1, keepdims=True)) + lam_init)
    comp_mask = [lane < HEAD_DIM, lane >= HEAD_DIM]
    sub = sub_ref[...]

    def attend(q, keys, vals):
        ps = []
        for c in range(2):
            qm = jnp.where(comp_mask[c], q, jnp.zeros_like(q))
            (e,), l = _softmax_parts([_dot_nt(qm, keys)])
            ps.append((e, l))
        a = ps[0][0] * (1.0 / ps[0][1]) - ps[1][0] * (lam / ps[1][1])
        o = _dot(a.astype(BF16), vals)
        return _rms_rows(o, sub) * (1.0 - lam_init)

    o_ref[0:CTX_LEN, :] = attend(qn_s[0:CTX_LEN, :], kn_s[0:CTX_LEN, :], v_s[0:CTX_LEN, :]).astype(o_ref.dtype)

    def q_step(i, carry):
        q0 = pl.multiple_of(CTX_LEN + i * DIFF_TQ, DIFF_TQ)
        o_ref[pl.ds(q0, DIFF_TQ), :] = attend(qn_s[pl.ds(q0, DIFF_TQ), :], kn_s[...], v_s[...]).astype(o_ref.dtype)
        return carry

    lax.fori_loop(0, (seq - CTX_LEN) // DIFF_TQ, q_step, 0)


def _rope_angles(pos, dim):
    inv = ROPE_BASE ** (-np.arange(0, dim, 2, dtype=np.float32) / dim)
    return pos.astype(np.float32)[:, None] * inv[None, :]


def _diff_rope_tables(s):
    t = np.arange(s)
    ang_r = _rope_angles(t // GRID_W, HEAD_DIM // 2)
    ang_c = _rope_angles(t % GRID_W, HEAD_DIM // 2)
    ang64 = np.concatenate([ang_r, ang_r, ang_c, ang_c], axis=1)
    sign64 = np.concatenate([-np.ones(16), np.ones(16), -np.ones(16), np.ones(16)]).astype(np.float32)
    ang = jnp.asarray(np.tile(ang64, (1, 2)))
    return jnp.cos(ang), jnp.sin(ang) * jnp.asarray(np.tile(sign64, 2))[None, :]


def _diff_branch(p3, qk_gain, lam_p, subln, lam_init):
    b, seq, _ = p3.shape
    s = seq - CTX_LEN
    cos, sin = _diff_rope_tables(s)
    gain2 = jnp.tile(qk_gain, (1, 2))
    base = 3072 // 128
    nh = DIFF_HEADS
    return pl.pallas_call(
        functools.partial(_diff_kernel, lam_init=lam_init),
        out_shape=jax.ShapeDtypeStruct((b, seq, BRANCH_W), BF16),
        grid=(b, nh),
        in_specs=[
            pl.BlockSpec((None, seq, 128), lambda i, h: (i, 0, base + h)),
            pl.BlockSpec((None, seq, 128), lambda i, h: (i, 0, base + nh + h)),
            pl.BlockSpec((None, seq, 128), lambda i, h: (i, 0, base + 2 * nh + h)),
            pl.BlockSpec((2, 128), lambda i, h: (0, 0)),
            pl.BlockSpec((s, 128), lambda i, h: (0, 0)),
            pl.BlockSpec((s, 128), lambda i, h: (0, 0)),
            pl.BlockSpec((4, HEAD_DIM), lambda i, h: (0, 0)),
            pl.BlockSpec((1, 128), lambda i, h: (0, 0)),
        ],
        out_specs=pl.BlockSpec((None, seq, 128), lambda i, h: (i, 0, h)),
        scratch_shapes=[pltpu.VMEM((seq, 128), BF16), pltpu.VMEM((seq, 128), BF16),
                        pltpu.VMEM((seq, 128), BF16)],
        compiler_params=pltpu.CompilerParams(
            dimension_semantics=("arbitrary", "arbitrary"), vmem_limit_bytes=VMEM_LIMIT),
        name="diff_branch",
    )(p3, p3, p3, gain2, cos, sin, lam_p, subln.reshape(1, 128))


def _log_sigmoid(x):
    return jnp.minimum(x, 0.0) - jnp.log(1.0 + jnp.exp(-jnp.abs(x)))


def _ret_kernel(q_ref, k_ref, v_ref, g_ref, cos_ref, sin_ref, dec_ref, gn_ref, o_ref,
                q_s, k_s, o_s):
    seq = q_ref.shape[0]
    c = RET_CHUNK
    n_ctx_chunks = CTX_LEN // c
    n_chunks = seq // c
    hw = RET_HEADS * HEAD_DIM
    vw = RET_HEADS * 2 * HEAD_DIM
    lane = lax.broadcasted_iota(jnp.int32, (1, hw), 1)
    first = (lane % HEAD_DIM) < (HEAD_DIM // 2)
    blk = 256
    kscale = HEAD_DIM ** -0.5
    for cb in range(seq // blk):
        sl = slice(cb * blk, (cb + 1) * blk)
        q = q_ref[sl, :]
        k = k_ref[sl, :]
        if cb * blk >= CTX_LEN:
            ps = slice(cb * blk - CTX_LEN, (cb + 1) * blk - CTX_LEN)
            q = _rope_lanes(q, cos_ref[ps, :], sin_ref[ps, :], HEAD_DIM // 2, first)
            k = _rope_lanes(k, cos_ref[ps, :], sin_ref[ps, :], HEAD_DIM // 2, first)
        q_s[sl, :] = q
        k_s[sl, :] = k * kscale

    log_g = _log_sigmoid(dec_ref[...])
    pos_r = lax.broadcasted_iota(jnp.int32, (c, c), 0).astype(F32)
    pos_c = lax.broadcasted_iota(jnp.int32, (c, c), 1).astype(F32)
    pos = lax.broadcasted_iota(jnp.int32, (c, 1), 0).astype(F32)
    head_of_q = lax.broadcasted_iota(jnp.int32, (1, hw), 1) // HEAD_DIM
    head_of_v = lax.broadcasted_iota(jnp.int32, (1, vw), 1) // (2 * HEAD_DIM)
    row_head = lax.broadcasted_iota(jnp.int32, (hw, vw), 0) // HEAD_DIM
    col_head = lax.broadcasted_iota(jnp.int32, (hw, vw), 1) // (2 * HEAD_DIM)
    state_mask = row_head == col_head

    intra, q_dec, k_dec, c_dec = [], [], [], []
    for d in range(2):
        rel = (pos_r - pos_c) if d == 0 else (pos_c - pos_r)
        lg_q = jnp.zeros((1, hw), F32)
        lg_v = jnp.zeros((1, vw), F32)
        per_head = []
        for h in range(RET_HEADS):
            lg = log_g[d:d + 1, h:h + 1]
            per_head.append(jnp.where(rel >= 0, jnp.exp(lg * jnp.maximum(rel, 0.0)), 0.0))
            lg_q = jnp.where(head_of_q == h, lg, lg_q)
            lg_v = jnp.where(head_of_v == h, lg, lg_v)
        intra.append(per_head)
        qpos = (pos + 1.0) if d == 0 else (c - pos)
        kpos = (c - 1.0 - pos) if d == 0 else pos
        q_dec.append(jnp.exp(lg_q * qpos))
        k_dec.append(jnp.exp(lg_q * kpos))
        c_dec.append(jnp.exp(lg_v * c))

    head_mask = [head_of_q == h for h in range(RET_HEADS)]

    def chunk_start(d, n):
        if d == 0:
            return n * c
        rev_ctx = (n_ctx_chunks - 1 - n) * c
        rev_lat = CTX_LEN + (n_chunks - 1 - n) * c
        return jnp.where(n < n_ctx_chunks, rev_ctx, rev_lat)

    def step(n, states):
        new_states = []
        for d in range(2):
            r0 = pl.multiple_of(chunk_start(d, n), c)
            s = states[d]
            qi = q_s[pl.ds(r0, c), :]
            ki = k_s[pl.ds(r0, c), :]
            vi = v_ref[pl.ds(r0, c), :].astype(BF16)
            kb = ki.astype(BF16)
            cross = _dot((qi * q_dec[d]).astype(BF16), s.astype(BF16))
            parts = []
            for h in range(RET_HEADS):
                qm = jnp.where(head_mask[h], qi, 0.0).astype(BF16)
                att = _dot_nt(qm, kb) * intra[d][h]
                parts.append(_dot(att.astype(BF16), vi[:, h * 128:(h + 1) * 128]))
            o = jnp.concatenate(parts, axis=1) + cross
            o_s[pl.ds(r0, c), :] = o_s[pl.ds(r0, c), :] + o
            upd = _dot_tn((ki * k_dec[d]).astype(BF16), vi)
            new_states.append(s * c_dec[d] + jnp.where(state_mask, upd, 0.0))
        return tuple(new_states)

    o_s[...] = jnp.zeros(o_s.shape, F32)
    zero = jnp.zeros((hw, vw), F32)
    lax.fori_loop(0, n_chunks, step, (zero, zero))

    gn = gn_ref[...]
    for cb in range(seq // blk):
        sl = slice(cb * blk, (cb + 1) * blk)
        outs = []
        for h in range(RET_HEADS):
            hs = slice(h * 128, (h + 1) * 128)
            y = o_s[sl, hs]
            yc = y - jnp.mean(y, axis=-1, keepdims=True)
            yn = yc * lax.rsqrt(jnp.mean(yc * yc, axis=-1, keepdims=True) + EPS) * gn
            outs.append(yn * _silu(g_ref[sl, hs]))
        o_ref[sl, :] = jnp.concatenate(outs, axis=1).astype(o_ref.dtype)


def _ret_rope_tables(s):
    ang = _rope_angles(np.arange(s), HEAD_DIM)
    ang64 = np.concatenate([ang, ang], axis=1)
    sign64 = np.concatenate([-np.ones(32), np.ones(32)]).astype(np.float32)
    ang = jnp.asarray(np.tile(ang64, (1, RET_HEADS)))
    return jnp.cos(ang), jnp.sin(ang) * jnp.asarray(np.tile(sign64, RET_HEADS))[None, :]


def _ret_branch(p3, decay, gn_gain):
    b, seq, _ = p3.shape
    s = seq - CTX_LEN
    cos, sin = _ret_rope_tables(s)
    return pl.pallas_call(
        _ret_kernel,
        out_shape=jax.ShapeDtypeStruct((b, seq, BRANCH_W), BF16),
        grid=(b,),
        in_specs=[
            pl.BlockSpec((None, seq, 256), lambda i: (i, 0, 6)),
            pl.BlockSpec((None, seq, 256), lambda i: (i, 0, 7)),
            pl.BlockSpec((None, seq, 512), lambda i: (i, 0, 4)),
            pl.BlockSpec((None, seq, 512), lambda i: (i, 0, 5)),
            pl.BlockSpec((s, 256), lambda i: (0, 0)),
            pl.BlockSpec((s, 256), lambda i: (0, 0)),
            pl.BlockSpec((2, RET_HEADS), lambda i: (0, 0)),
            pl.BlockSpec((1, 128), lambda i: (0, 0)),
        ],
        out_specs=pl.BlockSpec((None, seq, BRANCH_W), lambda i: (i, 0, 0)),
        scratch_shapes=[pltpu.VMEM((seq, 256), F32), pltpu.VMEM((seq, 256), F32),
                        pltpu.VMEM((seq, 512), F32)],
        compiler_params=pltpu.CompilerParams(
            dimension_semantics=("arbitrary",), vmem_limit_bytes=VMEM_LIMIT),
        name="ret_branch",
    )(p3, p3, p3, p3, cos, sin, decay, gn_gain.reshape(1, 128))


DN_PAD = 8
DN_ROWS = 4 * DN_CHUNK


def _softplus(x):
    return jnp.maximum(x, 0.0) + jnp.log(1.0 + jnp.exp(-jnp.abs(x)))


def _dn_conv_silu(pad_s, w_ref, seq, width):
    blk = 256
    outs = []
    for cb in range(seq // blk):
        base = DN_PAD + cb * blk + (DN_PAD if cb * blk >= CTX_LEN else 0)
        ext = pad_s[base - DN_PAD:base + blk + DN_PAD, 0:width]
        n = blk + 2 * DN_PAD
        acc = ext * w_ref[DN_CONV // 2:DN_CONV // 2 + 1, :]
        for j in range(DN_CONV):
            if j == DN_CONV // 2:
                continue
            acc = acc + pltpu.roll(ext, (DN_CONV // 2 - j) % n, 0) * w_ref[j:j + 1, :]
        outs.append(_silu(acc[DN_PAD:DN_PAD + blk, :]))
    return outs


def _dn_fill_pad(pad_s, refs, seq):
    off = 0
    for ref in refs:
        w = ref.shape[1]
        pad_s[DN_PAD:DN_PAD + CTX_LEN, off:off + w] = ref[0:CTX_LEN, :]
        pad_s[2 * DN_PAD + CTX_LEN:2 * DN_PAD + seq, off:off + w] = ref[CTX_LEN:seq, :]
        off += w


def _dn_kernel(q_ref, k_ref, v_ref, g_ref, ab_ref, wq_ref, wk_ref, wv_ref, par_ref, ng_ref, o_ref,
               pad_s, qk_s, v_s, gb_s, o_s):
    seq = q_ref.shape[0]
    hp = pl.program_id(1)
    c = DN_CHUNK
    n_ctx_chunks = CTX_LEN // c
    n_chunks = seq // c
    blk = 256

    zeros_pad = jnp.zeros((DN_PAD, 256), F32)
    pad_s[0:DN_PAD, :] = zeros_pad
    pad_s[DN_PAD + CTX_LEN:2 * DN_PAD + CTX_LEN, :] = zeros_pad
    pad_s[2 * DN_PAD + seq:3 * DN_PAD + seq, :] = zeros_pad
    ones_bd = _group_ones(128, HEAD_DIM)

    _dn_fill_pad(pad_s, [q_ref], seq)
    for cb, x in enumerate(_dn_conv_silu(pad_s, wq_ref, seq, 128)):
        ss = _split_dot(x * x, ones_bd)
        qk_s[cb * blk:(cb + 1) * blk, 0:128] = x * lax.rsqrt(ss + EPS) * (HEAD_DIM ** -0.5)
    _dn_fill_pad(pad_s, [k_ref], seq)
    for cb, x in enumerate(_dn_conv_silu(pad_s, wk_ref, seq, 128)):
        ss = _split_dot(x * x, ones_bd)
        qk_s[cb * blk:(cb + 1) * blk, 128:256] = x * lax.rsqrt(ss + EPS)
    _dn_fill_pad(pad_s, [v_ref], seq)
    for cb, x in enumerate(_dn_conv_silu(pad_s, wv_ref, seq, 256)):
        v_s[cb * blk:(cb + 1) * blk, :] = x

    lane = lax.broadcasted_iota(jnp.int32, (1, 128), 1)
    a_log = par_ref[0:1, :]
    dt_bias = par_ref[1:2, :]
    shift = (128 - 2 * hp) % 128
    for cb in range(seq // blk):
        ab = ab_ref[cb * blk:(cb + 1) * blk, :]
        gdec = -jnp.exp(a_log) * _softplus(ab + dt_bias)
        beta = _sigmoid(ab)
        gb = jnp.where(lane < 2 * DN_HEADS, gdec, beta)
        gb_s[cb * blk:(cb + 1) * blk, :] = pltpu.roll(gb, shift, 1)

    ri = lax.broadcasted_iota(jnp.int32, (DN_ROWS, DN_ROWS), 0)
    ci = lax.broadcasted_iota(jnp.int32, (DN_ROWS, DN_ROWS), 1)
    same_blk = (ri // c) == (ci // c)
    is_fwd = ri < 2 * c
    rp, cp = ri % c, ci % c
    is_bwd = jnp.logical_not(is_fwd)
    strict = jnp.logical_and(same_blk, jnp.logical_or(jnp.logical_and(is_fwd, rp > cp),
                                                      jnp.logical_and(is_bwd, rp < cp)))
    incl = jnp.logical_or(strict, ri == ci)
    eye = jnp.where(ri == ci, 1.0, 0.0)
    r64 = lax.broadcasted_iota(jnp.int32, (c, c), 0)
    c64 = lax.broadcasted_iota(jnp.int32, (c, c), 1)
    tri_lo = jnp.where(r64 >= c64, 1.0, 0.0).astype(BF16)
    tri_up = jnp.where(r64 <= c64, 1.0, 0.0).astype(BF16)
    ones64 = jnp.ones((c, c), BF16)

    def split_left(m, x):
        hi = x.astype(BF16)
        lo = (x - hi.astype(F32)).astype(BF16)
        return _dot(m, hi) + _dot(m, lo)

    def chunk_start(d, n):
        if d == 0:
            return n * c
        rev_ctx = (n_ctx_chunks - 1 - n) * c
        rev_lat = CTX_LEN + (n_chunks - 1 - n) * c
        return jnp.where(n < n_ctx_chunks, rev_ctx, rev_lat)

    def stacked_cols(mats, cols, width):
        return jnp.concatenate(
            [jnp.broadcast_to(mats[d][:, cols[d][h]:cols[d][h] + 1], (c, width))
             for d in range(2) for h in range(2)], axis=0)

    def step(n, s):
        r0 = [pl.multiple_of(chunk_start(d, n), c) for d in range(2)]
        qk = [qk_s[pl.ds(r0[d], c), :] for d in range(2)]
        vv = [v_s[pl.ds(r0[d], c), :] for d in range(2)]
        gb = [gb_s[pl.ds(r0[d], c), :] for d in range(2)]
        gc = [split_left(tri_lo, gb[0]), split_left(tri_up, gb[1])]
        gt = [split_left(ones64, gb[d]) for d in range(2)]
        g_cols = [[0, 1], [4, 5]]
        b_cols = [[8, 9], [12, 13]]
        gcb = stacked_cols(gc, g_cols, DN_ROWS)
        gtb = stacked_cols(gt, g_cols, DN_ROWS)
        bb = stacked_cols(gb, b_cols, DN_ROWS)

        qrow = jnp.concatenate([qk[d][:, 0:128] for d in range(2) for _ in range(2)], axis=0)
        krow = jnp.concatenate([qk[d][:, 128:256] for d in range(2) for _ in range(2)], axis=0)
        qx = jnp.where(same_blk, jnp.concatenate([qrow, qrow], axis=1), 0.0)
        kx = jnp.where(same_blk, jnp.concatenate([krow, krow], axis=1), 0.0)
        vx = jnp.concatenate([vv[d][:, h * 128:(h + 1) * 128] for d in range(2) for h in range(2)], axis=0)

        kxb = kx.astype(BF16)
        kk = _dot_nt(kxb, kxb)
        qkm = _dot_nt(qx.astype(BF16), kxb)
        diff = gcb - jnp.transpose(gcb)
        decay = jnp.where(incl, jnp.exp(jnp.where(incl, diff, 0.0)), 0.0)
        lower = jnp.where(strict, bb * kk * decay, 0.0)

        m = -lower
        x = eye + m
        for _ in range(5):
            mb = m.astype(BF16)
            m = _dot(mb, mb)
            x = x + _dot(x.astype(BF16), m.astype(BF16))
        tb = x.astype(BF16)

        eg = jnp.exp(gcb)
        u = _dot(tb, (vx * bb[:, 0:128]).astype(BF16))
        w = _dot(tb, (kx * bb * eg).astype(BF16))
        sb = s.astype(BF16)
        v_new = u - _dot(w.astype(BF16), sb)
        o = _dot((qx * eg).astype(BF16), sb) + _dot((qkm * decay).astype(BF16), v_new.astype(BF16))
        k_tail = kx * jnp.exp(gtb - gcb)
        s_new = s * jnp.exp(gtb[:, 0:128]) + _dot_tn(k_tail.astype(BF16), v_new.astype(BF16))
        for d in range(2):
            od = jnp.concatenate([o[(2 * d + h) * c:(2 * d + h + 1) * c, :] for h in range(2)], axis=1)
            o_s[pl.ds(r0[d], c), :] = o_s[pl.ds(r0[d], c), :] + od
        return s_new

    o_s[...] = jnp.zeros(o_s.shape, F32)
    lax.fori_loop(0, n_chunks, step, jnp.zeros((DN_ROWS, 128), F32))

    ng = ng_ref[...]
    for cb in range(seq // blk):
        sl = slice(cb * blk, (cb + 1) * blk)
        outs = []
        for h in range(2):
            hs = slice(h * 128, (h + 1) * 128)
            outs.append(_rms_rows(o_s[sl, hs], ng) * _silu(g_ref[sl, hs]))
        o_ref[sl, :] = jnp.concatenate(outs, axis=1).astype(o_ref.dtype)


def _dn_branch(p3, conv_w, a_log, dt_bias, norm_gain):
    b, seq, _ = p3.shape
    par = jnp.zeros((2, 128), F32)
    par = par.at[0, 0:2 * DN_HEADS].set(a_log.reshape(-1)).at[1, 0:2 * DN_HEADS].set(dt_bias.reshape(-1))
    qb, kb_, vb, gb_, abb = 4608 // 128, 4864 // 128, 5120 // 256, 5632 // 256, 6144 // 128
    return pl.pallas_call(
        _dn_kernel,
        out_shape=jax.ShapeDtypeStruct((b, seq, BRANCH_W), BF16),
        grid=(b, 2),
        in_specs=[
            pl.BlockSpec((None, seq, 128), lambda i, hp: (i, 0, qb + hp)),
            pl.BlockSpec((None, seq, 128), lambda i, hp: (i, 0, kb_ + hp)),
            pl.BlockSpec((None, seq, 256), lambda i, hp: (i, 0, vb + hp)),
            pl.BlockSpec((None, seq, 256), lambda i, hp: (i, 0, gb_ + hp)),
            pl.BlockSpec((None, seq, 128), lambda i, hp: (i, 0, abb)),
            pl.BlockSpec((DN_CONV, 128), lambda i, hp: (0, hp)),
            pl.BlockSpec((DN_CONV, 128), lambda i, hp: (0, 2 + hp)),
            pl.BlockSpec((DN_CONV, 256), lambda i, hp: (0, 2 + hp)),
            pl.BlockSpec((2, 128), lambda i, hp: (0, 0)),
            pl.BlockSpec((1, 128), lambda i, hp: (0, 0)),
        ],
        out_specs=pl.BlockSpec((None, seq, 256), lambda i, hp: (i, 0, hp)),
        scratch_shapes=[pltpu.VMEM((seq + 3 * DN_PAD, 256), F32), pltpu.VMEM((seq, 256), F32),
                        pltpu.VMEM((seq, 256), F32), pltpu.VMEM((seq, 128), F32),
                        pltpu.VMEM((seq, 256), F32)],
        compiler_params=pltpu.CompilerParams(
            dimension_semantics=("arbitrary", "arbitrary"), vmem_limit_bytes=VMEM_LIMIT),
        name="dn_branch",
    )(p3, p3, p3, p3, p3, conv_w, conv_w, conv_w, par, norm_gain.reshape(1, 128))


def _combine_kernel(hz_ref, y0_ref, y1_ref, y2_ref, y3_ref, xs_ref, mod_ref, wg_ref, bg_ref, wb_ref,
                    wo_ref, gf_ref, xo_ref, hl_ref, *, tiles_per_b):
    i = pl.program_id(0)
    tm = xs_ref.shape[0]
    hz = hz_ref[...]
    acc = jnp.zeros((tm, D_MODEL), F32)
    for n, y_ref in enumerate((y0_ref, y1_ref, y2_ref, y3_ref)):
        cs = slice(n * D_MODEL, (n + 1) * D_MODEL)
        gate = _sigmoid(_dot(hz, wg_ref[:, cs]) + bg_ref[:, cs])
        acc = acc + gate * _dot(y_ref[...], wb_ref[n])
    mix = _dot(acc.astype(BF16), wo_ref[...])
    xn = xs_ref[...] + _mod_rows(mod_ref, i, tiles_per_b, tm, 2) * mix
    xo_ref[...] = xn
    h = _rms_rows(xn, gf_ref[...])
    hl_ref[...] = h * (1.0 + _mod_rows(mod_ref, i, tiles_per_b, tm, 4)) + _mod_rows(mod_ref, i, tiles_per_b, tm, 3)


def _combine(hz, ys, xs, mod, w_mgate, b_mgate, w_branch, w_out, g_ffn, seq_len):
    t = xs.shape[0]
    tiles_per_b = seq_len // TM
    tok = lambda i: (i, 0)
    fix2 = lambda i: (0, 0)
    return pl.pallas_call(
        functools.partial(_combine_kernel, tiles_per_b=tiles_per_b),
        out_shape=(jax.ShapeDtypeStruct((t, D_MODEL), F32), jax.ShapeDtypeStruct((t, D_MODEL), F32)),
        grid=(t // TM,),
        in_specs=[
            pl.BlockSpec((TM, D_MODEL), tok),
            pl.BlockSpec((TM, BRANCH_W), tok), pl.BlockSpec((TM, BRANCH_W), tok),
            pl.BlockSpec((TM, BRANCH_W), tok), pl.BlockSpec((TM, BRANCH_W), tok),
            pl.BlockSpec((TM, D_MODEL), tok),
            pl.BlockSpec((16, 6 * D_MODEL), fix2),
            pl.BlockSpec((D_MODEL, N_BRANCH * D_MODEL), fix2),
            pl.BlockSpec((1, N_BRANCH * D_MODEL), fix2),
            pl.BlockSpec((N_BRANCH, BRANCH_W, D_MODEL), lambda i: (0, 0, 0)),
            pl.BlockSpec((D_MODEL, D_MODEL), fix2),
            pl.BlockSpec((1, D_MODEL), fix2),
        ],
        out_specs=(pl.BlockSpec((TM, D_MODEL), tok), pl.BlockSpec((TM, D_MODEL), tok)),
        compiler_params=pltpu.CompilerParams(
            dimension_semantics=("arbitrary",), vmem_limit_bytes=VMEM_LIMIT),
        name="combine",
    )(hz, *ys, xs, mod, w_mgate, b_mgate.reshape(1, -1), w_branch, w_out, g_ffn.reshape(1, D_MODEL))


def _route_kernel(h_ref, w_ref, b_ref, topi_ref, gate_ref, rank_ref, cnt_ref, base_s):
    i = pl.program_id(0)

    @pl.when(i == 0)
    def _():
        base_s[...] = jnp.zeros(base_s.shape, F32)

    tm = h_ref.shape[0]
    logits = _split_dot_w(h_ref[...], w_ref[...]) + b_ref[...]
    lane = lax.broadcasted_iota(jnp.int32, (tm, 128), 1)
    l = logits
    idxs, vals, hots = [], [], []
    for _ in range(TOP_K):
        m = l.max(axis=-1, keepdims=True)
        idx = jnp.min(jnp.where(l == m, lane, 128), axis=-1, keepdims=True)
        hot = lane == idx
        idxs.append(idx)
        vals.append(m)
        hots.append(hot)
        l = jnp.where(hot, -3e38, l)
    es = [jnp.exp(v - vals[0]) for v in vals]
    den = es[0] + es[1] + es[2] + es[3]

    cnt = jnp.zeros((tm, 128), F32)
    for hot in hots:
        cnt = cnt + jnp.where(hot, 1.0, 0.0)
    r = lax.broadcasted_iota(jnp.int32, (tm, tm), 0)
    c = lax.broadcasted_iota(jnp.int32, (tm, tm), 1)
    before = jnp.where(r > c, 1.0, 0.0).astype(BF16)
    prior = _dot(before, cnt.astype(BF16)) + base_s[...]

    topi = jnp.zeros((tm, 128), jnp.int32)
    gate = jnp.zeros((tm, 128), F32)
    rank = jnp.zeros((tm, 128), jnp.int32)
    for k in range(TOP_K):
        rk = jnp.sum(jnp.where(hots[k], prior, 0.0), axis=-1, keepdims=True)
        topi = jnp.where(lane == k, idxs[k], topi)
        gate = jnp.where(lane == k, es[k] / den, gate)
        rank = jnp.where(lane == k, rk.astype(jnp.int32), rank)
    topi_ref[...] = topi
    gate_ref[...] = gate
    rank_ref[...] = rank
    base_s[...] = base_s[...] + jnp.sum(cnt, axis=0, keepdims=True)
    cnt_ref[...] = base_s[...]


def _route(hl, router_w, router_b):
    t = hl.shape[0]
    w = jnp.pad(router_w, ((0, 0), (0, 128 - N_EXPERTS)))
    b = jnp.pad(router_b, (0, 128 - N_EXPERTS), constant_values=NEG_BIG).reshape(1, 128)
    tok = lambda i: (i, 0)
    return pl.pallas_call(
        _route_kernel,
        out_shape=(jax.ShapeDtypeStruct((t, 128), jnp.int32), jax.ShapeDtypeStruct((t, 128), F32),
                   jax.ShapeDtypeStruct((t, 128), jnp.int32), jax.ShapeDtypeStruct((1, 128), F32)),
        grid=(t // TM_ROUTE,),
        in_specs=[pl.BlockSpec((TM_ROUTE, D_MODEL), tok),
                  pl.BlockSpec((D_MODEL, 128), lambda i: (0, 0)),
                  pl.BlockSpec((1, 128), lambda i: (0, 0))],
        out_specs=(pl.BlockSpec((TM_ROUTE, 128), tok), pl.BlockSpec((TM_ROUTE, 128), tok),
                   pl.BlockSpec((TM_ROUTE, 128), tok), pl.BlockSpec((1, 128), lambda i: (0, 0))),
        scratch_shapes=[pltpu.VMEM((1, 128), F32)],
        compiler_params=pltpu.CompilerParams(dimension_semantics=("arbitrary",)),
        name="route",
    )(hl, w, b)


def _dispatch_kernel(dest_ref, h_ref, rows_in_ref, rows_out_ref, sem):
    del rows_in_ref
    tm = h_ref.shape[0]

    def row_copy(t, d):
        return pltpu.make_async_copy(h_ref.at[pl.ds(t, 1)], rows_out_ref.at[pl.ds(d, 1)], sem)

    def issue(t, carry):
        for k in range(TOP_K):
            row_copy(t, dest_ref[t * TOP_K + k]).start()
        return carry

    def drain(t, carry):
        for k in range(TOP_K):
            row_copy(0, 0).wait()
        return carry

    lax.fori_loop(0, tm, issue, 0)
    lax.fori_loop(0, tm, drain, 0)


def _dispatch(dest_flat, hl, n_rows):
    t = hl.shape[0]
    rows0 = jnp.zeros((n_rows, D_MODEL), F32)
    return pl.pallas_call(
        _dispatch_kernel,
        out_shape=jax.ShapeDtypeStruct((n_rows, D_MODEL), F32),
        grid=(t // TM_DMA,),
        in_specs=[pl.BlockSpec((TM_DMA * TOP_K,), lambda i: (i,), memory_space=pltpu.SMEM),
                  pl.BlockSpec((TM_DMA, D_MODEL), lambda i: (i, 0)),
                  pl.BlockSpec(memory_space=pl.ANY)],
        out_specs=pl.BlockSpec(memory_space=pl.ANY),
        scratch_shapes=[pltpu.SemaphoreType.DMA(())],
        input_output_aliases={2: 0},
        compiler_params=pltpu.CompilerParams(dimension_semantics=("arbitrary",), has_side_effects=True),
        name="moe_dispatch",
    )(dest_flat, hl, rows0)


def _expert_kernel(be_ref, nu_ref, x_ref, wgu_ref, bgu_ref, wd_ref, bd_ref, o_ref):
    del be_ref
    j = pl.program_id(0)
    d_ff = wd_ref.shape[0]

    @pl.when(j < nu_ref[0])
    def _():
        gu = _dot(x_ref[...].astype(BF16), wgu_ref[...]) + bgu_ref[...]
        gate = jnp.minimum(gu[:, :d_ff], SWIGLU_LIMIT)
        up = jnp.clip(gu[:, d_ff:], -SWIGLU_LIMIT, SWIGLU_LIMIT)
        y = (up + 1.0) * (gate * _sigmoid(SWIGLU_ALPHA * gate))
        o_ref[...] = _dot(y.astype(BF16), wd_ref[...]) + bd_ref[...]

    @pl.when(j >= nu_ref[0])
    def _():
        o_ref[...] = jnp.zeros(o_ref.shape, F32)


def _experts(block_e, n_used, rows, w_gu, b_gu, w_down, b_down):
    n_rows = rows.shape[0]
    d_ff = w_down.shape[1]
    grid_spec = pltpu.PrefetchScalarGridSpec(
        num_scalar_prefetch=2,
        grid=(n_rows // MOE_BM,),
        in_specs=[
            pl.BlockSpec((MOE_BM, D_MODEL), lambda j, be, nu: (j, 0)),
            pl.BlockSpec((None, D_MODEL, 2 * d_ff), lambda j, be, nu: (be[j], 0, 0)),
            pl.BlockSpec((None, 1, 2 * d_ff), lambda j, be, nu: (be[j], 0, 0)),
            pl.BlockSpec((None, d_ff, D_MODEL), lambda j, be, nu: (be[j], 0, 0)),
            pl.BlockSpec((None, 1, D_MODEL), lambda j, be, nu: (be[j], 0, 0)),
        ],
        out_specs=pl.BlockSpec((MOE_BM, D_MODEL), lambda j, be, nu: (j, 0)),
    )
    return pl.pallas_call(
        _expert_kernel,
        out_shape=jax.ShapeDtypeStruct((n_rows, D_MODEL), F32),
        grid_spec=grid_spec,
        compiler_params=pltpu.CompilerParams(
            dimension_semantics=("arbitrary",), vmem_limit_bytes=VMEM_LIMIT),
        name="moe_experts",
    )(block_e, n_used, rows, w_gu, b_gu.reshape(N_EXPERTS, 1, -1), w_down, b_down.reshape(N_EXPERTS, 1, -1))


def _finish_kernel(dest_ref, gate_ref, xs_ref, mod_ref, rows_ref, o_ref, buf, sem, *, tiles_per_b):
    i = pl.program_id(0)
    tm = xs_ref.shape[0]

    def row_copy(t, k, d):
        return pltpu.make_async_copy(rows_ref.at[pl.ds(d, 1)], buf.at[k, pl.ds(t, 1)], sem)

    def issue(t, carry):
        for k in range(TOP_K):
            row_copy(t, k, dest_ref[t * TOP_K + k]).start()
        return carry

    def drain(t, carry):
        for k in range(TOP_K):
            row_copy(0, 0, 0).wait()
        return carry

    lax.fori_loop(0, tm, issue, 0)
    lax.fori_loop(0, tm, drain, 0)
    gate = gate_ref[...]
    y = gate[:, 0:1] * buf[0]
    for k in range(1, TOP_K):
        y = y + gate[:, k:k + 1] * buf[k]
    o_ref[...] = xs_ref[...] + _mod_rows(mod_ref, i, tiles_per_b, tm, 5) * y


def _finish(dest_flat, gates, xs, mod, out_rows, seq_len):
    t = xs.shape[0]
    tiles_per_b = seq_len // TM_DMA
    return pl.pallas_call(
        functools.partial(_finish_kernel, tiles_per_b=tiles_per_b),
        out_shape=jax.ShapeDtypeStruct((t, D_MODEL), F32),
        grid=(t // TM_DMA,),
        in_specs=[pl.BlockSpec((TM_DMA * TOP_K,), lambda i: (i,), memory_space=pltpu.SMEM),
                  pl.BlockSpec((TM_DMA, 128), lambda i: (i, 0)),
                  pl.BlockSpec((TM_DMA, D_MODEL), lambda i: (i, 0)),
                  pl.BlockSpec((16, 6 * D_MODEL), lambda i: (0, 0)),
                  pl.BlockSpec(memory_space=pl.ANY)],
        out_specs=pl.BlockSpec((TM_DMA, D_MODEL), lambda i: (i, 0)),
        scratch_shapes=[pltpu.VMEM((TOP_K, TM_DMA, D_MODEL), F32), pltpu.SemaphoreType.DMA(())],
        compiler_params=pltpu.CompilerParams(
            dimension_semantics=("arbitrary",), vmem_limit_bytes=VMEM_LIMIT),
        name="moe_finish",
    )(dest_flat, gates, xs, mod, out_rows)


def _moe_plan(topi, rank, counts):
    t = topi.shape[0]
    padded = (counts + MOE_BM - 1) // MOE_BM * MOE_BM
    pad_end = jnp.cumsum(padded)
    pad_start = pad_end - padded
    onehot = topi[:, :, None] == jnp.arange(N_EXPERTS, dtype=jnp.int32)[None, None, :]
    dest = jnp.sum(jnp.where(onehot, pad_start[None, None, :], 0), axis=-1) + rank
    n_blocks = -(-(t * TOP_K + N_EXPERTS * (MOE_BM - 1)) // MOE_BM)
    starts = jnp.arange(n_blocks, dtype=jnp.int32) * MOE_BM
    block_e = jnp.sum(starts[:, None] >= pad_end[None, :], axis=-1).astype(jnp.int32)
    block_e = jnp.minimum(block_e, N_EXPERTS - 1)
    n_used = (pad_end[-1] // MOE_BM).astype(jnp.int32).reshape(1)
    return dest.reshape(-1).astype(jnp.int32), block_e, n_used, n_blocks * MOE_BM


def kernel(x, c, ctx, c_ctx, w_ada, b_ada, norm_mix, norm_ffn, w_in, na_qk_gain, na_rpb, ret_decay, ret_gn,
           diff_qk_gain, diff_lam, diff_subln, dn_conv, dn_a_log, dn_dt_bias, dn_norm, w_branch, w_mgate,
           b_mgate, w_out, router_w, router_b, w_gu, b_gu, w_down, b_down):
    b, s, d = x.shape
    n_ctx = ctx.shape[1]
    seq = n_ctx + s
    depth = w_ada.shape[0]
    assert (d, n_ctx, s) == (D_MODEL, CTX_LEN, 2048) and b <= 8 and seq % TM == 0

    xs = jnp.concatenate([ctx, x], axis=1).reshape(b * seq, d)
    cc = jnp.zeros((16, d), F32).at[0:b].set(c).at[8].set(c_ctx)
    mods = _ada(cc, w_ada, b_ada)

    for l in range(depth):
        lam_init = 0.8 - 0.6 * math.exp(-0.3 * l)
        w_in_p = jnp.pad(w_in[l], ((0, 0), (0, D_IN_PAD - D_IN))).astype(BF16)
        p, hz = _in_proj(xs, mods[l], norm_mix[l], w_in_p, seq)
        p3 = p.reshape(b, seq, D_IN_PAD)
        ys = (_na_branch(p3, na_qk_gain[l], na_rpb[l]),
              _ret_branch(p3, ret_decay[l], ret_gn[l]),
              _diff_branch(p3, diff_qk_gain[l], diff_lam[l], diff_subln[l], lam_init),
              _dn_branch(p3, dn_conv[l], dn_a_log[l], dn_dt_bias[l], dn_norm[l]))
        ys = tuple(y.reshape(b * seq, BRANCH_W) for y in ys)
        xs, hl = _combine(hz, ys, xs, mods[l], w_mgate[l].astype(BF16), b_mgate[l], w_branch[l].astype(BF16),
                          w_out[l].astype(BF16), norm_ffn[l], seq)
        topi, gates, rank, cnt = _route(hl, router_w[l], router_b[l])
        counts = cnt[0, :N_EXPERTS].astype(jnp.int32)
        dest, block_e, n_used, n_rows = _moe_plan(topi[:, :TOP_K], rank[:, :TOP_K], counts)
        rows = _dispatch(dest, hl, n_rows)
        out_rows = _experts(block_e, n_used, rows, w_gu[l].astype(BF16), b_gu[l], w_down[l].astype(BF16), b_down[l])
        xs = _finish(dest, gates, xs, mods[l], out_rows, seq)

    return xs.reshape(b, seq, d)[:, n_ctx:]
```

```python
import functools
import math

import numpy as np
import jax
import jax.numpy as jnp
from jax import lax
from jax.experimental import pallas as pl
from jax.experimental.pallas import tpu as pltpu

F32 = jnp.float32
BF16 = jnp.bfloat16

D_MODEL = 1024
GRID_W = 64
CTX_LEN = 256
HEAD_DIM = 64
N_BRANCH = 4
BRANCH_W = 512
NA_HEADS = 8
NA_WIN_R = 8
NA_WIN_C = 16
RET_HEADS = 4
RET_CHUNK = 128
DIFF_HEADS = 4
DN_HEADS = 4
DN_CONV = 5
DN_CHUNK = 64
N_EXPERTS = 32
TOP_K = 4
SWIGLU_LIMIT = 7.0
SWIGLU_ALPHA = 1.702
ROPE_BASE = 10000.0
EPS = 1e-6
D_IN = 6160
D_IN_MAIN = 6144
NEG_BIG = -1e30

VMEM_LIMIT = 56 * 1024 * 1024

TM = 768
TN_IN = 768
TM_ROUTE = 256
TM_DMA = 256
MOE_BM = 256


def _dot(a, b):
    return jnp.dot(a, b, preferred_element_type=F32)


def _dot_nt(a, b):
    return lax.dot_general(a, b, (((1,), (1,)), ((), ())), preferred_element_type=F32)


def _dot_tn(a, b):
    return lax.dot_general(a, b, (((0,), (0,)), ((), ())), preferred_element_type=F32)


def _split_dot(x, w_bf16):
    hi = x.astype(BF16)
    lo = (x - hi.astype(F32)).astype(BF16)
    return _dot(hi, w_bf16) + _dot(lo, w_bf16)


def _group_ones(n, group):
    r = lax.broadcasted_iota(jnp.int32, (n, n), 0) // group
    c = lax.broadcasted_iota(jnp.int32, (n, n), 1) // group
    return jnp.where(r == c, 1.0, 0.0).astype(BF16)


def _sigmoid(x):
    return 1.0 / (1.0 + jnp.exp(-x))


def _silu(x):
    return x * _sigmoid(x)


def _ada_kernel(c_ref, w_ref, b_ref, o_ref):
    c = c_ref[...]
    a = _silu(c)
    o_ref[...] = _split_dot_w(a, w_ref[...]) + b_ref[...]


def _split_dot_w(a, w):
    ah = a.astype(BF16)
    al = (a - ah.astype(F32)).astype(BF16)
    wh = w.astype(BF16)
    wl = (w - wh.astype(F32)).astype(BF16)
    return _dot(ah, wh) + _dot(al, wh) + _dot(ah, wl)


def _ada(cc, w_ada, b_ada):
    depth = w_ada.shape[0]
    n = w_ada.shape[2]
    tn = 768
    return pl.pallas_call(
        _ada_kernel,
        out_shape=jax.ShapeDtypeStruct((depth, 16, n), F32),
        grid=(depth, n // tn),
        in_specs=[
            pl.BlockSpec((16, D_MODEL), lambda l, j: (0, 0)),
            pl.BlockSpec((None, D_MODEL, tn), lambda l, j: (l, 0, j)),
            pl.BlockSpec((None, 1, tn), lambda l, j: (l, 0, j)),
        ],
        out_specs=pl.BlockSpec((None, 16, tn), lambda l, j: (l, 0, j)),
        compiler_params=pltpu.CompilerParams(dimension_semantics=("arbitrary", "arbitrary")),
        name="ada_mod",
    )(cc, w_ada, b_ada.reshape(depth, 1, n))


def _mod_rows(mod_ref, tile, tiles_per_b, rows, which):
    b = tile // tiles_per_b
    lo = which * D_MODEL
    ml = mod_ref[pl.ds(b, 1), lo:lo + D_MODEL]
    mc = mod_ref[8:9, lo:lo + D_MODEL]
    rid = lax.broadcasted_iota(jnp.int32, (rows, 1), 0)
    is_ctx = jnp.logical_and(tile % tiles_per_b == 0, rid < CTX_LEN)
    return jnp.where(is_ctx, mc, ml)


def _rms_rows(x, g):
    ms = jnp.mean(x * x, axis=-1, keepdims=True)
    return x * lax.rsqrt(ms + EPS) * g


def _in_proj_kernel(x_ref, mod_ref, g_ref, w_ref, wab_ref, p_ref, pab_ref, hz_ref, hz_s, *, tiles_per_b):
    i = pl.program_id(0)
    j = pl.program_id(1)

    @pl.when(j == 0)
    def _():
        tm = x_ref.shape[0]
        y = _rms_rows(x_ref[...], g_ref[...])
        shift = _mod_rows(mod_ref, i, tiles_per_b, tm, 0)
        scale = _mod_rows(mod_ref, i, tiles_per_b, tm, 1)
        hz = (y * (1.0 + scale) + shift).astype(BF16)
        hz_s[...] = hz
        hz_ref[...] = hz
        pab_ref[...] = _dot(hz, wab_ref[...])

    p_ref[...] = _dot(hz_s[...], w_ref[...]).astype(p_ref.dtype)


def _in_proj(xs, mod, g, w_main, w_ab, seq_len):
    t = xs.shape[0]
    n = w_main.shape[1]
    tiles_per_b = seq_len // TM
    return pl.pallas_call(
        functools.partial(_in_proj_kernel, tiles_per_b=tiles_per_b),
        out_shape=(jax.ShapeDtypeStruct((t, n), BF16),
                   jax.ShapeDtypeStruct((t, 128), F32),
                   jax.ShapeDtypeStruct((t, D_MODEL), BF16)),
        grid=(t // TM, n // TN_IN),
        in_specs=[
            pl.BlockSpec((TM, D_MODEL), lambda i, j: (i, 0)),
            pl.BlockSpec((16, 6 * D_MODEL), lambda i, j: (0, 0)),
            pl.BlockSpec((1, D_MODEL), lambda i, j: (0, 0)),
            pl.BlockSpec((D_MODEL, TN_IN), lambda i, j: (0, j)),
            pl.BlockSpec((D_MODEL, 128), lambda i, j: (0, 0)),
        ],
        out_specs=(pl.BlockSpec((TM, TN_IN), lambda i, j: (i, j)),
                   pl.BlockSpec((TM, 128), lambda i, j: (i, 0)),
                   pl.BlockSpec((TM, D_MODEL), lambda i, j: (i, 0))),
        scratch_shapes=[pltpu.VMEM((TM, D_MODEL), BF16)],
        compiler_params=pltpu.CompilerParams(
            dimension_semantics=("arbitrary", "arbitrary"), vmem_limit_bytes=VMEM_LIMIT),
        name="in_proj",
    )(xs, mod, g.reshape(1, D_MODEL), w_main, w_ab)


def _group_rms(x, g, ones_bd, group):
    ss = _split_dot(x * x, ones_bd)
    return x * lax.rsqrt(ss * (1.0 / group) + EPS) * g


def _softmax_parts(parts):
    m = parts[0].max(axis=-1, keepdims=True)
    for s in parts[1:]:
        m = jnp.maximum(m, s.max(axis=-1, keepdims=True))
    es = [jnp.exp(s - m) for s in parts]
    l = es[0].sum(axis=-1, keepdims=True)
    for e in es[1:]:
        l = l + e.sum(axis=-1, keepdims=True)
    return es, l


NA_ROWS_PER_STEP = 4


def _na_bias_tiles(w_ref, tab_s):
    c_i = lax.broadcasted_iota(jnp.int32, (GRID_W, 128), 0)
    l_i = lax.broadcasted_iota(jnp.int32, (GRID_W, 128), 1)
    c0 = jnp.clip(c_i - NA_WIN_C // 2, 0, GRID_W - NA_WIN_C)
    in_lo = jnp.logical_and(l_i >= c0, l_i < c0 + NA_WIN_C)
    in_hi = jnp.logical_and(l_i - GRID_W >= c0, l_i - GRID_W < c0 + NA_WIN_C)
    for h in range(2):
        lo, hi = [], []
        for dr in range(2 * NA_WIN_R - 1):
            row = jnp.broadcast_to(w_ref[h, dr:dr + 1, :], (GRID_W, 128))
            lo.append(pltpu.roll(row, 128 - (GRID_W - 1), 1, stride=1, stride_axis=0))
            hi.append(pltpu.roll(row, 1, 1, stride=1, stride_axis=0))
        for d0 in range(NA_WIN_R):
            for j in range(NA_WIN_R // 2):
                tile = jnp.where(in_lo, lo[d0 + 2 * j], jnp.where(in_hi, hi[d0 + 2 * j + 1], NEG_BIG))
                tab_s[h, d0, :, j * 128:(j + 1) * 128] = tile


def _na_kernel(q_ref, k_ref, v_ref, gain_ref, w_ref, o_ref, qn_s, kn_s, tab_s):
    seq = q_ref.shape[0]
    rows = (seq - CTX_LEN) // GRID_W

    @pl.when(pl.program_id(1) == 0)
    def _():
        _na_bias_tiles(w_ref, tab_s)

    ones_bd = _group_ones(128, HEAD_DIM)
    gq = gain_ref[0:1, :]
    gk = gain_ref[1:2, :]
    scale = HEAD_DIM ** -0.5
    blk = 256
    for c in range(seq // blk):
        sl = slice(c * blk, (c + 1) * blk)
        qn_s[sl, :] = (_group_rms(q_ref[sl, :].astype(F32), gq, ones_bd, HEAD_DIM) * scale).astype(BF16)
        kn_s[sl, :] = _group_rms(k_ref[sl, :].astype(F32), gk, ones_bd, HEAD_DIM).astype(BF16)

    lane = lax.broadcasted_iota(jnp.int32, (1, 128), 1)
    head_mask = [lane < HEAD_DIM, lane >= HEAD_DIM]
    kc = kn_s[0:CTX_LEN, :]
    vc = v_ref[0:CTX_LEN, :]

    qc = qn_s[0:CTX_LEN, :]
    outs = []
    for h in range(2):
        qm = jnp.where(head_mask[h], qc, jnp.zeros_like(qc))
        (e,), l = _softmax_parts([_dot_nt(qm, kc)])
        outs.append(_dot(e.astype(BF16), vc) / l)
    o_ref[0:CTX_LEN, :] = jnp.where(head_mask[0], outs[0], outs[1]).astype(o_ref.dtype)

    win = NA_WIN_R * GRID_W

    half = NA_WIN_R // 2

    def load_row(r, r0):
        aligned = (lambda x: x) if isinstance(r, int) else (lambda x: pl.multiple_of(x, GRID_W))
        q0 = aligned(CTX_LEN + r * GRID_W)
        k0 = aligned(CTX_LEN + r0 * GRID_W)
        return q0, qn_s[pl.ds(q0, GRID_W), :], kn_s[pl.ds(k0, win), :], v_ref[pl.ds(k0, win), :]

    def attend_row(qr, kw, vw, d0):
        res = []
        for h in range(2):
            qm = jnp.where(head_mask[h], qr, jnp.zeros_like(qr))
            s_nb = _dot_nt(qm, kw) + tab_s[h, d0]
            s_cx = _dot_nt(qm, kc)
            (e_nb, e_cx), l = _softmax_parts([s_nb, s_cx])
            o = _dot(e_nb.astype(BF16), vw) + _dot(e_cx.astype(BF16), vc)
            res.append(o / l)
        return jnp.where(head_mask[0], res[0], res[1]).astype(o_ref.dtype)

    def do_rows(row_ids, r0s, d0s):
        loaded = [load_row(r, r0) for r, r0 in zip(row_ids, r0s)]
        outs = [attend_row(qr, kw, vw, d0) for (_, qr, kw, vw), d0 in zip(loaded, d0s)]
        for (q0, _, _, _), o in zip(loaded, outs):
            o_ref[pl.ds(q0, GRID_W), :] = o

    top = list(range(half))
    do_rows(top, [0] * half, [NA_WIN_R - 1 - r for r in top])
    bot = list(range(rows - half, rows))
    do_rows(bot, [rows - NA_WIN_R] * half, [rows - NA_WIN_R - r + NA_WIN_R - 1 for r in bot])

    def row_step(i, carry):
        ids = [half + i * NA_ROWS_PER_STEP + u for u in range(NA_ROWS_PER_STEP)]
        do_rows(ids, [r - half for r in ids], [half - 1] * NA_ROWS_PER_STEP)
        return carry

    assert (rows - 2 * half) % NA_ROWS_PER_STEP == 0
    lax.fori_loop(0, (rows - 2 * half) // NA_ROWS_PER_STEP, row_step, 0)


def _na_branch(p3, qk_gain, rpb):
    b, seq, _ = p3.shape
    gain2 = jnp.tile(qk_gain, (1, 2))
    lo = GRID_W - NA_WIN_C
    w = jnp.pad(rpb, ((0, 0), (0, 0), (lo, 128 - lo - rpb.shape[2])))
    nb = BRANCH_W // 128
    return pl.pallas_call(
        _na_kernel,
        out_shape=jax.ShapeDtypeStruct((b, seq, BRANCH_W), BF16),
        grid=(nb, b),
        in_specs=[
            pl.BlockSpec((None, seq, 128), lambda hp, i: (i, 0, hp)),
            pl.BlockSpec((None, seq, 128), lambda hp, i: (i, 0, nb + hp)),
            pl.BlockSpec((None, seq, 128), lambda hp, i: (i, 0, 2 * nb + hp)),
            pl.BlockSpec((2, 128), lambda hp, i: (0, 0)),
            pl.BlockSpec((2, 2 * NA_WIN_R - 1, 128), lambda hp, i: (hp, 0, 0)),
        ],
        out_specs=pl.BlockSpec((None, seq, 128), lambda hp, i: (i, 0, hp)),
        scratch_shapes=[pltpu.VMEM((seq, 128), BF16), pltpu.VMEM((seq, 128), BF16),
                        pltpu.VMEM((2, NA_WIN_R, GRID_W, NA_WIN_R * GRID_W), F32)],
        compiler_params=pltpu.CompilerParams(
            dimension_semantics=("arbitrary", "arbitrary"), vmem_limit_bytes=VMEM_LIMIT),
        name="na_branch",
    )(p3, p3, p3, gain2, w)


DIFF_TQ = 256


def _rope_lanes(x, cos, sin_signed, half, first_mask):
    n = x.shape[-1]
    fwd = pltpu.roll(x, n - half, 1)
    bwd = pltpu.roll(x, half, 1)
    return x * cos + jnp.where(first_mask, fwd, bwd) * sin_signed


def _diff_kernel(q_ref, k_ref, v_ref, gain_ref, cos_ref, sin_ref, lam_ref, sub_ref, o_ref,
                 qn_s, kn_s, *, lam_init):
    seq = q_ref.shape[0]
    ones_bd = _group_ones(128, HEAD_DIM)
    gq = gain_ref[0:1, :]
    gk = gain_ref[1:2, :]
    scale = HEAD_DIM ** -0.5
    lane = lax.broadcasted_iota(jnp.int32, (1, 128), 1)
    first = (lane % 32) < 16
    blk = 256
    for c in range(seq // blk):
        sl = slice(c * blk, (c + 1) * blk)
        qn = _group_rms(q_ref[sl, :].astype(F32), gq, ones_bd, HEAD_DIM)
        kn = _group_rms(k_ref[sl, :].astype(F32), gk, ones_bd, HEAD_DIM)
        if c * blk >= CTX_LEN:
            ps = slice(c * blk - CTX_LEN, (c + 1) * blk - CTX_LEN)
            qn = _rope_lanes(qn, cos_ref[ps, :], sin_ref[ps, :], 16, first)
            kn = _rope_lanes(kn, cos_ref[ps, :], sin_ref[ps, :], 16, first)
        qn_s[sl, :] = (qn * scale).astype(BF16)
        kn_s[sl, :] = kn.astype(BF16)

    lp = lam_ref[...]
    lam = (jnp.exp(jnp.sum(lp[0:1, :] * lp[1:2, :], axis=-1, keepdims=True))
           - jnp.exp(jnp.sum(lp[2:3, :] * lp[3:4, :], axis=-1, keepdims=True)) + lam_init)
    comp_mask = [lane < HEAD_DIM, lane >= HEAD_DIM]
    sub = sub_ref[...]

    def attend(q, keys, vals):
        ps = []
        for c in range(2):
            qm = jnp.where(comp_mask[c], q, jnp.zeros_like(q))
            (e,), l = _softmax_parts([_dot_nt(qm, keys)])
            ps.append((e, l))
        a = ps[0][0] * (1.0 / ps[0][1]) - ps[1][0] * (lam / ps[1][1])
        o = _dot(a.astype(BF16), vals)
        return _rms_rows(o, sub) * (1.0 - lam_init)

    o_ref[0:CTX_LEN, :] = attend(qn_s[0:CTX_LEN, :], kn_s[0:CTX_LEN, :], v_ref[0:CTX_LEN, :]).astype(o_ref.dtype)

    def q_step(i, carry):
        q0 = pl.multiple_of(CTX_LEN + i * DIFF_TQ, DIFF_TQ)
        o_ref[pl.ds(q0, DIFF_TQ), :] = attend(qn_s[pl.ds(q0, DIFF_TQ), :], kn_s[...], v_ref[...]).astype(o_ref.dtype)
        return carry

    lax.fori_loop(0, (seq - CTX_LEN) // DIFF_TQ, q_step, 0)


def _rope_angles(pos, dim):
    inv = ROPE_BASE ** (-np.arange(0, dim, 2, dtype=np.float32) / dim)
    return pos.astype(np.float32)[:, None] * inv[None, :]


def _diff_rope_tables(s):
    t = np.arange(s)
    ang_r = _rope_angles(t // GRID_W, HEAD_DIM // 2)
    ang_c = _rope_angles(t % GRID_W, HEAD_DIM // 2)
    ang64 = np.concatenate([ang_r, ang_r, ang_c, ang_c], axis=1)
    sign64 = np.concatenate([-np.ones(16), np.ones(16), -np.ones(16), np.ones(16)]).astype(np.float32)
    ang = jnp.asarray(np.tile(ang64, (1, 2)))
    return jnp.cos(ang), jnp.sin(ang) * jnp.asarray(np.tile(sign64, 2))[None, :]


def _diff_branch(p3, qk_gain, lam_p, subln, lam_init):
    b, seq, _ = p3.shape
    s = seq - CTX_LEN
    cos, sin = _diff_rope_tables(s)
    gain2 = jnp.tile(qk_gain, (1, 2))
    base = 3072 // 128
    nh = DIFF_HEADS
    return pl.pallas_call(
        functools.partial(_diff_kernel, lam_init=lam_init),
        out_shape=jax.ShapeDtypeStruct((b, seq, BRANCH_W), BF16),
        grid=(b, nh),
        in_specs=[
            pl.BlockSpec((None, seq, 128), lambda i, h: (i, 0, base + h)),
            pl.BlockSpec((None, seq, 128), lambda i, h: (i, 0, base + nh + h)),
            pl.BlockSpec((None, seq, 128), lambda i, h: (i, 0, base + 2 * nh + h)),
            pl.BlockSpec((2, 128), lambda i, h: (0, 0)),
            pl.BlockSpec((s, 128), lambda i, h: (0, 0)),
            pl.BlockSpec((s, 128), lambda i, h: (0, 0)),
            pl.BlockSpec((4, HEAD_DIM), lambda i, h: (0, 0)),
            pl.BlockSpec((1, 128), lambda i, h: (0, 0)),
        ],
        out_specs=pl.BlockSpec((None, seq, 128), lambda i, h: (i, 0, h)),
        scratch_shapes=[pltpu.VMEM((seq, 128), BF16), pltpu.VMEM((seq, 128), BF16)],
        compiler_params=pltpu.CompilerParams(
            dimension_semantics=("arbitrary", "arbitrary"), vmem_limit_bytes=VMEM_LIMIT),
        name="diff_branch",
    )(p3, p3, p3, gain2, cos, sin, lam_p, subln.reshape(1, 128))


def _log_sigmoid(x):
    return jnp.minimum(x, 0.0) - jnp.log(1.0 + jnp.exp(-jnp.abs(x)))


def _ret_kernel(q_ref, k_ref, v_ref, g_ref, cos_ref, sin_ref, dec_ref, gn_ref, o_ref,
                q_s, k_s, o_s):
    seq = q_ref.shape[0]
    c = RET_CHUNK
    n_ctx_chunks = CTX_LEN // c
    n_chunks = seq // c
    hw = RET_HEADS * HEAD_DIM
    vw = RET_HEADS * 2 * HEAD_DIM
    lane = lax.broadcasted_iota(jnp.int32, (1, hw), 1)
    first = (lane % HEAD_DIM) < (HEAD_DIM // 2)
    blk = 256
    kscale = HEAD_DIM ** -0.5
    for cb in range(seq // blk):
        sl = slice(cb * blk, (cb + 1) * blk)
        q = q_ref[sl, :].astype(F32)
        k = k_ref[sl, :].astype(F32)
        if cb * blk >= CTX_LEN:
            ps = slice(cb * blk - CTX_LEN, (cb + 1) * blk - CTX_LEN)
            q = _rope_lanes(q, cos_ref[ps, :], sin_ref[ps, :], HEAD_DIM // 2, first)
            k = _rope_lanes(k, cos_ref[ps, :], sin_ref[ps, :], HEAD_DIM // 2, first)
        q_s[sl, :] = q
        k_s[sl, :] = k * kscale

    log_g = _log_sigmoid(dec_ref[...])
    pos_r = lax.broadcasted_iota(jnp.int32, (c, c), 0).astype(F32)
    pos_c = lax.broadcasted_iota(jnp.int32, (c, c), 1).astype(F32)
    pos = lax.broadcasted_iota(jnp.int32, (c, 1), 0).astype(F32)
    head_of_q = lax.broadcasted_iota(jnp.int32, (1, hw), 1) // HEAD_DIM
    head_of_v = lax.broadcasted_iota(jnp.int32, (1, vw), 1) // (2 * HEAD_DIM)
    row_head = lax.broadcasted_iota(jnp.int32, (hw, vw), 0) // HEAD_DIM
    col_head = lax.broadcasted_iota(jnp.int32, (hw, vw), 1) // (2 * HEAD_DIM)
    state_mask = row_head == col_head

    intra, q_dec, k_dec, c_dec = [], [], [], []
    for d in range(2):
        rel = (pos_r - pos_c) if d == 0 else (pos_c - pos_r)
        lg_q = jnp.zeros((1, hw), F32)
        lg_v = jnp.zeros((1, vw), F32)
        per_head = []
        for h in range(RET_HEADS):
            lg = log_g[d:d + 1, h:h + 1]
            per_head.append(jnp.where(rel >= 0, jnp.exp(lg * jnp.maximum(rel, 0.0)), 0.0))
            lg_q = jnp.where(head_of_q == h, lg, lg_q)
            lg_v = jnp.where(head_of_v == h, lg, lg_v)
        intra.append(per_head)
        qpos = (pos + 1.0) if d == 0 else (c - pos)
        kpos = (c - 1.0 - pos) if d == 0 else pos
        q_dec.append(jnp.exp(lg_q * qpos))
        k_dec.append(jnp.exp(lg_q * kpos))
        c_dec.append(jnp.exp(lg_v * c))

    head_mask = [head_of_q == h for h in range(RET_HEADS)]

    def chunk_start(d, n):
        if d == 0:
            return n * c
        rev_ctx = (n_ctx_chunks - 1 - n) * c
        rev_lat = CTX_LEN + (n_chunks - 1 - n) * c
        return jnp.where(n < n_ctx_chunks, rev_ctx, rev_lat)

    def step(n, states):
        new_states = []
        for d in range(2):
            r0 = pl.multiple_of(chunk_start(d, n), c)
            s = states[d]
            qi = q_s[pl.ds(r0, c), :]
            ki = k_s[pl.ds(r0, c), :]
            vi = v_ref[pl.ds(r0, c), :].astype(BF16)
            kb = ki.astype(BF16)
            cross = _dot((qi * q_dec[d]).astype(BF16), s.astype(BF16))
            parts = []
            for h in range(RET_HEADS):
                qm = jnp.where(head_mask[h], qi, 0.0).astype(BF16)
                att = _dot_nt(qm, kb) * intra[d][h]
                parts.append(_dot(att.astype(BF16), vi[:, h * 128:(h + 1) * 128]))
            o = jnp.concatenate(parts, axis=1) + cross
            o_s[pl.ds(r0, c), :] = o_s[pl.ds(r0, c), :] + o
            upd = _dot_tn((ki * k_dec[d]).astype(BF16), vi)
            new_states.append(s * c_dec[d] + jnp.where(state_mask, upd, 0.0))
        return tuple(new_states)

    o_s[...] = jnp.zeros(o_s.shape, F32)
    zero = jnp.zeros((hw, vw), F32)
    lax.fori_loop(0, n_chunks, step, (zero, zero))

    gn = gn_ref[...]
    for cb in range(seq // blk):
        sl = slice(cb * blk, (cb + 1) * blk)
        outs = []
        for h in range(RET_HEADS):
            hs = slice(h * 128, (h + 1) * 128)
            y = o_s[sl, hs]
            yc = y - jnp.mean(y, axis=-1, keepdims=True)
            yn = yc * lax.rsqrt(jnp.mean(yc * yc, axis=-1, keepdims=True) + EPS) * gn
            outs.append(yn * _silu(g_ref[sl, hs].astype(F32)))
        o_ref[sl, :] = jnp.concatenate(outs, axis=1).astype(o_ref.dtype)


def _ret_rope_tables(s):
    ang = _rope_angles(np.arange(s), HEAD_DIM)
    ang64 = np.concatenate([ang, ang], axis=1)
    sign64 = np.concatenate([-np.ones(32), np.ones(32)]).astype(np.float32)
    ang = jnp.asarray(np.tile(ang64, (1, RET_HEADS)))
    return jnp.cos(ang), jnp.sin(ang) * jnp.asarray(np.tile(sign64, RET_HEADS))[None, :]


def _ret_branch(p3, decay, gn_gain):
    b, seq, _ = p3.shape
    s = seq - CTX_LEN
    cos, sin = _ret_rope_tables(s)
    return pl.pallas_call(
        _ret_kernel,
        out_shape=jax.ShapeDtypeStruct((b, seq, BRANCH_W), BF16),
        grid=(b,),
        in_specs=[
            pl.BlockSpec((None, seq, 256), lambda i: (i, 0, 6)),
            pl.BlockSpec((None, seq, 256), lambda i: (i, 0, 7)),
            pl.BlockSpec((None, seq, 512), lambda i: (i, 0, 4)),
            pl.BlockSpec((None, seq, 512), lambda i: (i, 0, 5)),
            pl.BlockSpec((s, 256), lambda i: (0, 0)),
            pl.BlockSpec((s, 256), lambda i: (0, 0)),
            pl.BlockSpec((2, RET_HEADS), lambda i: (0, 0)),
            pl.BlockSpec((1, 128), lambda i: (0, 0)),
        ],
        out_specs=pl.BlockSpec((None, seq, BRANCH_W), lambda i: (i, 0, 0)),
        scratch_shapes=[pltpu.VMEM((seq, 256), F32), pltpu.VMEM((seq, 256), F32),
                        pltpu.VMEM((seq, 512), F32)],
        compiler_params=pltpu.CompilerParams(
            dimension_semantics=("arbitrary",), vmem_limit_bytes=VMEM_LIMIT),
        name="ret_branch",
    )(p3, p3, p3, p3, cos, sin, decay, gn_gain.reshape(1, 128))


DN_PAD = 8
DN_ROWS = 4 * DN_CHUNK


def _softplus(x):
    return jnp.maximum(x, 0.0) + jnp.log(1.0 + jnp.exp(-jnp.abs(x)))


def _dn_conv_silu(pad_s, w_ref, seq, width):
    blk = 256
    outs = []
    for cb in range(seq // blk):
        base = DN_PAD + cb * blk + (DN_PAD if cb * blk >= CTX_LEN else 0)
        ext = pad_s[base - DN_PAD:base + blk + DN_PAD, 0:width]
        n = blk + 2 * DN_PAD
        acc = ext * w_ref[DN_CONV // 2:DN_CONV // 2 + 1, :]
        for j in range(DN_CONV):
            if j == DN_CONV // 2:
                continue
            acc = acc + pltpu.roll(ext, (DN_CONV // 2 - j) % n, 0) * w_ref[j:j + 1, :]
        outs.append(_silu(acc[DN_PAD:DN_PAD + blk, :]))
    return outs


def _dn_fill_pad(pad_s, refs, seq):
    off = 0
    for ref in refs:
        w = ref.shape[1]
        pad_s[DN_PAD:DN_PAD + CTX_LEN, off:off + w] = ref[0:CTX_LEN, :].astype(F32)
        pad_s[2 * DN_PAD + CTX_LEN:2 * DN_PAD + seq, off:off + w] = ref[CTX_LEN:seq, :].astype(F32)
        off += w


def _dn_kernel(q_ref, k_ref, v_ref, g_ref, ab_ref, wq_ref, wk_ref, wv_ref, par_ref, ng_ref, o_ref,
               pad_s, qk_s, v_s, gb_s, o_s, ru_s, rw_s, rq_s, ra_s, rk_s, rg_s):
    seq = q_ref.shape[0]
    hp = pl.program_id(1)
    c = DN_CHUNK
    n_ctx_chunks = CTX_LEN // c
    n_chunks = seq // c
    blk = 256

    zeros_pad = jnp.zeros((DN_PAD, 256), F32)
    pad_s[0:DN_PAD, :] = zeros_pad
    pad_s[DN_PAD + CTX_LEN:2 * DN_PAD + CTX_LEN, :] = zeros_pad
    pad_s[2 * DN_PAD + seq:3 * DN_PAD + seq, :] = zeros_pad
    ones_bd = _group_ones(128, HEAD_DIM)

    _dn_fill_pad(pad_s, [q_ref], seq)
    for cb, x in enumerate(_dn_conv_silu(pad_s, wq_ref, seq, 128)):
        ss = _split_dot(x * x, ones_bd)
        qk_s[cb * blk:(cb + 1) * blk, 0:128] = x * lax.rsqrt(ss + EPS) * (HEAD_DIM ** -0.5)
    _dn_fill_pad(pad_s, [k_ref], seq)
    for cb, x in enumerate(_dn_conv_silu(pad_s, wk_ref, seq, 128)):
        ss = _split_dot(x * x, ones_bd)
        qk_s[cb * blk:(cb + 1) * blk, 128:256] = x * lax.rsqrt(ss + EPS)
    _dn_fill_pad(pad_s, [v_ref], seq)
    for cb, x in enumerate(_dn_conv_silu(pad_s, wv_ref, seq, 256)):
        v_s[cb * blk:(cb + 1) * blk, :] = x

    lane = lax.broadcasted_iota(jnp.int32, (1, 128), 1)
    a_log = par_ref[0:1, :]
    dt_bias = par_ref[1:2, :]
    shift = (128 - 2 * hp) % 128
    for cb in range(seq // blk):
        ab = ab_ref[cb * blk:(cb + 1) * blk, :]
        gdec = -jnp.exp(a_log) * _softplus(ab + dt_bias)
        beta = _sigmoid(ab)
        gb = jnp.where(lane < 2 * DN_HEADS, gdec, beta)
        gb_s[cb * blk:(cb + 1) * blk, :] = pltpu.roll(gb, shift, 1)

    ri = lax.broadcasted_iota(jnp.int32, (DN_ROWS, DN_ROWS), 0)
    ci = lax.broadcasted_iota(jnp.int32, (DN_ROWS, DN_ROWS), 1)
    same_blk = (ri // c) == (ci // c)
    is_fwd = ri < 2 * c
    rp, cp = ri % c, ci % c
    is_bwd = jnp.logical_not(is_fwd)
    strict = jnp.logical_and(same_blk, jnp.logical_or(jnp.logical_and(is_fwd, rp > cp),
                                                      jnp.logical_and(is_bwd, rp < cp)))
    incl = jnp.logical_or(strict, ri == ci)
    eye = jnp.where(ri == ci, 1.0, 0.0)
    r64 = lax.broadcasted_iota(jnp.int32, (c, c), 0)
    c64 = lax.broadcasted_iota(jnp.int32, (c, c), 1)
    tri_lo = jnp.where(r64 >= c64, 1.0, 0.0).astype(BF16)
    tri_up = jnp.where(r64 <= c64, 1.0, 0.0).astype(BF16)
    ones64 = jnp.ones((c, c), BF16)

    def split_left(m, x):
        hi = x.astype(BF16)
        lo = (x - hi.astype(F32)).astype(BF16)
        return _dot(m, hi) + _dot(m, lo)

    def chunk_start(d, n):
        if d == 0:
            return n * c
        rev_ctx = (n_ctx_chunks - 1 - n) * c
        rev_lat = CTX_LEN + (n_chunks - 1 - n) * c
        return jnp.where(n < n_ctx_chunks, rev_ctx, rev_lat)

    def stacked_cols(mats, cols, width):
        return jnp.concatenate(
            [jnp.broadcast_to(mats[d][:, cols[d][h]:cols[d][h] + 1], (c, width))
             for d in range(2) for h in range(2)], axis=0)

    def chunk_rows(n):
        n = jnp.asarray(n, jnp.int32)
        return [pl.multiple_of(chunk_start(d, n), c) for d in range(2)]

    def prep(n, slot):
        r0 = chunk_rows(n)
        qk = [qk_s[pl.ds(r0[d], c), :] for d in range(2)]
        vv = [v_s[pl.ds(r0[d], c), :] for d in range(2)]
        gb = [gb_s[pl.ds(r0[d], c), :] for d in range(2)]
        gc = [split_left(tri_lo, gb[0]), split_left(tri_up, gb[1])]
        gt = [split_left(ones64, gb[d]) for d in range(2)]
        g_cols = [[0, 1], [4, 5]]
        b_cols = [[8, 9], [12, 13]]
        gcb = stacked_cols(gc, g_cols, DN_ROWS)
        gtb = stacked_cols(gt, g_cols, DN_ROWS)
        bb = stacked_cols(gb, b_cols, DN_ROWS)

        qrow = jnp.concatenate([qk[d][:, 0:128] for d in range(2) for _ in range(2)], axis=0)
        krow = jnp.concatenate([qk[d][:, 128:256] for d in range(2) for _ in range(2)], axis=0)
        qx = jnp.where(same_blk, jnp.concatenate([qrow, qrow], axis=1), 0.0)
        kx = jnp.where(same_blk, jnp.concatenate([krow, krow], axis=1), 0.0)
        vx = jnp.concatenate([vv[d][:, h * 128:(h + 1) * 128] for d in range(2) for h in range(2)], axis=0)

        kxb = kx.astype(BF16)
        kk = _dot_nt(kxb, kxb)
        qkm = _dot_nt(qx.astype(BF16), kxb)
        diff = gcb - jnp.transpose(gcb)
        decay = jnp.where(incl, jnp.exp(jnp.where(incl, diff, 0.0)), 0.0)
        lower = jnp.where(strict, bb * kk * decay, 0.0)

        m = -lower
        x = eye + m
        for _ in range(5):
            mb = m.astype(BF16)
            m = _dot(mb, mb)
            x = x + _dot(x.astype(BF16), m.astype(BF16))
        tb = x.astype(BF16)

        eg = jnp.exp(gcb)
        ru_s[slot] = _dot(tb, (vx * bb[:, 0:128]).astype(BF16))
        rw_s[slot] = _dot(tb, (kx * bb * eg).astype(BF16)).astype(BF16)
        rq_s[slot] = (qx * eg).astype(BF16)
        ra_s[slot] = (qkm * decay).astype(BF16)
        rk_s[slot] = (kx * jnp.exp(gtb - gcb)).astype(BF16)
        rg_s[slot] = jnp.exp(gtb[:, 0:128])

    def apply(n, slot, s):
        r0 = chunk_rows(n)
        sb = s.astype(BF16)
        v_new = (ru_s[slot] - _dot(rw_s[slot], sb)).astype(BF16)
        o = _dot(rq_s[slot], sb) + _dot(ra_s[slot], v_new)
        s_new = s * rg_s[slot] + _dot_tn(rk_s[slot], v_new)
        for d in range(2):
            od = jnp.concatenate([o[(2 * d + h) * c:(2 * d + h + 1) * c, :] for h in range(2)], axis=1)
            o_s[pl.ds(r0[d], c), :] = o_s[pl.ds(r0[d], c), :] + od
        return s_new

    o_s[...] = jnp.zeros(o_s.shape, F32)
    prep(0, 0)
    prep(1, 1)

    def group(g, s):
        n = 4 * g
        prep(n + 2, 2)
        prep(n + 3, 3)
        s = apply(n, 0, s)
        s = apply(n + 1, 1, s)
        prep(jnp.minimum(n + 4, n_chunks - 1), 0)
        prep(jnp.minimum(n + 5, n_chunks - 1), 1)
        s = apply(n + 2, 2, s)
        s = apply(n + 3, 3, s)
        return s

    assert n_chunks % 4 == 0
    lax.fori_loop(0, n_chunks // 4, group, jnp.zeros((DN_ROWS, 128), F32))

    ng = ng_ref[...]
    for cb in range(seq // blk):
        sl = slice(cb * blk, (cb + 1) * blk)
        outs = []
        for h in range(2):
            hs = slice(h * 128, (h + 1) * 128)
            outs.append(_rms_rows(o_s[sl, hs], ng) * _silu(g_ref[sl, hs].astype(F32)))
        o_ref[sl, :] = jnp.concatenate(outs, axis=1).astype(o_ref.dtype)


def _dn_branch(p3, pab3, conv_w, a_log, dt_bias, norm_gain):
    b, seq, _ = p3.shape
    par = jnp.zeros((2, 128), F32)
    par = par.at[0, 0:2 * DN_HEADS].set(a_log.reshape(-1)).at[1, 0:2 * DN_HEADS].set(dt_bias.reshape(-1))
    qb, kb_, vb, gb_ = 4608 // 128, 4864 // 128, 5120 // 256, 5632 // 256
    ring = lambda w, dt: pltpu.VMEM((4, DN_ROWS, w), dt)
    return pl.pallas_call(
        _dn_kernel,
        out_shape=jax.ShapeDtypeStruct((b, seq, BRANCH_W), BF16),
        grid=(b, 2),
        in_specs=[
            pl.BlockSpec((None, seq, 128), lambda i, hp: (i, 0, qb + hp)),
            pl.BlockSpec((None, seq, 128), lambda i, hp: (i, 0, kb_ + hp)),
            pl.BlockSpec((None, seq, 256), lambda i, hp: (i, 0, vb + hp)),
            pl.BlockSpec((None, seq, 256), lambda i, hp: (i, 0, gb_ + hp)),
            pl.BlockSpec((None, seq, 128), lambda i, hp: (i, 0, 0)),
            pl.BlockSpec((DN_CONV, 128), lambda i, hp: (0, hp)),
            pl.BlockSpec((DN_CONV, 128), lambda i, hp: (0, 2 + hp)),
            pl.BlockSpec((DN_CONV, 256), lambda i, hp: (0, 2 + hp)),
            pl.BlockSpec((2, 128), lambda i, hp: (0, 0)),
            pl.BlockSpec((1, 128), lambda i, hp: (0, 0)),
        ],
        out_specs=pl.BlockSpec((None, seq, 256), lambda i, hp: (i, 0, hp)),
        scratch_shapes=[pltpu.VMEM((seq + 3 * DN_PAD, 256), F32), pltpu.VMEM((seq, 256), F32),
                        pltpu.VMEM((seq, 256), F32), pltpu.VMEM((seq, 128), F32),
                        pltpu.VMEM((seq, 256), F32),
                        ring(128, F32), ring(DN_ROWS, BF16), ring(DN_ROWS, BF16), ring(DN_ROWS, BF16),
                        ring(DN_ROWS, BF16), ring(128, F32)],
        compiler_params=pltpu.CompilerParams(
            dimension_semantics=("arbitrary", "arbitrary"), vmem_limit_bytes=VMEM_LIMIT),
        name="dn_branch",
    )(p3, p3, p3, p3, pab3, conv_w, conv_w, conv_w, par, norm_gain.reshape(1, 128))


def _combine_kernel(hz_ref, y0_ref, y1_ref, y2_ref, y3_ref, xs_ref, mod_ref, wg_ref, bg_ref, wb_ref,
                    wo_ref, gf_ref, xo_ref, hl_ref, *, tiles_per_b):
    i = pl.program_id(0)
    tm = xs_ref.shape[0]
    hz = hz_ref[...]
    acc = jnp.zeros((tm, D_MODEL), F32)
    for n, y_ref in enumerate((y0_ref, y1_ref, y2_ref, y3_ref)):
        cs = slice(n * D_MODEL, (n + 1) * D_MODEL)
        gate = _sigmoid(_dot(hz, wg_ref[:, cs]) + bg_ref[:, cs])
        acc = acc + gate * _dot(y_ref[...], wb_ref[n])
    mix = _dot(acc.astype(BF16), wo_ref[...])
    xn = xs_ref[...] + _mod_rows(mod_ref, i, tiles_per_b, tm, 2) * mix
    xo_ref[...] = xn
    h = _rms_rows(xn, gf_ref[...])
    hl_ref[...] = h * (1.0 + _mod_rows(mod_ref, i, tiles_per_b, tm, 4)) + _mod_rows(mod_ref, i, tiles_per_b, tm, 3)


def _combine(hz, ys, xs, mod, w_mgate, b_mgate, w_branch, w_out, g_ffn, seq_len):
    t = xs.shape[0]
    tiles_per_b = seq_len // TM
    tok = lambda i: (i, 0)
    fix2 = lambda i: (0, 0)
    return pl.pallas_call(
        functools.partial(_combine_kernel, tiles_per_b=tiles_per_b),
        out_shape=(jax.ShapeDtypeStruct((t, D_MODEL), F32), jax.ShapeDtypeStruct((t, D_MODEL), F32)),
        grid=(t // TM,),
        in_specs=[
            pl.BlockSpec((TM, D_MODEL), tok),
            pl.BlockSpec((TM, BRANCH_W), tok), pl.BlockSpec((TM, BRANCH_W), tok),
            pl.BlockSpec((TM, BRANCH_W), tok), pl.BlockSpec((TM, BRANCH_W), tok),
            pl.BlockSpec((TM, D_MODEL), tok),
            pl.BlockSpec((16, 6 * D_MODEL), fix2),
            pl.BlockSpec((D_MODEL, N_BRANCH * D_MODEL), fix2),
            pl.BlockSpec((1, N_BRANCH * D_MODEL), fix2),
            pl.BlockSpec((N_BRANCH, BRANCH_W, D_MODEL), lambda i: (0, 0, 0)),
            pl.BlockSpec((D_MODEL, D_MODEL), fix2),
            pl.BlockSpec((1, D_MODEL), fix2),
        ],
        out_specs=(pl.BlockSpec((TM, D_MODEL), tok), pl.BlockSpec((TM, D_MODEL), tok)),
        compiler_params=pltpu.CompilerParams(
            dimension_semantics=("arbitrary",), vmem_limit_bytes=VMEM_LIMIT),
        name="combine",
    )(hz, *ys, xs, mod, w_mgate, b_mgate.reshape(1, -1), w_branch, w_out, g_ffn.reshape(1, D_MODEL))


def _route_kernel(h_ref, w_ref, b_ref, topi_ref, gate_ref, rank_ref, cnt_ref, base_s):
    i = pl.program_id(0)

    @pl.when(i == 0)
    def _():
        base_s[...] = jnp.zeros(base_s.shape, F32)

    tm = h_ref.shape[0]
    logits = _split_dot_w(h_ref[...], w_ref[...]) + b_ref[...]
    lane = lax.broadcasted_iota(jnp.int32, (tm, 128), 1)
    l = logits
    idxs, vals, hots = [], [], []
    for _ in range(TOP_K):
        m = l.max(axis=-1, keepdims=True)
        idx = jnp.min(jnp.where(l == m, lane, 128), axis=-1, keepdims=True)
        hot = lane == idx
        idxs.append(idx)
        vals.append(m)
        hots.append(hot)
        l = jnp.where(hot, -3e38, l)
    es = [jnp.exp(v - vals[0]) for v in vals]
    den = es[0] + es[1] + es[2] + es[3]

    cnt = jnp.zeros((tm, 128), F32)
    for hot in hots:
        cnt = cnt + jnp.where(hot, 1.0, 0.0)
    r = lax.broadcasted_iota(jnp.int32, (tm, tm), 0)
    c = lax.broadcasted_iota(jnp.int32, (tm, tm), 1)
    before = jnp.where(r > c, 1.0, 0.0).astype(BF16)
    prior = _dot(before, cnt.astype(BF16)) + base_s[...]

    topi = jnp.zeros((tm, 128), jnp.int32)
    gate = jnp.zeros((tm, 128), F32)
    rank = jnp.zeros((tm, 128), jnp.int32)
    for k in range(TOP_K):
        rk = jnp.sum(jnp.where(hots[k], prior, 0.0), axis=-1, keepdims=True)
        topi = jnp.where(lane == k, idxs[k], topi)
        gate = jnp.where(lane == k, es[k] / den, gate)
        rank = jnp.where(lane == k, rk.astype(jnp.int32), rank)
    topi_ref[...] = topi
    gate_ref[...] = gate
    rank_ref[...] = rank
    base_s[...] = base_s[...] + jnp.sum(cnt, axis=0, keepdims=True)
    cnt_ref[...] = base_s[...]


def _route(hl, router_w, router_b):
    t = hl.shape[0]
    w = jnp.pad(router_w, ((0, 0), (0, 128 - N_EXPERTS)))
    b = jnp.pad(router_b, (0, 128 - N_EXPERTS), constant_values=NEG_BIG).reshape(1, 128)
    tok = lambda i: (i, 0)
    return pl.pallas_call(
        _route_kernel,
        out_shape=(jax.ShapeDtypeStruct((t, 128), jnp.int32), jax.ShapeDtypeStruct((t, 128), F32),
                   jax.ShapeDtypeStruct((t, 128), jnp.int32), jax.ShapeDtypeStruct((1, 128), F32)),
        grid=(t // TM_ROUTE,),
        in_specs=[pl.BlockSpec((TM_ROUTE, D_MODEL), tok),
                  pl.BlockSpec((D_MODEL, 128), lambda i: (0, 0)),
                  pl.BlockSpec((1, 128), lambda i: (0, 0))],
        out_specs=(pl.BlockSpec((TM_ROUTE, 128), tok), pl.BlockSpec((TM_ROUTE, 128), tok),
                   pl.BlockSpec((TM_ROUTE, 128), tok), pl.BlockSpec((1, 128), lambda i: (0, 0))),
        scratch_shapes=[pltpu.VMEM((1, 128), F32)],
        compiler_params=pltpu.CompilerParams(dimension_semantics=("arbitrary",)),
        name="route",
    )(hl, w, b)


def _dispatch_kernel(last_ref, dest_ref, h_ref, rows_ref, zero_s, sem):
    i = pl.program_id(0)
    tm = h_ref.shape[0]

    @pl.when(i == 0)
    def _():
        zero_s[...] = jnp.zeros(zero_s.shape, F32)

        def blk_copy(blk):
            r = pl.multiple_of(blk * MOE_BM, MOE_BM)
            return pltpu.make_async_copy(zero_s, rows_ref.at[pl.ds(r, MOE_BM)], sem)

        for e in range(N_EXPERTS):
            blk_copy(last_ref[e]).start()
        for e in range(N_EXPERTS):
            blk_copy(last_ref[e]).wait()

        n_blocks = rows_ref.shape[0] // MOE_BM

        def clear_tail(blk, carry):
            blk_copy(blk).start()
            blk_copy(blk).wait()
            return carry

        lax.fori_loop(last_ref[N_EXPERTS], n_blocks, clear_tail, 0)

    def issue(t, carry):
        for k in range(TOP_K):
            d = dest_ref[t * TOP_K + k]
            pltpu.make_async_copy(h_ref.at[pl.ds(t, 1)], rows_ref.at[pl.ds(d, 1)], sem).start(priority=k % 2)
        return carry

    lax.fori_loop(0, tm, issue, 0)
    for k in range(TOP_K):
        pltpu.make_async_copy(h_ref, rows_ref.at[pl.ds(0, tm)], sem).wait()


def _dispatch(last_blk, dest_flat, hl, n_rows):
    t = hl.shape[0]
    grid_spec = pltpu.PrefetchScalarGridSpec(
        num_scalar_prefetch=1,
        grid=(t // TM_DMA,),
        in_specs=[pl.BlockSpec((TM_DMA * TOP_K,), lambda i, lb: (i,), memory_space=pltpu.SMEM),
                  pl.BlockSpec((TM_DMA, D_MODEL), lambda i, lb: (i, 0))],
        out_specs=pl.BlockSpec(memory_space=pl.ANY),
        scratch_shapes=[pltpu.VMEM((MOE_BM, D_MODEL), F32), pltpu.SemaphoreType.DMA(())],
    )
    return pl.pallas_call(
        _dispatch_kernel,
        out_shape=jax.ShapeDtypeStruct((n_rows, D_MODEL), F32),
        grid_spec=grid_spec,
        compiler_params=pltpu.CompilerParams(dimension_semantics=("arbitrary",), has_side_effects=True),
        name="moe_dispatch",
    )(last_blk, dest_flat, hl)


def _expert_kernel(be_ref, nu_ref, x_ref, wgu_ref, bgu_ref, wd_ref, bd_ref, o_ref, wgu_s, wd_s):
    j = pl.program_id(0)
    d_ff = wd_ref.shape[0]
    used = j < nu_ref[0]
    new_expert = jnp.logical_or(j == 0, be_ref[j] != be_ref[jnp.maximum(j - 1, 0)])

    @pl.when(jnp.logical_and(used, new_expert))
    def _():
        rb = 256
        for r in range(0, D_MODEL, rb):
            wgu_s[r:r + rb, :] = wgu_ref[r:r + rb, :].astype(BF16)
        for r in range(0, d_ff, rb):
            wd_s[r:r + rb, :] = wd_ref[r:r + rb, :].astype(BF16)

    @pl.when(used)
    def _():
        gu = _dot(x_ref[...].astype(BF16), wgu_s[...]) + bgu_ref[...]
        gate = jnp.minimum(gu[:, :d_ff], SWIGLU_LIMIT)
        up = jnp.clip(gu[:, d_ff:], -SWIGLU_LIMIT, SWIGLU_LIMIT)
        y = (up + 1.0) * (gate * _sigmoid(SWIGLU_ALPHA * gate))
        o_ref[...] = _dot(y.astype(BF16), wd_s[...]) + bd_ref[...]

    @pl.when(jnp.logical_not(used))
    def _():
        o_ref[...] = jnp.zeros(o_ref.shape, F32)


def _experts(block_e, n_used, rows, layer, w_gu, b_gu, w_down, b_down):
    n_rows = rows.shape[0]
    d_ff = w_down.shape[2]
    depth = w_gu.shape[0]
    row_blk = lambda j, be, nu: (jnp.minimum(j, nu[0] - 1), 0)
    grid_spec = pltpu.PrefetchScalarGridSpec(
        num_scalar_prefetch=2,
        grid=(n_rows // MOE_BM,),
        in_specs=[
            pl.BlockSpec((MOE_BM, D_MODEL), row_blk),
            pl.BlockSpec((None, None, D_MODEL, 2 * d_ff), lambda j, be, nu: (layer, be[j], 0, 0)),
            pl.BlockSpec((None, None, 1, 2 * d_ff), lambda j, be, nu: (layer, be[j], 0, 0)),
            pl.BlockSpec((None, None, d_ff, D_MODEL), lambda j, be, nu: (layer, be[j], 0, 0)),
            pl.BlockSpec((None, None, 1, D_MODEL), lambda j, be, nu: (layer, be[j], 0, 0)),
        ],
        out_specs=pl.BlockSpec((MOE_BM, D_MODEL), lambda j, be, nu: (j, 0)),
        scratch_shapes=[pltpu.VMEM((D_MODEL, 2 * d_ff), BF16), pltpu.VMEM((d_ff, D_MODEL), BF16)],
    )
    return pl.pallas_call(
        _expert_kernel,
        out_shape=jax.ShapeDtypeStruct((n_rows, D_MODEL), F32),
        grid_spec=grid_spec,
        compiler_params=pltpu.CompilerParams(
            dimension_semantics=("arbitrary",), vmem_limit_bytes=VMEM_LIMIT),
        name="moe_experts",
    )(block_e, n_used, rows, w_gu, b_gu.reshape(depth, N_EXPERTS, 1, -1), w_down,
      b_down.reshape(depth, N_EXPERTS, 1, -1))


def _finish_kernel(dest_ref, gate_ref, xs_ref, mod_ref, rows_ref, o_ref, buf, sem, *, tiles_per_b):
    i = pl.program_id(0)
    tm = xs_ref.shape[0]

    def issue(t, carry):
        for k in range(TOP_K):
            d = dest_ref[t * TOP_K + k]
            pltpu.make_async_copy(rows_ref.at[pl.ds(d, 1)], buf.at[k, pl.ds(t, 1)], sem).start(priority=k % 2)
        return carry

    lax.fori_loop(0, tm, issue, 0)
    for k in range(TOP_K):
        pltpu.make_async_copy(rows_ref.at[pl.ds(0, tm)], buf.at[k], sem).wait()
    gate = gate_ref[...]
    y = gate[:, 0:1] * buf[0]
    for k in range(1, TOP_K):
        y = y + gate[:, k:k + 1] * buf[k]
    o_ref[...] = xs_ref[...] + _mod_rows(mod_ref, i, tiles_per_b, tm, 5) * y


def _finish(dest_flat, gates, xs, mod, out_rows, seq_len):
    t = xs.shape[0]
    tiles_per_b = seq_len // TM_DMA
    return pl.pallas_call(
        functools.partial(_finish_kernel, tiles_per_b=tiles_per_b),
        out_shape=jax.ShapeDtypeStruct((t, D_MODEL), F32),
        grid=(t // TM_DMA,),
        in_specs=[pl.BlockSpec((TM_DMA * TOP_K,), lambda i: (i,), memory_space=pltpu.SMEM),
                  pl.BlockSpec((TM_DMA, 128), lambda i: (i, 0)),
                  pl.BlockSpec((TM_DMA, D_MODEL), lambda i: (i, 0)),
                  pl.BlockSpec((16, 6 * D_MODEL), lambda i: (0, 0)),
                  pl.BlockSpec(memory_space=pl.ANY)],
        out_specs=pl.BlockSpec((TM_DMA, D_MODEL), lambda i: (i, 0)),
        scratch_shapes=[pltpu.VMEM((TOP_K, TM_DMA, D_MODEL), F32), pltpu.SemaphoreType.DMA(())],
        compiler_params=pltpu.CompilerParams(
            dimension_semantics=("arbitrary",), vmem_limit_bytes=VMEM_LIMIT),
        name="moe_finish",
    )(dest_flat, gates, xs, mod, out_rows)


def _moe_plan(topi, rank, counts):
    t = topi.shape[0]
    padded = (counts + MOE_BM - 1) // MOE_BM * MOE_BM
    pad_end = jnp.cumsum(padded)
    pad_start = pad_end - padded
    onehot = topi[:, :, None] == jnp.arange(N_EXPERTS, dtype=jnp.int32)[None, None, :]
    dest = jnp.sum(jnp.where(onehot, pad_start[None, None, :], 0), axis=-1) + rank
    n_blocks = -(-(t * TOP_K + N_EXPERTS * (MOE_BM - 1)) // MOE_BM)
    starts = jnp.arange(n_blocks, dtype=jnp.int32) * MOE_BM
    block_e = jnp.sum(starts[:, None] >= pad_end[None, :], axis=-1).astype(jnp.int32)
    block_e = jnp.minimum(block_e, N_EXPERTS - 1)
    n_used = (pad_end[-1] // MOE_BM).astype(jnp.int32).reshape(1)
    last_blk = jnp.maximum(pad_end // MOE_BM - 1, 0).astype(jnp.int32)
    last_blk = jnp.concatenate([last_blk, n_used])
    return dest.reshape(-1).astype(jnp.int32), block_e, n_used, last_blk, n_blocks * MOE_BM


def kernel(x, c, ctx, c_ctx, w_ada, b_ada, norm_mix, norm_ffn, w_in, na_qk_gain, na_rpb, ret_decay, ret_gn,
           diff_qk_gain, diff_lam, diff_subln, dn_conv, dn_a_log, dn_dt_bias, dn_norm, w_branch, w_mgate,
           b_mgate, w_out, router_w, router_b, w_gu, b_gu, w_down, b_down):
    b, s, d = x.shape
    n_ctx = ctx.shape[1]
    seq = n_ctx + s
    depth = w_ada.shape[0]
    assert (d, n_ctx, s) == (D_MODEL, CTX_LEN, 2048) and b <= 8 and seq % TM == 0

    xs = jnp.concatenate([ctx, x], axis=1).reshape(b * seq, d)
    cc = jnp.zeros((16, d), F32).at[0:b].set(c).at[8].set(c_ctx)
    mods = _ada(cc, w_ada, b_ada)

    for l in range(depth):
        lam_init = 0.8 - 0.6 * math.exp(-0.3 * l)
        w_main = w_in[l, :, :D_IN_MAIN].astype(BF16)
        w_ab = jnp.pad(w_in[l, :, D_IN_MAIN:], ((0, 0), (0, 128 - (D_IN - D_IN_MAIN)))).astype(BF16)
        p, pab, hz = _in_proj(xs, mods[l], norm_mix[l], w_main, w_ab, seq)
        p3 = p.reshape(b, seq, D_IN_MAIN)
        ys = (_na_branch(p3, na_qk_gain[l], na_rpb[l]),
              _ret_branch(p3, ret_decay[l], ret_gn[l]),
              _diff_branch(p3, diff_qk_gain[l], diff_lam[l], diff_subln[l], lam_init),
              _dn_branch(p3, pab.reshape(b, seq, 128), dn_conv[l], dn_a_log[l], dn_dt_bias[l], dn_norm[l]))
        ys = tuple(y.reshape(b * seq, BRANCH_W) for y in ys)
        xs, hl = _combine(hz, ys, xs, mods[l], w_mgate[l].astype(BF16), b_mgate[l], w_branch[l].astype(BF16),
                          w_out[l].astype(BF16), norm_ffn[l], seq)
        topi, gates, rank, cnt = _route(hl, router_w[l], router_b[l])
        counts = cnt[0, :N_EXPERTS].astype(jnp.int32)
        dest, block_e, n_used, last_blk, n_rows = _moe_plan(topi[:, :TOP_K], rank[:, :TOP_K], counts)
        rows = _dispatch(last_blk, dest, hl, n_rows)
        out_rows = _experts(block_e, n_used, rows, l, w_gu, b_gu, w_down, b_down)
        xs = _finish(dest, gates, xs, mods[l], out_rows, seq)

    return xs.reshape(b, seq, d)[:, n_ctx:]
```

```python
import functools
import math

import numpy as np
import jax
import jax.numpy as jnp
from jax import lax
from jax.experimental import pallas as pl
from jax.experimental.pallas import tpu as pltpu

F32 = jnp.float32
BF16 = jnp.bfloat16

D_MODEL = 1024
GRID_W = 64
CTX_LEN = 256
HEAD_DIM = 64
N_BRANCH = 4
BRANCH_W = 512
NA_HEADS = 8
NA_WIN_R = 8
NA_WIN_C = 16
RET_HEADS = 4
RET_CHUNK = 128
DIFF_HEADS = 4
DN_HEADS = 4
DN_CONV = 5
DN_CHUNK = 64
N_EXPERTS = 32
TOP_K = 4
SWIGLU_LIMIT = 7.0
SWIGLU_ALPHA = 1.702
ROPE_BASE = 10000.0
EPS = 1e-6
D_IN = 6160
D_IN_MAIN = 6144
NEG_BIG = -1e30

VMEM_LIMIT = 56 * 1024 * 1024

TM = 768
TM_IN = 1152
TN_IN = 768
TM_ROUTE = 256
TM_DMA = 256
MOE_BM = 256


def _dot(a, b):
    return jnp.dot(a, b, preferred_element_type=F32)


def _dot_nt(a, b):
    return lax.dot_general(a, b, (((1,), (1,)), ((), ())), preferred_element_type=F32)


def _dot_tn(a, b):
    return lax.dot_general(a, b, (((0,), (0,)), ((), ())), preferred_element_type=F32)


def _split_dot(x, w_bf16):
    hi = x.astype(BF16)
    lo = (x - hi.astype(F32)).astype(BF16)
    return _dot(hi, w_bf16) + _dot(lo, w_bf16)


def _group_ones(n, group):
    r = lax.broadcasted_iota(jnp.int32, (n, n), 0) // group
    c = lax.broadcasted_iota(jnp.int32, (n, n), 1) // group
    return jnp.where(r == c, 1.0, 0.0).astype(BF16)


def _sigmoid(x):
    return 1.0 / (1.0 + jnp.exp(-x))


def _silu(x):
    return x * _sigmoid(x)


def _ada_kernel(c_ref, w_ref, b_ref, o_ref):
    c = c_ref[...]
    a = _silu(c)
    o_ref[...] = _split_dot_w(a, w_ref[...]) + b_ref[...]


def _split_dot_w(a, w):
    ah = a.astype(BF16)
    al = (a - ah.astype(F32)).astype(BF16)
    wh = w.astype(BF16)
    wl = (w - wh.astype(F32)).astype(BF16)
    return _dot(ah, wh) + _dot(al, wh) + _dot(ah, wl)


def _ada(cc, w_ada, b_ada):
    depth = w_ada.shape[0]
    n = w_ada.shape[2]
    tn = 768
    return pl.pallas_call(
        _ada_kernel,
        out_shape=jax.ShapeDtypeStruct((depth, 16, n), F32),
        grid=(depth, n // tn),
        in_specs=[
            pl.BlockSpec((16, D_MODEL), lambda l, j: (0, 0)),
            pl.BlockSpec((None, D_MODEL, tn), lambda l, j: (l, 0, j)),
            pl.BlockSpec((None, 1, tn), lambda l, j: (l, 0, j)),
        ],
        out_specs=pl.BlockSpec((None, 16, tn), lambda l, j: (l, 0, j)),
        compiler_params=pltpu.CompilerParams(dimension_semantics=("arbitrary", "arbitrary")),
        name="ada_mod",
    )(cc, w_ada, b_ada.reshape(depth, 1, n))


def _mod_rows(mod_ref, tile, tiles_per_b, rows, which):
    b = tile // tiles_per_b
    lo = which * D_MODEL
    ml = mod_ref[pl.ds(b, 1), lo:lo + D_MODEL]
    mc = mod_ref[8:9, lo:lo + D_MODEL]
    rid = lax.broadcasted_iota(jnp.int32, (rows, 1), 0)
    is_ctx = jnp.logical_and(tile % tiles_per_b == 0, rid < CTX_LEN)
    return jnp.where(is_ctx, mc, ml)


def _rms_rows(x, g):
    ms = jnp.mean(x * x, axis=-1, keepdims=True)
    return x * lax.rsqrt(ms + EPS) * g


def _in_proj_kernel(x_ref, mod_ref, g_ref, w_ref, wab_ref, p_ref, pab_ref, hz_ref, hz_s, *, tiles_per_b):
    i = pl.program_id(0)
    j = pl.program_id(1)

    @pl.when(j == 0)
    def _():
        tm = x_ref.shape[0]
        y = _rms_rows(x_ref[...], g_ref[...])
        shift = _mod_rows(mod_ref, i, tiles_per_b, tm, 0)
        scale = _mod_rows(mod_ref, i, tiles_per_b, tm, 1)
        hz = (y * (1.0 + scale) + shift).astype(BF16)
        hz_s[...] = hz
        hz_ref[...] = hz
        pab_ref[...] = _dot(hz, wab_ref[...])

    p_ref[...] = _dot(hz_s[...], w_ref[...]).astype(p_ref.dtype)


def _in_proj(xs, mod, g, w_main, w_ab, seq_len):
    t = xs.shape[0]
    n = w_main.shape[1]
    tm = TM_IN
    tiles_per_b = seq_len // tm
    return pl.pallas_call(
        functools.partial(_in_proj_kernel, tiles_per_b=tiles_per_b),
        out_shape=(jax.ShapeDtypeStruct((t, n), BF16),
                   jax.ShapeDtypeStruct((t, 128), F32),
                   jax.ShapeDtypeStruct((t, D_MODEL), BF16)),
        grid=(t // tm, n // TN_IN),
        in_specs=[
            pl.BlockSpec((tm, D_MODEL), lambda i, j: (i, 0)),
            pl.BlockSpec((16, 6 * D_MODEL), lambda i, j: (0, 0)),
            pl.BlockSpec((1, D_MODEL), lambda i, j: (0, 0)),
            pl.BlockSpec((D_MODEL, TN_IN), lambda i, j: (0, j)),
            pl.BlockSpec((D_MODEL, 128), lambda i, j: (0, 0)),
        ],
        out_specs=(pl.BlockSpec((tm, TN_IN), lambda i, j: (i, j)),
                   pl.BlockSpec((tm, 128), lambda i, j: (i, 0)),
                   pl.BlockSpec((tm, D_MODEL), lambda i, j: (i, 0))),
        scratch_shapes=[pltpu.VMEM((tm, D_MODEL), BF16)],
        compiler_params=pltpu.CompilerParams(
            dimension_semantics=("arbitrary", "arbitrary"), vmem_limit_bytes=VMEM_LIMIT),
        name="in_proj",
    )(xs, mod, g.reshape(1, D_MODEL), w_main, w_ab)


def _group_rms(x, g, ones_bd, group):
    ss = _split_dot(x * x, ones_bd)
    return x * lax.rsqrt(ss * (1.0 / group) + EPS) * g


def _softmax_parts(parts):
    m = parts[0].max(axis=-1, keepdims=True)
    for s in parts[1:]:
        m = jnp.maximum(m, s.max(axis=-1, keepdims=True))
    es = [jnp.exp(s - m) for s in parts]
    l = es[0].sum(axis=-1, keepdims=True)
    for e in es[1:]:
        l = l + e.sum(axis=-1, keepdims=True)
    return es, l


NA_GROUP = 4
NA_WROWS_MID = NA_WIN_R + NA_GROUP - 1
NA_KEYS_EDGE = CTX_LEN + NA_WIN_R * GRID_W
NA_KEYS_MID = CTX_LEN + NA_WROWS_MID * GRID_W


def _na_bias_tables(w_ref, top_s, mid_s, bot_s):
    c_i = lax.broadcasted_iota(jnp.int32, (GRID_W, 128), 0)
    l_i = lax.broadcasted_iota(jnp.int32, (GRID_W, 128), 1)
    c0 = jnp.clip(c_i - NA_WIN_C // 2, 0, GRID_W - NA_WIN_C)
    in_lo = jnp.logical_and(l_i >= c0, l_i < c0 + NA_WIN_C)
    in_hi = jnp.logical_and(l_i - GRID_W >= c0, l_i - GRID_W < c0 + NA_WIN_C)
    neg = jnp.full((GRID_W, 128), NEG_BIG, F32)
    plans = (
        (top_s, NA_WIN_R, lambda u, j: j - u + NA_WIN_R - 1),
        (mid_s, NA_WROWS_MID, lambda u, j: j - u + NA_WIN_R // 2 - 1 if 0 <= j - u < NA_WIN_R else None),
        (bot_s, NA_WIN_R, lambda u, j: j - u + NA_WIN_R // 2 - 1),
    )
    for h in range(2):
        lo, hi = [], []
        for dr in range(2 * NA_WIN_R - 1):
            row = jnp.broadcast_to(w_ref[h, dr:dr + 1, :], (GRID_W, 128))
            lo.append(jnp.where(in_lo, pltpu.roll(row, 128 - (GRID_W - 1), 1, stride=1, stride_axis=0), neg))
            hi.append(jnp.where(in_hi, pltpu.roll(row, 1, 1, stride=1, stride_axis=0), neg))
        for tab, wrows, dr_of in plans:
            tab[h, :, 0:CTX_LEN] = jnp.zeros((NA_GROUP * GRID_W, CTX_LEN), F32)
            for u in range(NA_GROUP):
                rs = slice(u * GRID_W, (u + 1) * GRID_W)
                for j in range(0, wrows, 2):
                    d_lo = dr_of(u, j)
                    d_hi = dr_of(u, j + 1) if j + 1 < wrows else None
                    t_lo = neg if d_lo is None else lo[d_lo]
                    t_hi = neg if d_hi is None else hi[d_hi]
                    tile = jnp.where(l_i < GRID_W, t_lo, t_hi)
                    col = CTX_LEN + j * GRID_W
                    if j + 1 < wrows:
                        tab[h, rs, col:col + 128] = tile
                    else:
                        tab[h, rs, col:col + GRID_W] = tile[:, 0:GRID_W]


def _na_kernel(q_ref, k_ref, v_ref, gain_ref, w_ref, o_ref, qn_s, kn_s, top_s, mid_s, bot_s, kcat_s, vcat_s):
    seq = q_ref.shape[0]
    rows = (seq - CTX_LEN) // GRID_W

    @pl.when(pl.program_id(1) == 0)
    def _():
        _na_bias_tables(w_ref, top_s, mid_s, bot_s)

    ones_bd = _group_ones(128, HEAD_DIM)
    gq = gain_ref[0:1, :]
    gk = gain_ref[1:2, :]
    scale = HEAD_DIM ** -0.5
    blk = 256
    for c in range(seq // blk):
        sl = slice(c * blk, (c + 1) * blk)
        qn_s[sl, :] = (_group_rms(q_ref[sl, :].astype(F32), gq, ones_bd, HEAD_DIM) * scale).astype(BF16)
        kn_s[sl, :] = _group_rms(k_ref[sl, :].astype(F32), gk, ones_bd, HEAD_DIM).astype(BF16)

    lane = lax.broadcasted_iota(jnp.int32, (1, 128), 1)
    head_mask = [lane < HEAD_DIM, lane >= HEAD_DIM]
    kc = kn_s[0:CTX_LEN, :]
    vc = v_ref[0:CTX_LEN, :]

    qc = qn_s[0:CTX_LEN, :]
    outs = []
    for h in range(2):
        qm = jnp.where(head_mask[h], qc, jnp.zeros_like(qc))
        (e,), l = _softmax_parts([_dot_nt(qm, kc)])
        outs.append(_dot(e.astype(BF16), vc) / l)
    o_ref[0:CTX_LEN, :] = jnp.where(head_mask[0], outs[0], outs[1]).astype(o_ref.dtype)

    assert NA_GROUP == NA_WIN_R // 2 and rows % NA_GROUP == 0
    kcat_s[0:CTX_LEN, :] = kc
    vcat_s[0:CTX_LEN, :] = vc
    gq_rows = NA_GROUP * GRID_W
    for g in range(rows // NA_GROUP):
        r = g * NA_GROUP
        if g == 0:
            tab, w0, nk = top_s, 0, NA_KEYS_EDGE
        elif g == rows // NA_GROUP - 1:
            tab, w0, nk = bot_s, rows - NA_WIN_R, NA_KEYS_EDGE
        else:
            tab, w0, nk = mid_s, r - NA_WIN_R // 2, NA_KEYS_MID
        k0 = CTX_LEN + w0 * GRID_W
        kcat_s[CTX_LEN:nk, :] = kn_s[k0:k0 + nk - CTX_LEN, :]
        vcat_s[CTX_LEN:nk, :] = v_ref[k0:k0 + nk - CTX_LEN, :]
        q0 = CTX_LEN + r * GRID_W
        qg = qn_s[q0:q0 + gq_rows, :]
        res = []
        for h in range(2):
            qm = jnp.where(head_mask[h], qg, jnp.zeros_like(qg))
            (e,), l = _softmax_parts([_dot_nt(qm, kcat_s[0:nk, :]) + tab[h, :, 0:nk]])
            res.append(_dot(e.astype(BF16), vcat_s[0:nk, :]) / l)
        o_ref[q0:q0 + gq_rows, :] = jnp.where(head_mask[0], res[0], res[1]).astype(o_ref.dtype)


def _na_branch(p3, qk_gain, rpb):
    b, seq, _ = p3.shape
    gain2 = jnp.tile(qk_gain, (1, 2))
    lo = GRID_W - NA_WIN_C
    w = jnp.pad(rpb, ((0, 0), (0, 0), (lo, 128 - lo - rpb.shape[2])))
    nb = BRANCH_W // 128
    return pl.pallas_call(
        _na_kernel,
        out_shape=jax.ShapeDtypeStruct((b, seq, BRANCH_W), BF16),
        grid=(nb, b),
        in_specs=[
            pl.BlockSpec((None, seq, 128), lambda hp, i: (i, 0, hp)),
            pl.BlockSpec((None, seq, 128), lambda hp, i: (i, 0, nb + hp)),
            pl.BlockSpec((None, seq, 128), lambda hp, i: (i, 0, 2 * nb + hp)),
            pl.BlockSpec((2, 128), lambda hp, i: (0, 0)),
            pl.BlockSpec((2, 2 * NA_WIN_R - 1, 128), lambda hp, i: (hp, 0, 0)),
        ],
        out_specs=pl.BlockSpec((None, seq, 128), lambda hp, i: (i, 0, hp)),
        scratch_shapes=[pltpu.VMEM((seq, 128), BF16), pltpu.VMEM((seq, 128), BF16),
                        pltpu.VMEM((2, NA_GROUP * GRID_W, NA_KEYS_EDGE), F32),
                        pltpu.VMEM((2, NA_GROUP * GRID_W, NA_KEYS_MID), F32),
                        pltpu.VMEM((2, NA_GROUP * GRID_W, NA_KEYS_EDGE), F32),
                        pltpu.VMEM((NA_KEYS_MID, 128), BF16), pltpu.VMEM((NA_KEYS_MID, 128), BF16)],
        compiler_params=pltpu.CompilerParams(
            dimension_semantics=("arbitrary", "arbitrary"), vmem_limit_bytes=VMEM_LIMIT),
        name="na_branch",
    )(p3, p3, p3, gain2, w)


DIFF_TQ = 256


def _rope_lanes(x, cos, sin_signed, half, first_mask):
    n = x.shape[-1]
    fwd = pltpu.roll(x, n - half, 1)
    bwd = pltpu.roll(x, half, 1)
    return x * cos + jnp.where(first_mask, fwd, bwd) * sin_signed


def _diff_kernel(q_ref, k_ref, v_ref, gain_ref, cos_ref, sin_ref, lam_ref, sub_ref, o_ref,
                 qn_s, kn_s, *, lam_init):
    seq = q_ref.shape[0]
    ones_bd = _group_ones(128, HEAD_DIM)
    gq = gain_ref[0:1, :]
    gk = gain_ref[1:2, :]
    scale = HEAD_DIM ** -0.5
    lane = lax.broadcasted_iota(jnp.int32, (1, 128), 1)
    first = (lane % 32) < 16
    blk = 256
    for c in range(seq // blk):
        sl = slice(c * blk, (c + 1) * blk)
        qn = _group_rms(q_ref[sl, :].astype(F32), gq, ones_bd, HEAD_DIM)
        kn = _group_rms(k_ref[sl, :].astype(F32), gk, ones_bd, HEAD_DIM)
        if c * blk >= CTX_LEN:
            ps = slice(c * blk - CTX_LEN, (c + 1) * blk - CTX_LEN)
            qn = _rope_lanes(qn, cos_ref[ps, :], sin_ref[ps, :], 16, first)
            kn = _rope_lanes(kn, cos_ref[ps, :], sin_ref[ps, :], 16, first)
        qn_s[sl, :] = (qn * scale).astype(BF16)
        kn_s[sl, :] = kn.astype(BF16)

    lp = lam_ref[...]
    lam = (jnp.exp(jnp.sum(lp[0:1, :] * lp[1:2, :], axis=-1, keepdims=True))
           - jnp.exp(jnp.sum(lp[2:3, :] * lp[3:4, :], axis=-1, keepdims=True)) + lam_init)
    comp_mask = [lane < HEAD_DIM, lane >= HEAD_DIM]
    sub = sub_ref[...]

    def attend(q, keys, vals):
        ps = []
        for c in range(2):
            qm = jnp.where(comp_mask[c], q, jnp.zeros_like(q))
            (e,), l = _softmax_parts([_dot_nt(qm, keys)])
            ps.append((e, l))
        a = ps[0][0] * (1.0 / ps[0][1]) - ps[1][0] * (lam / ps[1][1])
        o = _dot(a.astype(BF16), vals)
        return _rms_rows(o, sub) * (1.0 - lam_init)

    o_ref[0:CTX_LEN, :] = attend(qn_s[0:CTX_LEN, :], kn_s[0:CTX_LEN, :], v_ref[0:CTX_LEN, :]).astype(o_ref.dtype)

    def q_step(i, carry):
        q0 = pl.multiple_of(CTX_LEN + i * DIFF_TQ, DIFF_TQ)
        o_ref[pl.ds(q0, DIFF_TQ), :] = attend(qn_s[pl.ds(q0, DIFF_TQ), :], kn_s[...], v_ref[...]).astype(o_ref.dtype)
        return carry

    lax.fori_loop(0, (seq - CTX_LEN) // DIFF_TQ, q_step, 0)


def _rope_angles(pos, dim):
    inv = ROPE_BASE ** (-np.arange(0, dim, 2, dtype=np.float32) / dim)
    return pos.astype(np.float32)[:, None] * inv[None, :]


def _diff_rope_tables(s):
    t = np.arange(s)
    ang_r = _rope_angles(t // GRID_W, HEAD_DIM // 2)
    ang_c = _rope_angles(t % GRID_W, HEAD_DIM // 2)
    ang64 = np.concatenate([ang_r, ang_r, ang_c, ang_c], axis=1)
    sign64 = np.concatenate([-np.ones(16), np.ones(16), -np.ones(16), np.ones(16)]).astype(np.float32)
    ang = jnp.asarray(np.tile(ang64, (1, 2)))
    return jnp.cos(ang), jnp.sin(ang) * jnp.asarray(np.tile(sign64, 2))[None, :]


def _diff_branch(p3, qk_gain, lam_p, subln, lam_init):
    b, seq, _ = p3.shape
    s = seq - CTX_LEN
    cos, sin = _diff_rope_tables(s)
    gain2 = jnp.tile(qk_gain, (1, 2))
    base = 3072 // 128
    nh = DIFF_HEADS
    return pl.pallas_call(
        functools.partial(_diff_kernel, lam_init=lam_init),
        out_shape=jax.ShapeDtypeStruct((b, seq, BRANCH_W), BF16),
        grid=(b, nh),
        in_specs=[
            pl.BlockSpec((None, seq, 128), lambda i, h: (i, 0, base + h)),
            pl.BlockSpec((None, seq, 128), lambda i, h: (i, 0, base + nh + h)),
            pl.BlockSpec((None, seq, 128), lambda i, h: (i, 0, base + 2 * nh + h)),
            pl.BlockSpec((2, 128), lambda i, h: (0, 0)),
            pl.BlockSpec((s, 128), lambda i, h: (0, 0)),
            pl.BlockSpec((s, 128), lambda i, h: (0, 0)),
            pl.BlockSpec((4, HEAD_DIM), lambda i, h: (0, 0)),
            pl.BlockSpec((1, 128), lambda i, h: (0, 0)),
        ],
        out_specs=pl.BlockSpec((None, seq, 128), lambda i, h: (i, 0, h)),
        scratch_shapes=[pltpu.VMEM((seq, 128), BF16), pltpu.VMEM((seq, 128), BF16)],
        compiler_params=pltpu.CompilerParams(
            dimension_semantics=("arbitrary", "arbitrary"), vmem_limit_bytes=VMEM_LIMIT),
        name="diff_branch",
    )(p3, p3, p3, gain2, cos, sin, lam_p, subln.reshape(1, 128))


def _log_sigmoid(x):
    return jnp.minimum(x, 0.0) - jnp.log(1.0 + jnp.exp(-jnp.abs(x)))


def _ret_kernel(q_ref, k_ref, v_ref, g_ref, cos_ref, sin_ref, dec_ref, gn_ref, o_ref,
                q_s, k_s, o_s):
    seq = q_ref.shape[0]
    c = RET_CHUNK
    n_ctx_chunks = CTX_LEN // c
    n_chunks = seq // c
    hw = RET_HEADS * HEAD_DIM
    vw = RET_HEADS * 2 * HEAD_DIM
    lane = lax.broadcasted_iota(jnp.int32, (1, hw), 1)
    first = (lane % HEAD_DIM) < (HEAD_DIM // 2)
    blk = 256
    kscale = HEAD_DIM ** -0.5
    for cb in range(seq // blk):
        sl = slice(cb * blk, (cb + 1) * blk)
        q = q_ref[sl, :].astype(F32)
        k = k_ref[sl, :].astype(F32)
        if cb * blk >= CTX_LEN:
            ps = slice(cb * blk - CTX_LEN, (cb + 1) * blk - CTX_LEN)
            q = _rope_lanes(q, cos_ref[ps, :], sin_ref[ps, :], HEAD_DIM // 2, first)
            k = _rope_lanes(k, cos_ref[ps, :], sin_ref[ps, :], HEAD_DIM // 2, first)
        q_s[sl, :] = q
        k_s[sl, :] = k * kscale

    log_g = _log_sigmoid(dec_ref[...])
    pos_r = lax.broadcasted_iota(jnp.int32, (c, c), 0).astype(F32)
    pos_c = lax.broadcasted_iota(jnp.int32, (c, c), 1).astype(F32)
    pos = lax.broadcasted_iota(jnp.int32, (c, 1), 0).astype(F32)
    head_of_q = lax.broadcasted_iota(jnp.int32, (1, hw), 1) // HEAD_DIM
    head_of_v = lax.broadcasted_iota(jnp.int32, (1, vw), 1) // (2 * HEAD_DIM)
    row_head = lax.broadcasted_iota(jnp.int32, (hw, vw), 0) // HEAD_DIM
    col_head = lax.broadcasted_iota(jnp.int32, (hw, vw), 1) // (2 * HEAD_DIM)
    state_mask = row_head == col_head

    intra, q_dec, k_dec, c_dec = [], [], [], []
    for d in range(2):
        rel = (pos_r - pos_c) if d == 0 else (pos_c - pos_r)
        lg_q = jnp.zeros((1, hw), F32)
        lg_v = jnp.zeros((1, vw), F32)
        per_head = []
        for h in range(RET_HEADS):
            lg = log_g[d:d + 1, h:h + 1]
            per_head.append(jnp.where(rel >= 0, jnp.exp(lg * jnp.maximum(rel, 0.0)), 0.0))
            lg_q = jnp.where(head_of_q == h, lg, lg_q)
            lg_v = jnp.where(head_of_v == h, lg, lg_v)
        intra.append(per_head)
        qpos = (pos + 1.0) if d == 0 else (c - pos)
        kpos = (c - 1.0 - pos) if d == 0 else pos
        q_dec.append(jnp.exp(lg_q * qpos))
        k_dec.append(jnp.exp(lg_q * kpos))
        c_dec.append(jnp.exp(lg_v * c))

    head_mask = [head_of_q == h for h in range(RET_HEADS)]

    def chunk_start(d, n):
        if d == 0:
            return n * c
        rev_ctx = (n_ctx_chunks - 1 - n) * c
        rev_lat = CTX_LEN + (n_chunks - 1 - n) * c
        return jnp.where(n < n_ctx_chunks, rev_ctx, rev_lat)

    def step(n, states):
        new_states = []
        for d in range(2):
            r0 = pl.multiple_of(chunk_start(d, n), c)
            s = states[d]
            qi = q_s[pl.ds(r0, c), :]
            ki = k_s[pl.ds(r0, c), :]
            vi = v_ref[pl.ds(r0, c), :].astype(BF16)
            kb = ki.astype(BF16)
            cross = _dot((qi * q_dec[d]).astype(BF16), s.astype(BF16))
            parts = []
            for h in range(RET_HEADS):
                qm = jnp.where(head_mask[h], qi, 0.0).astype(BF16)
                att = _dot_nt(qm, kb) * intra[d][h]
                parts.append(_dot(att.astype(BF16), vi[:, h * 128:(h + 1) * 128]))
            o = jnp.concatenate(parts, axis=1) + cross
            o_s[pl.ds(r0, c), :] = o_s[pl.ds(r0, c), :] + o
            upd = _dot_tn((ki * k_dec[d]).astype(BF16), vi)
            new_states.append(s * c_dec[d] + jnp.where(state_mask, upd, 0.0))
        return tuple(new_states)

    o_s[...] = jnp.zeros(o_s.shape, F32)
    zero = jnp.zeros((hw, vw), F32)
    lax.fori_loop(0, n_chunks, step, (zero, zero))

    gn = gn_ref[...]
    for cb in range(seq // blk):
        sl = slice(cb * blk, (cb + 1) * blk)
        outs = []
        for h in range(RET_HEADS):
            hs = slice(h * 128, (h + 1) * 128)
            y = o_s[sl, hs]
            yc = y - jnp.mean(y, axis=-1, keepdims=True)
            yn = yc * lax.rsqrt(jnp.mean(yc * yc, axis=-1, keepdims=True) + EPS) * gn
            outs.append(yn * _silu(g_ref[sl, hs].astype(F32)))
        o_ref[sl, :] = jnp.concatenate(outs, axis=1).astype(o_ref.dtype)


def _ret_rope_tables(s):
    ang = _rope_angles(np.arange(s), HEAD_DIM)
    ang64 = np.concatenate([ang, ang], axis=1)
    sign64 = np.concatenate([-np.ones(32), np.ones(32)]).astype(np.float32)
    ang = jnp.asarray(np.tile(ang64, (1, RET_HEADS)))
    return jnp.cos(ang), jnp.sin(ang) * jnp.asarray(np.tile(sign64, RET_HEADS))[None, :]


def _ret_branch(p3, decay, gn_gain):
    b, seq, _ = p3.shape
    s = seq - CTX_LEN
    cos, sin = _ret_rope_tables(s)
    return pl.pallas_call(
        _ret_kernel,
        out_shape=jax.ShapeDtypeStruct((b, seq, BRANCH_W), BF16),
        grid=(b,),
        in_specs=[
            pl.BlockSpec((None, seq, 256), lambda i: (i, 0, 6)),
            pl.BlockSpec((None, seq, 256), lambda i: (i, 0, 7)),
            pl.BlockSpec((None, seq, 512), lambda i: (i, 0, 4)),
            pl.BlockSpec((None, seq, 512), lambda i: (i, 0, 5)),
            pl.BlockSpec((s, 256), lambda i: (0, 0)),
            pl.BlockSpec((s, 256), lambda i: (0, 0)),
            pl.BlockSpec((2, RET_HEADS), lambda i: (0, 0)),
            pl.BlockSpec((1, 128), lambda i: (0, 0)),
        ],
        out_specs=pl.BlockSpec((None, seq, BRANCH_W), lambda i: (i, 0, 0)),
        scratch_shapes=[pltpu.VMEM((seq, 256), F32), pltpu.VMEM((seq, 256), F32),
                        pltpu.VMEM((seq, 512), F32)],
        compiler_params=pltpu.CompilerParams(
            dimension_semantics=("arbitrary",), vmem_limit_bytes=VMEM_LIMIT),
        name="ret_branch",
    )(p3, p3, p3, p3, cos, sin, decay, gn_gain.reshape(1, 128))


DN_PAD = 8
DN_ROWS = 4 * DN_CHUNK


def _softplus(x):
    return jnp.maximum(x, 0.0) + jnp.log(1.0 + jnp.exp(-jnp.abs(x)))


def _dn_conv_silu(pad_s, w_ref, seq, width):
    blk = 256
    outs = []
    for cb in range(seq // blk):
        base = DN_PAD + cb * blk + (DN_PAD if cb * blk >= CTX_LEN else 0)
        ext = pad_s[base - DN_PAD:base + blk + DN_PAD, 0:width]
        n = blk + 2 * DN_PAD
        acc = ext * w_ref[DN_CONV // 2:DN_CONV // 2 + 1, :]
        for j in range(DN_CONV):
            if j == DN_CONV // 2:
                continue
            acc = acc + pltpu.roll(ext, (DN_CONV // 2 - j) % n, 0) * w_ref[j:j + 1, :]
        outs.append(_silu(acc[DN_PAD:DN_PAD + blk, :]))
    return outs


def _dn_fill_pad(pad_s, refs, seq):
    off = 0
    for ref in refs:
        w = ref.shape[1]
        pad_s[DN_PAD:DN_PAD + CTX_LEN, off:off + w] = ref[0:CTX_LEN, :].astype(F32)
        pad_s[2 * DN_PAD + CTX_LEN:2 * DN_PAD + seq, off:off + w] = ref[CTX_LEN:seq, :].astype(F32)
        off += w


def _dn_kernel(q_ref, k_ref, v_ref, g_ref, ab_ref, wq_ref, wk_ref, wv_ref, par_ref, ng_ref, o_ref,
               pad_s, qk_s, v_s, gb_s, o_s, ru_s, rw_s, rq_s, ra_s, rk_s, rg_s):
    seq = q_ref.shape[0]
    hp = pl.program_id(1)
    c = DN_CHUNK
    n_ctx_chunks = CTX_LEN // c
    n_chunks = seq // c
    blk = 256

    zeros_pad = jnp.zeros((DN_PAD, 256), F32)
    pad_s[0:DN_PAD, :] = zeros_pad
    pad_s[DN_PAD + CTX_LEN:2 * DN_PAD + CTX_LEN, :] = zeros_pad
    pad_s[2 * DN_PAD + seq:3 * DN_PAD + seq, :] = zeros_pad
    ones_bd = _group_ones(128, HEAD_DIM)

    _dn_fill_pad(pad_s, [q_ref], seq)
    for cb, x in enumerate(_dn_conv_silu(pad_s, wq_ref, seq, 128)):
        ss = _split_dot(x * x, ones_bd)
        qk_s[cb * blk:(cb + 1) * blk, 0:128] = x * lax.rsqrt(ss + EPS) * (HEAD_DIM ** -0.5)
    _dn_fill_pad(pad_s, [k_ref], seq)
    for cb, x in enumerate(_dn_conv_silu(pad_s, wk_ref, seq, 128)):
        ss = _split_dot(x * x, ones_bd)
        qk_s[cb * blk:(cb + 1) * blk, 128:256] = x * lax.rsqrt(ss + EPS)
    _dn_fill_pad(pad_s, [v_ref], seq)
    for cb, x in enumerate(_dn_conv_silu(pad_s, wv_ref, seq, 256)):
        v_s[cb * blk:(cb + 1) * blk, :] = x

    lane = lax.broadcasted_iota(jnp.int32, (1, 128), 1)
    a_log = par_ref[0:1, :]
    dt_bias = par_ref[1:2, :]
    shift = (128 - 2 * hp) % 128
    for cb in range(seq // blk):
        ab = ab_ref[cb * blk:(cb + 1) * blk, :]
        gdec = -jnp.exp(a_log) * _softplus(ab + dt_bias)
        beta = _sigmoid(ab)
        gb = jnp.where(lane < 2 * DN_HEADS, gdec, beta)
        gb_s[cb * blk:(cb + 1) * blk, :] = pltpu.roll(gb, shift, 1)

    ri = lax.broadcasted_iota(jnp.int32, (DN_ROWS, DN_ROWS), 0)
    ci = lax.broadcasted_iota(jnp.int32, (DN_ROWS, DN_ROWS), 1)
    same_blk = (ri // c) == (ci // c)
    is_fwd = ri < 2 * c
    rp, cp = ri % c, ci % c
    is_bwd = jnp.logical_not(is_fwd)
    strict = jnp.logical_and(same_blk, jnp.logical_or(jnp.logical_and(is_fwd, rp > cp),
                                                      jnp.logical_and(is_bwd, rp < cp)))
    incl = jnp.logical_or(strict, ri == ci)
    eye = jnp.where(ri == ci, 1.0, 0.0)

    r64 = lax.broadcasted_iota(jnp.int32, (c, c), 0)
    c64 = lax.broadcasted_iota(jnp.int32, (c, c), 1)
    tri_lo = jnp.where(r64 >= c64, 1.0, 0.0).astype(BF16)
    tri_up = jnp.where(r64 <= c64, 1.0, 0.0).astype(BF16)
    ones64 = jnp.ones((c, c), BF16)

    def split_left(m, x):
        hi = x.astype(BF16)
        lo = (x - hi.astype(F32)).astype(BF16)
        return _dot(m, hi) + _dot(m, lo)

    def chunk_start(d, n):
        if d == 0:
            return n * c
        rev_ctx = (n_ctx_chunks - 1 - n) * c
        rev_lat = CTX_LEN + (n_chunks - 1 - n) * c
        return jnp.where(n < n_ctx_chunks, rev_ctx, rev_lat)

    def stacked_cols(mats, cols, width):
        return jnp.concatenate(
            [jnp.broadcast_to(mats[d][:, cols[d][h]:cols[d][h] + 1], (c, width))
             for d in range(2) for h in range(2)], axis=0)

    def chunk_rows(n):
        n = jnp.asarray(n, jnp.int32)
        return [pl.multiple_of(chunk_start(d, n), c) for d in range(2)]

    def prep(n, slot):
        r0 = chunk_rows(n)
        qk = [qk_s[pl.ds(r0[d], c), :] for d in range(2)]
        vv = [v_s[pl.ds(r0[d], c), :] for d in range(2)]
        gb = [gb_s[pl.ds(r0[d], c), :] for d in range(2)]
        gc = [split_left(tri_lo, gb[0]), split_left(tri_up, gb[1])]
        gt = [split_left(ones64, gb[d]) for d in range(2)]
        g_cols = [[0, 1], [4, 5]]
        b_cols = [[8, 9], [12, 13]]
        gcb = stacked_cols(gc, g_cols, DN_ROWS)
        gtb = stacked_cols(gt, g_cols, DN_ROWS)
        bb = stacked_cols(gb, b_cols, DN_ROWS)

        qrow = jnp.concatenate([qk[d][:, 0:128] for d in range(2) for _ in range(2)], axis=0)
        krow = jnp.concatenate([qk[d][:, 128:256] for d in range(2) for _ in range(2)], axis=0)
        qx = jnp.where(same_blk, jnp.concatenate([qrow, qrow], axis=1), 0.0)
        kx = jnp.where(same_blk, jnp.concatenate([krow, krow], axis=1), 0.0)
        vx = jnp.concatenate([vv[d][:, h * 128:(h + 1) * 128] for d in range(2) for h in range(2)], axis=0)

        kxb = kx.astype(BF16)
        kk = _dot_nt(kxb, kxb)
        qkm = _dot_nt(qx.astype(BF16), kxb)
        diff = gcb - jnp.transpose(gcb)
        decay = jnp.where(incl, jnp.exp(jnp.where(incl, diff, 0.0)), 0.0)
        lower = jnp.where(strict, bb * kk * decay, 0.0)

        m = -lower
        x = eye + m
        for _ in range(5):
            mb = m.astype(BF16)
            m = _dot(mb, mb)
            x = x + _dot(x.astype(BF16), m.astype(BF16))
        tb = x.astype(BF16)

        eg = jnp.exp(gcb)
        ru_s[slot] = _dot(tb, (vx * bb[:, 0:128]).astype(BF16))
        rw_s[slot] = _dot(tb, (kx * bb * eg).astype(BF16)).astype(BF16)
        rq_s[slot] = (qx * eg).astype(BF16)
        ra_s[slot] = (qkm * decay).astype(BF16)
        rk_s[slot] = (kx * jnp.exp(gtb - gcb)).astype(BF16)
        rg_s[slot] = jnp.exp(gtb[:, 0:128])

    def apply(n, slot, s):
        r0 = chunk_rows(n)
        sb = s.astype(BF16)
        v_new = (ru_s[slot] - _dot(rw_s[slot], sb)).astype(BF16)
        o = _dot(rq_s[slot], sb) + _dot(ra_s[slot], v_new)
        s_new = s * rg_s[slot] + _dot_tn(rk_s[slot], v_new)
        for d in range(2):
            od = jnp.concatenate([o[(2 * d + h) * c:(2 * d + h + 1) * c, :] for h in range(2)], axis=1)
            o_s[pl.ds(r0[d], c), :] = o_s[pl.ds(r0[d], c), :] + od
        return s_new

    o_s[...] = jnp.zeros(o_s.shape, F32)
    prep(0, 0)
    prep(1, 1)

    def group(g, s):
        n = 4 * g
        prep(n + 2, 2)
        prep(n + 3, 3)
        s = apply(n, 0, s)
        s = apply(n + 1, 1, s)
        prep(jnp.minimum(n + 4, n_chunks - 1), 0)
        prep(jnp.minimum(n + 5, n_chunks - 1), 1)
        s = apply(n + 2, 2, s)
        s = apply(n + 3, 3, s)
        return s

    assert n_chunks % 4 == 0
    lax.fori_loop(0, n_chunks // 4, group, jnp.zeros((DN_ROWS, 128), F32))

    ng = ng_ref[...]
    for cb in range(seq // blk):
        sl = slice(cb * blk, (cb + 1) * blk)
        outs = []
        for h in range(2):
            hs = slice(h * 128, (h + 1) * 128)
            outs.append(_rms_rows(o_s[sl, hs], ng) * _silu(g_ref[sl, hs].astype(F32)))
        o_ref[sl, :] = jnp.concatenate(outs, axis=1).astype(o_ref.dtype)


def _dn_branch(p3, pab3, conv_w, a_log, dt_bias, norm_gain):
    b, seq, _ = p3.shape
    par = jnp.zeros((2, 128), F32)
    par = par.at[0, 0:2 * DN_HEADS].set(a_log.reshape(-1)).at[1, 0:2 * DN_HEADS].set(dt_bias.reshape(-1))
    qb, kb_, vb, gb_ = 4608 // 128, 4864 // 128, 5120 // 256, 5632 // 256
    ring = lambda r, w, dt: pltpu.VMEM((4, r, w), dt)
    return pl.pallas_call(
        _dn_kernel,
        out_shape=jax.ShapeDtypeStruct((b, seq, BRANCH_W), BF16),
        grid=(b, 2),
        in_specs=[
            pl.BlockSpec((None, seq, 128), lambda i, hp: (i, 0, qb + hp)),
            pl.BlockSpec((None, seq, 128), lambda i, hp: (i, 0, kb_ + hp)),
            pl.BlockSpec((None, seq, 256), lambda i, hp: (i, 0, vb + hp)),
            pl.BlockSpec((None, seq, 256), lambda i, hp: (i, 0, gb_ + hp)),
            pl.BlockSpec((None, seq, 128), lambda i, hp: (i, 0, 0)),
            pl.BlockSpec((DN_CONV, 128), lambda i, hp: (0, hp)),
            pl.BlockSpec((DN_CONV, 128), lambda i, hp: (0, 2 + hp)),
            pl.BlockSpec((DN_CONV, 256), lambda i, hp: (0, 2 + hp)),
            pl.BlockSpec((2, 128), lambda i, hp: (0, 0)),
            pl.BlockSpec((1, 128), lambda i, hp: (0, 0)),
        ],
        out_specs=pl.BlockSpec((None, seq, 256), lambda i, hp: (i, 0, hp)),
        scratch_shapes=[pltpu.VMEM((seq + 3 * DN_PAD, 256), F32), pltpu.VMEM((seq, 256), F32),
                        pltpu.VMEM((seq, 256), F32), pltpu.VMEM((seq, 128), F32),
                        pltpu.VMEM((seq, 256), F32),
                        ring(DN_ROWS, 128, F32), ring(DN_ROWS, DN_ROWS, BF16), ring(DN_ROWS, DN_ROWS, BF16),
                        ring(DN_ROWS, DN_ROWS, BF16), ring(DN_ROWS, DN_ROWS, BF16), ring(DN_ROWS, 128, F32)],
        compiler_params=pltpu.CompilerParams(
            dimension_semantics=("arbitrary", "arbitrary"), vmem_limit_bytes=VMEM_LIMIT),
        name="dn_branch",
    )(p3, p3, p3, p3, pab3, conv_w, conv_w, conv_w, par, norm_gain.reshape(1, 128))


def _combine_kernel(hz_ref, y0_ref, y1_ref, y2_ref, y3_ref, xs_ref, mod_ref, wg_ref, bg_ref, wb_ref,
                    wo_ref, gf_ref, xo_ref, hl_ref, *, tiles_per_b):
    i = pl.program_id(0)
    tm = xs_ref.shape[0]
    hz = hz_ref[...]
    acc = jnp.zeros((tm, D_MODEL), F32)
    for n, y_ref in enumerate((y0_ref, y1_ref, y2_ref, y3_ref)):
        cs = slice(n * D_MODEL, (n + 1) * D_MODEL)
        gate = _sigmoid(_dot(hz, wg_ref[:, cs]) + bg_ref[:, cs])
        acc = acc + gate * _dot(y_ref[...], wb_ref[n])
    mix = _dot(acc.astype(BF16), wo_ref[...])
    xn = xs_ref[...] + _mod_rows(mod_ref, i, tiles_per_b, tm, 2) * mix
    xo_ref[...] = xn
    h = _rms_rows(xn, gf_ref[...])
    hl_ref[...] = h * (1.0 + _mod_rows(mod_ref, i, tiles_per_b, tm, 4)) + _mod_rows(mod_ref, i, tiles_per_b, tm, 3)


def _combine(hz, ys, xs, mod, w_mgate, b_mgate, w_branch, w_out, g_ffn, seq_len):
    t = xs.shape[0]
    tiles_per_b = seq_len // TM
    tok = lambda i: (i, 0)
    fix2 = lambda i: (0, 0)
    return pl.pallas_call(
        functools.partial(_combine_kernel, tiles_per_b=tiles_per_b),
        out_shape=(jax.ShapeDtypeStruct((t, D_MODEL), F32), jax.ShapeDtypeStruct((t, D_MODEL), F32)),
        grid=(t // TM,),
        in_specs=[
            pl.BlockSpec((TM, D_MODEL), tok),
            pl.BlockSpec((TM, BRANCH_W), tok), pl.BlockSpec((TM, BRANCH_W), tok),
            pl.BlockSpec((TM, BRANCH_W), tok), pl.BlockSpec((TM, BRANCH_W), tok),
            pl.BlockSpec((TM, D_MODEL), tok),
            pl.BlockSpec((16, 6 * D_MODEL), fix2),
            pl.BlockSpec((D_MODEL, N_BRANCH * D_MODEL), fix2),
            pl.BlockSpec((1, N_BRANCH * D_MODEL), fix2),
            pl.BlockSpec((N_BRANCH, BRANCH_W, D_MODEL), lambda i: (0, 0, 0)),
            pl.BlockSpec((D_MODEL, D_MODEL), fix2),
            pl.BlockSpec((1, D_MODEL), fix2),
        ],
        out_specs=(pl.BlockSpec((TM, D_MODEL), tok), pl.BlockSpec((TM, D_MODEL), tok)),
        compiler_params=pltpu.CompilerParams(
            dimension_semantics=("arbitrary",), vmem_limit_bytes=VMEM_LIMIT),
        name="combine",
    )(hz, *ys, xs, mod, w_mgate, b_mgate.reshape(1, -1), w_branch, w_out, g_ffn.reshape(1, D_MODEL))


def _route_kernel(h_ref, w_ref, b_ref, topi_ref, gate_ref, rank_ref, cnt_ref, base_s):
    i = pl.program_id(0)

    @pl.when(i == 0)
    def _():
        base_s[...] = jnp.zeros(base_s.shape, F32)

    tm = h_ref.shape[0]
    logits = _split_dot_w(h_ref[...], w_ref[...]) + b_ref[...]
    lane = lax.broadcasted_iota(jnp.int32, (tm, 128), 1)
    l = logits
    idxs, vals, hots = [], [], []
    for _ in range(TOP_K):
        m = l.max(axis=-1, keepdims=True)
        idx = jnp.min(jnp.where(l == m, lane, 128), axis=-1, keepdims=True)
        hot = lane == idx
        idxs.append(idx)
        vals.append(m)
        hots.append(hot)
        l = jnp.where(hot, -3e38, l)
    es = [jnp.exp(v - vals[0]) for v in vals]
    den = es[0] + es[1] + es[2] + es[3]

    cnt = jnp.zeros((tm, 128), F32)
    for hot in hots:
        cnt = cnt + jnp.where(hot, 1.0, 0.0)
    r = lax.broadcasted_iota(jnp.int32, (tm, tm), 0)
    c = lax.broadcasted_iota(jnp.int32, (tm, tm), 1)
    before = jnp.where(r > c, 1.0, 0.0).astype(BF16)
    prior = _dot(before, cnt.astype(BF16)) + base_s[...]

    topi = jnp.zeros((tm, 128), jnp.int32)
    gate = jnp.zeros((tm, 128), F32)
    rank = jnp.zeros((tm, 128), jnp.int32)
    for k in range(TOP_K):
        rk = jnp.sum(jnp.where(hots[k], prior, 0.0), axis=-1, keepdims=True)
        topi = jnp.where(lane == k, idxs[k], topi)
        gate = jnp.where(lane == k, es[k] / den, gate)
        rank = jnp.where(lane == k, rk.astype(jnp.int32), rank)
    topi_ref[...] = topi
    gate_ref[...] = gate
    rank_ref[...] = rank
    base_s[...] = base_s[...] + jnp.sum(cnt, axis=0, keepdims=True)
    cnt_ref[...] = base_s[...]


def _route(hl, router_w, router_b):
    t = hl.shape[0]
    w = jnp.pad(router_w, ((0, 0), (0, 128 - N_EXPERTS)))
    b = jnp.pad(router_b, (0, 128 - N_EXPERTS), constant_values=NEG_BIG).reshape(1, 128)
    tok = lambda i: (i, 0)
    return pl.pallas_call(
        _route_kernel,
        out_shape=(jax.ShapeDtypeStruct((t, 128), jnp.int32), jax.ShapeDtypeStruct((t, 128), F32),
                   jax.ShapeDtypeStruct((t, 128), jnp.int32), jax.ShapeDtypeStruct((1, 128), F32)),
        grid=(t // TM_ROUTE,),
        in_specs=[pl.BlockSpec((TM_ROUTE, D_MODEL), tok),
                  pl.BlockSpec((D_MODEL, 128), lambda i: (0, 0)),
                  pl.BlockSpec((1, 128), lambda i: (0, 0))],
        out_specs=(pl.BlockSpec((TM_ROUTE, 128), tok), pl.BlockSpec((TM_ROUTE, 128), tok),
                   pl.BlockSpec((TM_ROUTE, 128), tok), pl.BlockSpec((1, 128), lambda i: (0, 0))),
        scratch_shapes=[pltpu.VMEM((1, 128), F32)],
        compiler_params=pltpu.CompilerParams(dimension_semantics=("arbitrary",)),
        name="route",
    )(hl, w, b)


def _dispatch_kernel(last_ref, dest_ref, h_ref, rows_ref, tile_s, zero_s, sem):
    i = pl.program_id(0)
    tm = h_ref.shape[0]

    @pl.when(i == 0)
    def _():
        zero_s[...] = jnp.zeros(zero_s.shape, F32)

        def blk_copy(blk):
            r = pl.multiple_of(blk * MOE_BM, MOE_BM)
            return pltpu.make_async_copy(zero_s, rows_ref.at[pl.ds(r, MOE_BM)], sem)

        for e in range(N_EXPERTS):
            blk_copy(last_ref[e]).start()
        for e in range(N_EXPERTS):
            blk_copy(last_ref[e]).wait()

        n_blocks = rows_ref.shape[0] // MOE_BM

        def clear_tail(blk, carry):
            blk_copy(blk).start()
            blk_copy(blk).wait()
            return carry

        lax.fori_loop(last_ref[N_EXPERTS], n_blocks, clear_tail, 0)

    tile_s[...] = h_ref[...].reshape(tile_s.shape)

    def issue(t, carry):
        for k in range(TOP_K):
            d = dest_ref[t * TOP_K + k]
            pltpu.make_async_copy(tile_s.at[t], rows_ref.at[d], sem).start(priority=k % 2)
        return carry

    lax.fori_loop(0, tm, issue, 0)
    for k in range(TOP_K):
        pltpu.make_async_copy(tile_s, rows_ref.at[pl.ds(0, tm)], sem).wait()


ROW_TILE = (8, D_MODEL // 8)


def _dispatch(last_blk, dest_flat, hl, n_rows):
    t = hl.shape[0]
    grid_spec = pltpu.PrefetchScalarGridSpec(
        num_scalar_prefetch=1,
        grid=(t // TM_DMA,),
        in_specs=[pl.BlockSpec((TM_DMA * TOP_K,), lambda i, lb: (i,), memory_space=pltpu.SMEM),
                  pl.BlockSpec((TM_DMA, D_MODEL), lambda i, lb: (i, 0))],
        out_specs=pl.BlockSpec(memory_space=pl.ANY),
        scratch_shapes=[pltpu.VMEM((TM_DMA,) + ROW_TILE, F32), pltpu.VMEM((MOE_BM,) + ROW_TILE, F32),
                        pltpu.SemaphoreType.DMA(())],
    )
    return pl.pallas_call(
        _dispatch_kernel,
        out_shape=jax.ShapeDtypeStruct((n_rows,) + ROW_TILE, F32),
        grid_spec=grid_spec,
        compiler_params=pltpu.CompilerParams(dimension_semantics=("arbitrary",), has_side_effects=True),
        name="moe_dispatch",
    )(last_blk, dest_flat, hl)


def _expert_kernel(be_ref, nu_ref, x_ref, wgu_ref, bgu_ref, wd_ref, bd_ref, o_ref, wgu_s, wd_s):
    j = pl.program_id(0)
    d_ff = wd_ref.shape[0]
    used = j < nu_ref[0]
    new_expert = jnp.logical_or(j == 0, be_ref[j] != be_ref[jnp.maximum(j - 1, 0)])

    @pl.when(jnp.logical_and(used, new_expert))
    def _():
        rb = 256
        for r in range(0, D_MODEL, rb):
            wgu_s[r:r + rb, :] = wgu_ref[r:r + rb, :].astype(BF16)
        for r in range(0, d_ff, rb):
            wd_s[r:r + rb, :] = wd_ref[r:r + rb, :].astype(BF16)

    @pl.when(used)
    def _():
        x = x_ref[...].reshape(x_ref.shape[0], D_MODEL)
        gu = _dot(x.astype(BF16), wgu_s[...]) + bgu_ref[...]
        gate = jnp.minimum(gu[:, :d_ff], SWIGLU_LIMIT)
        up = jnp.clip(gu[:, d_ff:], -SWIGLU_LIMIT, SWIGLU_LIMIT)
        y = (up + 1.0) * (gate * _sigmoid(SWIGLU_ALPHA * gate))
        o_ref[...] = (_dot(y.astype(BF16), wd_s[...]) + bd_ref[...]).reshape(o_ref.shape)

    @pl.when(jnp.logical_not(used))
    def _():
        o_ref[...] = jnp.zeros(o_ref.shape, F32)


def _experts(block_e, n_used, rows, layer, w_gu, b_gu, w_down, b_down):
    n_rows = rows.shape[0]
    d_ff = w_down.shape[2]
    depth = w_gu.shape[0]
    row_blk = lambda j, be, nu: (jnp.minimum(j, nu[0] - 1), 0, 0)
    grid_spec = pltpu.PrefetchScalarGridSpec(
        num_scalar_prefetch=2,
        grid=(n_rows // MOE_BM,),
        in_specs=[
            pl.BlockSpec((MOE_BM,) + ROW_TILE, row_blk),
            pl.BlockSpec((None, None, D_MODEL, 2 * d_ff), lambda j, be, nu: (layer, be[j], 0, 0)),
            pl.BlockSpec((None, None, 1, 2 * d_ff), lambda j, be, nu: (layer, be[j], 0, 0)),
            pl.BlockSpec((None, None, d_ff, D_MODEL), lambda j, be, nu: (layer, be[j], 0, 0)),
            pl.BlockSpec((None, None, 1, D_MODEL), lambda j, be, nu: (layer, be[j], 0, 0)),
        ],
        out_specs=pl.BlockSpec((MOE_BM,) + ROW_TILE, lambda j, be, nu: (j, 0, 0)),
        scratch_shapes=[pltpu.VMEM((D_MODEL, 2 * d_ff), BF16), pltpu.VMEM((d_ff, D_MODEL), BF16)],
    )
    return pl.pallas_call(
        _expert_kernel,
        out_shape=jax.ShapeDtypeStruct((n_rows,) + ROW_TILE, F32),
        grid_spec=grid_spec,
        compiler_params=pltpu.CompilerParams(
            dimension_semantics=("arbitrary",), vmem_limit_bytes=VMEM_LIMIT),
        name="moe_experts",
    )(block_e, n_used, rows, w_gu, b_gu.reshape(depth, N_EXPERTS, 1, -1), w_down,
      b_down.reshape(depth, N_EXPERTS, 1, -1))


def _finish_kernel(dest_ref, gate_ref, xs_ref, mod_ref, rows_ref, o_ref, buf, sem, *, tiles_per_b):
    i = pl.program_id(0)
    tm = xs_ref.shape[0]

    def issue(t, carry):
        for k in range(TOP_K):
            d = dest_ref[t * TOP_K + k]
            pltpu.make_async_copy(rows_ref.at[d], buf.at[k, t], sem).start(priority=k % 2)
        return carry

    lax.fori_loop(0, tm, issue, 0)
    for k in range(TOP_K):
        pltpu.make_async_copy(rows_ref.at[pl.ds(0, tm)], buf.at[k], sem).wait()
    gate = gate_ref[...]
    y = gate[:, 0:1] * buf[0].reshape(tm, D_MODEL)
    for k in range(1, TOP_K):
        y = y + gate[:, k:k + 1] * buf[k].reshape(tm, D_MODEL)
    o_ref[...] = xs_ref[...] + _mod_rows(mod_ref, i, tiles_per_b, tm, 5) * y


def _finish(dest_flat, gates, xs, mod, out_rows, seq_len):
    t = xs.shape[0]
    tiles_per_b = seq_len // TM_DMA
    return pl.pallas_call(
        functools.partial(_finish_kernel, tiles_per_b=tiles_per_b),
        out_shape=jax.ShapeDtypeStruct((t, D_MODEL), F32),
        grid=(t // TM_DMA,),
        in_specs=[pl.BlockSpec((TM_DMA * TOP_K,), lambda i: (i,), memory_space=pltpu.SMEM),
                  pl.BlockSpec((TM_DMA, 128), lambda i: (i, 0)),
                  pl.BlockSpec((TM_DMA, D_MODEL), lambda i: (i, 0)),
                  pl.BlockSpec((16, 6 * D_MODEL), lambda i: (0, 0)),
                  pl.BlockSpec(memory_space=pl.ANY)],
        out_specs=pl.BlockSpec((TM_DMA, D_MODEL), lambda i: (i, 0)),
        scratch_shapes=[pltpu.VMEM((TOP_K, TM_DMA) + ROW_TILE, F32), pltpu.SemaphoreType.DMA(())],
        compiler_params=pltpu.CompilerParams(
            dimension_semantics=("arbitrary",), vmem_limit_bytes=VMEM_LIMIT),
        name="moe_finish",
    )(dest_flat, gates, xs, mod, out_rows)


def _moe_plan(topi, rank, counts):
    t = topi.shape[0]
    padded = (counts + MOE_BM - 1) // MOE_BM * MOE_BM
    pad_end = jnp.cumsum(padded)
    pad_start = pad_end - padded
    onehot = topi[:, :, None] == jnp.arange(N_EXPERTS, dtype=jnp.int32)[None, None, :]
    dest = jnp.sum(jnp.where(onehot, pad_start[None, None, :], 0), axis=-1) + rank
    n_blocks = -(-(t * TOP_K + N_EXPERTS * (MOE_BM - 1)) // MOE_BM)
    starts = jnp.arange(n_blocks, dtype=jnp.int32) * MOE_BM
    block_e = jnp.sum(starts[:, None] >= pad_end[None, :], axis=-1).astype(jnp.int32)
    block_e = jnp.minimum(block_e, N_EXPERTS - 1)
    n_used = (pad_end[-1] // MOE_BM).astype(jnp.int32).reshape(1)
    last_blk = jnp.maximum(pad_end // MOE_BM - 1, 0).astype(jnp.int32)
    last_blk = jnp.concatenate([last_blk, n_used])
    return dest.reshape(-1).astype(jnp.int32), block_e, n_used, last_blk, n_blocks * MOE_BM


def kernel(x, c, ctx, c_ctx, w_ada, b_ada, norm_mix, norm_ffn, w_in, na_qk_gain, na_rpb, ret_decay, ret_gn,
           diff_qk_gain, diff_lam, diff_subln, dn_conv, dn_a_log, dn_dt_bias, dn_norm, w_branch, w_mgate,
           b_mgate, w_out, router_w, router_b, w_gu, b_gu, w_down, b_down):
    b, s, d = x.shape
    n_ctx = ctx.shape[1]
    seq = n_ctx + s
    depth = w_ada.shape[0]
    assert (d, n_ctx, s) == (D_MODEL, CTX_LEN, 2048) and b <= 8 and seq % TM == 0

    xs = jnp.concatenate([ctx, x], axis=1).reshape(b * seq, d)
    cc = jnp.zeros((16, d), F32).at[0:b].set(c).at[8].set(c_ctx)
    mods = _ada(cc, w_ada, b_ada)

    for l in range(depth):
        lam_init = 0.8 - 0.6 * math.exp(-0.3 * l)
        w_main = w_in[l, :, :D_IN_MAIN].astype(BF16)
        w_ab = jnp.pad(w_in[l, :, D_IN_MAIN:], ((0, 0), (0, 128 - (D_IN - D_IN_MAIN)))).astype(BF16)
        p, pab, hz = _in_proj(xs, mods[l], norm_mix[l], w_main, w_ab, seq)
        p3 = p.reshape(b, seq, D_IN_MAIN)
        ys = (_na_branch(p3, na_qk_gain[l], na_rpb[l]),
              _ret_branch(p3, ret_decay[l], ret_gn[l]),
              _diff_branch(p3, diff_qk_gain[l], diff_lam[l], diff_subln[l], lam_init),
              _dn_branch(p3, pab.reshape(b, seq, 128), dn_conv[l], dn_a_log[l], dn_dt_bias[l], dn_norm[l]))
        ys = tuple(y.reshape(b * seq, BRANCH_W) for y in ys)
        xs, hl = _combine(hz, ys, xs, mods[l], w_mgate[l].astype(BF16), b_mgate[l], w_branch[l].astype(BF16),
                          w_out[l].astype(BF16), norm_ffn[l], seq)
        topi, gates, rank, cnt = _route(hl, router_w[l], router_b[l])
        counts = cnt[0, :N_EXPERTS].astype(jnp.int32)
        dest, block_e, n_used, last_blk, n_rows = _moe_plan(topi[:, :TOP_K], rank[:, :TOP_K], counts)
        rows = _dispatch(last_blk, dest, hl, n_rows)
        out_rows = _experts(block_e, n_used, rows, l, w_gu, b_gu, w_down, b_down)
        xs = _finish(dest, gates, xs, mods[l], out_rows, seq)

    return xs.reshape(b, seq, d)[:, n_ctx:]
```

```python
import functools
import math

import numpy as np
import jax
import jax.numpy as jnp
from jax import lax
from jax.experimental import pallas as pl
from jax.experimental.pallas import tpu as pltpu

F32 = jnp.float32
BF16 = jnp.bfloat16

D_MODEL = 1024
GRID_W = 64
CTX_LEN = 256
HEAD_DIM = 64
N_BRANCH = 4
BRANCH_W = 512
NA_HEADS = 8
NA_WIN_R = 8
NA_WIN_C = 16
RET_HEADS = 4
RET_CHUNK = 128
DIFF_HEADS = 4
DN_HEADS = 4
DN_CONV = 5
DN_CHUNK = 64
N_EXPERTS = 32
TOP_K = 4
SWIGLU_LIMIT = 7.0
SWIGLU_ALPHA = 1.702
ROPE_BASE = 10000.0
EPS = 1e-6
D_IN = 6160
D_IN_MAIN = 6144
NEG_BIG = -1e30

VMEM_LIMIT = 56 * 1024 * 1024

TM = 768
TM_IN = 1152
TN_IN = 768
TM_ROUTE = 256
TM_DMA = 256
MOE_BM = 256


def _dot(a, b):
    return jnp.dot(a, b, preferred_element_type=F32)


def _dot_nt(a, b):
    return lax.dot_general(a, b, (((1,), (1,)), ((), ())), preferred_element_type=F32)


def _dot_tn(a, b):
    return lax.dot_general(a, b, (((0,), (0,)), ((), ())), preferred_element_type=F32)


def _split_dot(x, w_bf16):
    hi = x.astype(BF16)
    lo = (x - hi.astype(F32)).astype(BF16)
    return _dot(hi, w_bf16) + _dot(lo, w_bf16)


def _group_ones(n, group):
    r = lax.broadcasted_iota(jnp.int32, (n, n), 0) // group
    c = lax.broadcasted_iota(jnp.int32, (n, n), 1) // group
    return jnp.where(r == c, 1.0, 0.0).astype(BF16)


def _sigmoid(x):
    return 1.0 / (1.0 + jnp.exp(-x))


def _silu(x):
    return x * _sigmoid(x)


def _ada_kernel(c_ref, w_ref, b_ref, o_ref):
    c = c_ref[...]
    a = _silu(c)
    o_ref[...] = _split_dot_w(a, w_ref[...]) + b_ref[...]


def _split_dot_w(a, w):
    ah = a.astype(BF16)
    al = (a - ah.astype(F32)).astype(BF16)
    wh = w.astype(BF16)
    wl = (w - wh.astype(F32)).astype(BF16)
    return _dot(ah, wh) + _dot(al, wh) + _dot(ah, wl)


def _ada(cc, w_ada, b_ada):
    depth = w_ada.shape[0]
    n = w_ada.shape[2]
    tn = 768
    return pl.pallas_call(
        _ada_kernel,
        out_shape=jax.ShapeDtypeStruct((depth, 16, n), F32),
        grid=(depth, n // tn),
        in_specs=[
            pl.BlockSpec((16, D_MODEL), lambda l, j: (0, 0)),
            pl.BlockSpec((None, D_MODEL, tn), lambda l, j: (l, 0, j)),
            pl.BlockSpec((None, 1, tn), lambda l, j: (l, 0, j)),
        ],
        out_specs=pl.BlockSpec((None, 16, tn), lambda l, j: (l, 0, j)),
        compiler_params=pltpu.CompilerParams(dimension_semantics=("arbitrary", "arbitrary")),
        name="ada_mod",
    )(cc, w_ada, b_ada.reshape(depth, 1, n))


def _mod_rows(mod_ref, tile, tiles_per_b, rows, which, has_ctx=True):
    b = tile // tiles_per_b
    lo = which * D_MODEL
    ml = mod_ref[pl.ds(b, 1), lo:lo + D_MODEL]
    if not has_ctx:
        return ml
    mc = mod_ref[8:9, lo:lo + D_MODEL]
    rid = lax.broadcasted_iota(jnp.int32, (rows, 1), 0)
    is_ctx = jnp.logical_and(tile % tiles_per_b == 0, rid < CTX_LEN)
    return jnp.where(is_ctx, mc, ml)


def _token_tiles(seq_len, tile, latent_only):
    per_b = seq_len // tile
    if not latent_only:
        return per_b, (lambda i: i)
    skip = CTX_LEN // tile
    assert skip * tile == CTX_LEN
    return per_b - skip, (lambda i: (i // (per_b - skip)) * per_b + skip + i % (per_b - skip))


def _rms_rows(x, g):
    ms = jnp.mean(x * x, axis=-1, keepdims=True)
    return x * lax.rsqrt(ms + EPS) * g


def _in_proj_kernel(x_ref, mod_ref, g_ref, w_ref, wab_ref, p_ref, pab_ref, hz_ref, hz_s, *, tiles_per_b):
    i = pl.program_id(0)
    j = pl.program_id(1)

    @pl.when(j == 0)
    def _():
        tm = x_ref.shape[0]
        y = _rms_rows(x_ref[...], g_ref[...])
        shift = _mod_rows(mod_ref, i, tiles_per_b, tm, 0)
        scale = _mod_rows(mod_ref, i, tiles_per_b, tm, 1)
        hz = (y * (1.0 + scale) + shift).astype(BF16)
        hz_s[...] = hz
        hz_ref[...] = hz
        pab_ref[...] = _dot(hz, wab_ref[...])

    p_ref[...] = _dot(hz_s[...], w_ref[...]).astype(p_ref.dtype)


def _in_proj(xs, mod, g, w_main, w_ab, seq_len):
    t = xs.shape[0]
    n = w_main.shape[1]
    tm = TM_IN
    tiles_per_b = seq_len // tm
    return pl.pallas_call(
        functools.partial(_in_proj_kernel, tiles_per_b=tiles_per_b),
        out_shape=(jax.ShapeDtypeStruct((t, n), BF16),
                   jax.ShapeDtypeStruct((t, 128), F32),
                   jax.ShapeDtypeStruct((t, D_MODEL), BF16)),
        grid=(t // tm, n // TN_IN),
        in_specs=[
            pl.BlockSpec((tm, D_MODEL), lambda i, j: (i, 0)),
            pl.BlockSpec((16, 6 * D_MODEL), lambda i, j: (0, 0)),
            pl.BlockSpec((1, D_MODEL), lambda i, j: (0, 0)),
            pl.BlockSpec((D_MODEL, TN_IN), lambda i, j: (0, j)),
            pl.BlockSpec((D_MODEL, 128), lambda i, j: (0, 0)),
        ],
        out_specs=(pl.BlockSpec((tm, TN_IN), lambda i, j: (i, j)),
                   pl.BlockSpec((tm, 128), lambda i, j: (i, 0)),
                   pl.BlockSpec((tm, D_MODEL), lambda i, j: (i, 0))),
        scratch_shapes=[pltpu.VMEM((tm, D_MODEL), BF16)],
        compiler_params=pltpu.CompilerParams(
            dimension_semantics=("arbitrary", "arbitrary"), vmem_limit_bytes=VMEM_LIMIT),
        name="in_proj",
    )(xs, mod, g.reshape(1, D_MODEL), w_main, w_ab)


def _group_rms(x, g, ones_bd, group):
    ss = _split_dot(x * x, ones_bd)
    return x * lax.rsqrt(ss * (1.0 / group) + EPS) * g


def _softmax_parts(parts):
    m = parts[0].max(axis=-1, keepdims=True)
    for s in parts[1:]:
        m = jnp.maximum(m, s.max(axis=-1, keepdims=True))
    es = [jnp.exp(s - m) for s in parts]
    l = es[0].sum(axis=-1, keepdims=True)
    for e in es[1:]:
        l = l + e.sum(axis=-1, keepdims=True)
    return es, l


NA_GROUP = 4
NA_WROWS_MID = NA_WIN_R + NA_GROUP - 1
NA_KEYS_EDGE = CTX_LEN + NA_WIN_R * GRID_W
NA_KEYS_MID = CTX_LEN + NA_WROWS_MID * GRID_W


def _na_bias_tables(w_ref, top_s, mid_s, bot_s):
    c_i = lax.broadcasted_iota(jnp.int32, (GRID_W, 128), 0)
    l_i = lax.broadcasted_iota(jnp.int32, (GRID_W, 128), 1)
    c0 = jnp.clip(c_i - NA_WIN_C // 2, 0, GRID_W - NA_WIN_C)
    in_lo = jnp.logical_and(l_i >= c0, l_i < c0 + NA_WIN_C)
    in_hi = jnp.logical_and(l_i - GRID_W >= c0, l_i - GRID_W < c0 + NA_WIN_C)
    neg = jnp.full((GRID_W, 128), NEG_BIG, F32)
    plans = (
        (top_s, NA_WIN_R, lambda u, j: j - u + NA_WIN_R - 1),
        (mid_s, NA_WROWS_MID, lambda u, j: j - u + NA_WIN_R // 2 - 1 if 0 <= j - u < NA_WIN_R else None),
        (bot_s, NA_WIN_R, lambda u, j: j - u + NA_WIN_R // 2 - 1),
    )
    for h in range(2):
        lo, hi = [], []
        for dr in range(2 * NA_WIN_R - 1):
            row = jnp.broadcast_to(w_ref[h, dr:dr + 1, :], (GRID_W, 128))
            lo.append(jnp.where(in_lo, pltpu.roll(row, 128 - (GRID_W - 1), 1, stride=1, stride_axis=0), neg))
            hi.append(jnp.where(in_hi, pltpu.roll(row, 1, 1, stride=1, stride_axis=0), neg))
        for tab, wrows, dr_of in plans:
            tab[h, :, 0:CTX_LEN] = jnp.zeros((NA_GROUP * GRID_W, CTX_LEN), F32)
            for u in range(NA_GROUP):
                rs = slice(u * GRID_W, (u + 1) * GRID_W)
                for j in range(0, wrows, 2):
                    d_lo = dr_of(u, j)
                    d_hi = dr_of(u, j + 1) if j + 1 < wrows else None
                    t_lo = neg if d_lo is None else lo[d_lo]
                    t_hi = neg if d_hi is None else hi[d_hi]
                    tile = jnp.where(l_i < GRID_W, t_lo, t_hi)
                    col = CTX_LEN + j * GRID_W
                    if j + 1 < wrows:
                        tab[h, rs, col:col + 128] = tile
                    else:
                        tab[h, rs, col:col + GRID_W] = tile[:, 0:GRID_W]


def _na_kernel(q_ref, k_ref, v_ref, gain_ref, w_ref, o_ref, qn_s, kn_s, top_s, mid_s, bot_s, kcat_s, vcat_s):
    seq = q_ref.shape[0]
    rows = (seq - CTX_LEN) // GRID_W

    @pl.when(pl.program_id(1) == 0)
    def _():
        _na_bias_tables(w_ref, top_s, mid_s, bot_s)

    ones_bd = _group_ones(128, HEAD_DIM)
    gq = gain_ref[0:1, :]
    gk = gain_ref[1:2, :]
    scale = HEAD_DIM ** -0.5
    blk = 256
    for c in range(seq // blk):
        sl = slice(c * blk, (c + 1) * blk)
        qn_s[sl, :] = (_group_rms(q_ref[sl, :].astype(F32), gq, ones_bd, HEAD_DIM) * scale).astype(BF16)
        kn_s[sl, :] = _group_rms(k_ref[sl, :].astype(F32), gk, ones_bd, HEAD_DIM).astype(BF16)

    lane = lax.broadcasted_iota(jnp.int32, (1, 128), 1)
    head_mask = [lane < HEAD_DIM, lane >= HEAD_DIM]
    kc = kn_s[0:CTX_LEN, :]
    vc = v_ref[0:CTX_LEN, :]

    qc = qn_s[0:CTX_LEN, :]
    outs = []
    for h in range(2):
        qm = jnp.where(head_mask[h], qc, jnp.zeros_like(qc))
        (e,), l = _softmax_parts([_dot_nt(qm, kc)])
        outs.append(_dot(e.astype(BF16), vc) / l)
    o_ref[0:CTX_LEN, :] = jnp.where(head_mask[0], outs[0], outs[1]).astype(o_ref.dtype)

    assert NA_GROUP == NA_WIN_R // 2 and rows % NA_GROUP == 0
    kcat_s[0:CTX_LEN, :] = kc
    vcat_s[0:CTX_LEN, :] = vc
    gq_rows = NA_GROUP * GRID_W
    for g in range(rows // NA_GROUP):
        r = g * NA_GROUP
        if g == 0:
            tab, w0, nk = top_s, 0, NA_KEYS_EDGE
        elif g == rows // NA_GROUP - 1:
            tab, w0, nk = bot_s, rows - NA_WIN_R, NA_KEYS_EDGE
        else:
            tab, w0, nk = mid_s, r - NA_WIN_R // 2, NA_KEYS_MID
        k0 = CTX_LEN + w0 * GRID_W
        kcat_s[CTX_LEN:nk, :] = kn_s[k0:k0 + nk - CTX_LEN, :]
        vcat_s[CTX_LEN:nk, :] = v_ref[k0:k0 + nk - CTX_LEN, :]
        q0 = CTX_LEN + r * GRID_W
        qg = qn_s[q0:q0 + gq_rows, :]
        res = []
        for h in range(2):
            qm = jnp.where(head_mask[h], qg, jnp.zeros_like(qg))
            (e,), l = _softmax_parts([_dot_nt(qm, kcat_s[0:nk, :]) + tab[h, :, 0:nk]])
            res.append(_dot(e.astype(BF16), vcat_s[0:nk, :]) / l)
        o_ref[q0:q0 + gq_rows, :] = jnp.where(head_mask[0], res[0], res[1]).astype(o_ref.dtype)


def _na_branch(p3, qk_gain, rpb):
    b, seq, _ = p3.shape
    gain2 = jnp.tile(qk_gain, (1, 2))
    lo = GRID_W - NA_WIN_C
    w = jnp.pad(rpb, ((0, 0), (0, 0), (lo, 128 - lo - rpb.shape[2])))
    nb = BRANCH_W // 128
    return pl.pallas_call(
        _na_kernel,
        out_shape=jax.ShapeDtypeStruct((b, seq, BRANCH_W), BF16),
        grid=(nb, b),
        in_specs=[
            pl.BlockSpec((None, seq, 128), lambda hp, i: (i, 0, hp)),
            pl.BlockSpec((None, seq, 128), lambda hp, i: (i, 0, nb + hp)),
            pl.BlockSpec((None, seq, 128), lambda hp, i: (i, 0, 2 * nb + hp)),
            pl.BlockSpec((2, 128), lambda hp, i: (0, 0)),
            pl.BlockSpec((2, 2 * NA_WIN_R - 1, 128), lambda hp, i: (hp, 0, 0)),
        ],
        out_specs=pl.BlockSpec((None, seq, 128), lambda hp, i: (i, 0, hp)),
        scratch_shapes=[pltpu.VMEM((seq, 128), BF16), pltpu.VMEM((seq, 128), BF16),
                        pltpu.VMEM((2, NA_GROUP * GRID_W, NA_KEYS_EDGE), F32),
                        pltpu.VMEM((2, NA_GROUP * GRID_W, NA_KEYS_MID), F32),
                        pltpu.VMEM((2, NA_GROUP * GRID_W, NA_KEYS_EDGE), F32),
                        pltpu.VMEM((NA_KEYS_MID, 128), BF16), pltpu.VMEM((NA_KEYS_MID, 128), BF16)],
        compiler_params=pltpu.CompilerParams(
            dimension_semantics=("arbitrary", "arbitrary"), vmem_limit_bytes=VMEM_LIMIT),
        name="na_branch",
    )(p3, p3, p3, gain2, w)


DIFF_TQ = 256


def _rope_lanes(x, cos, sin_signed, half, first_mask):
    n = x.shape[-1]
    fwd = pltpu.roll(x, n - half, 1)
    bwd = pltpu.roll(x, half, 1)
    return x * cos + jnp.where(first_mask, fwd, bwd) * sin_signed


def _diff_kernel(q_ref, k_ref, v_ref, gain_ref, cos_ref, sin_ref, lam_ref, sub_ref, o_ref,
                 qn_s, kn_s, *, lam_init):
    seq = q_ref.shape[0]
    ones_bd = _group_ones(128, HEAD_DIM)
    gq = gain_ref[0:1, :]
    gk = gain_ref[1:2, :]
    scale = HEAD_DIM ** -0.5
    lane = lax.broadcasted_iota(jnp.int32, (1, 128), 1)
    first = (lane % 32) < 16
    blk = 256
    for c in range(seq // blk):
        sl = slice(c * blk, (c + 1) * blk)
        qn = _group_rms(q_ref[sl, :].astype(F32), gq, ones_bd, HEAD_DIM)
        kn = _group_rms(k_ref[sl, :].astype(F32), gk, ones_bd, HEAD_DIM)
        if c * blk >= CTX_LEN:
            ps = slice(c * blk - CTX_LEN, (c + 1) * blk - CTX_LEN)
            qn = _rope_lanes(qn, cos_ref[ps, :], sin_ref[ps, :], 16, first)
            kn = _rope_lanes(kn, cos_ref[ps, :], sin_ref[ps, :], 16, first)
        qn_s[sl, :] = (qn * scale).astype(BF16)
        kn_s[sl, :] = kn.astype(BF16)

    lp = lam_ref[...]
    lam = (jnp.exp(jnp.sum(lp[0:1, :] * lp[1:2, :], axis=-1, keepdims=True))
           - jnp.exp(jnp.sum(lp[2:3, :] * lp[3:4, :], axis=-1, keepdims=True)) + lam_init)
    comp_mask = [lane < HEAD_DIM, lane >= HEAD_DIM]
    sub = sub_ref[...]

    def attend(q, keys, vals):
        ps = []
        for c in range(2):
            qm = jnp.where(comp_mask[c], q, jnp.zeros_like(q))
            (e,), l = _softmax_parts([_dot_nt(qm, keys)])
            ps.append((e, l))
        a = ps[0][0] * (1.0 / ps[0][1]) - ps[1][0] * (lam / ps[1][1])
        o = _dot(a.astype(BF16), vals)
        return _rms_rows(o, sub) * (1.0 - lam_init)

    o_ref[0:CTX_LEN, :] = attend(qn_s[0:CTX_LEN, :], kn_s[0:CTX_LEN, :], v_ref[0:CTX_LEN, :]).astype(o_ref.dtype)

    def q_step(i, carry):
        q0 = pl.multiple_of(CTX_LEN + i * DIFF_TQ, DIFF_TQ)
        o_ref[pl.ds(q0, DIFF_TQ), :] = attend(qn_s[pl.ds(q0, DIFF_TQ), :], kn_s[...], v_ref[...]).astype(o_ref.dtype)
        return carry

    lax.fori_loop(0, (seq - CTX_LEN) // DIFF_TQ, q_step, 0)


def _rope_angles(pos, dim):
    inv = ROPE_BASE ** (-np.arange(0, dim, 2, dtype=np.float32) / dim)
    return pos.astype(np.float32)[:, None] * inv[None, :]


def _diff_rope_tables(s):
    t = np.arange(s)
    ang_r = _rope_angles(t // GRID_W, HEAD_DIM // 2)
    ang_c = _rope_angles(t % GRID_W, HEAD_DIM // 2)
    ang64 = np.concatenate([ang_r, ang_r, ang_c, ang_c], axis=1)
    sign64 = np.concatenate([-np.ones(16), np.ones(16), -np.ones(16), np.ones(16)]).astype(np.float32)
    ang = jnp.asarray(np.tile(ang64, (1, 2)))
    return jnp.cos(ang), jnp.sin(ang) * jnp.asarray(np.tile(sign64, 2))[None, :]


def _diff_branch(p3, qk_gain, lam_p, subln, lam_init):
    b, seq, _ = p3.shape
    s = seq - CTX_LEN
    cos, sin = _diff_rope_tables(s)
    gain2 = jnp.tile(qk_gain, (1, 2))
    base = 3072 // 128
    nh = DIFF_HEADS
    return pl.pallas_call(
        functools.partial(_diff_kernel, lam_init=lam_init),
        out_shape=jax.ShapeDtypeStruct((b, seq, BRANCH_W), BF16),
        grid=(b, nh),
        in_specs=[
            pl.BlockSpec((None, seq, 128), lambda i, h: (i, 0, base + h)),
            pl.BlockSpec((None, seq, 128), lambda i, h: (i, 0, base + nh + h)),
            pl.BlockSpec((None, seq, 128), lambda i, h: (i, 0, base + 2 * nh + h)),
            pl.BlockSpec((2, 128), lambda i, h: (0, 0)),
            pl.BlockSpec((s, 128), lambda i, h: (0, 0)),
            pl.BlockSpec((s, 128), lambda i, h: (0, 0)),
            pl.BlockSpec((4, HEAD_DIM), lambda i, h: (0, 0)),
            pl.BlockSpec((1, 128), lambda i, h: (0, 0)),
        ],
        out_specs=pl.BlockSpec((None, seq, 128), lambda i, h: (i, 0, h)),
        scratch_shapes=[pltpu.VMEM((seq, 128), BF16), pltpu.VMEM((seq, 128), BF16)],
        compiler_params=pltpu.CompilerParams(
            dimension_semantics=("arbitrary", "arbitrary"), vmem_limit_bytes=VMEM_LIMIT),
        name="diff_branch",
    )(p3, p3, p3, gain2, cos, sin, lam_p, subln.reshape(1, 128))


def _log_sigmoid(x):
    return jnp.minimum(x, 0.0) - jnp.log(1.0 + jnp.exp(-jnp.abs(x)))


def _ret_kernel(q_ref, k_ref, v_ref, g_ref, cos_ref, sin_ref, dec_ref, gn_ref, o_ref,
                q_s, k_s, o_s):
    seq = q_ref.shape[0]
    c = RET_CHUNK
    n_ctx_chunks = CTX_LEN // c
    n_chunks = seq // c
    hw = RET_HEADS * HEAD_DIM
    vw = RET_HEADS * 2 * HEAD_DIM
    lane = lax.broadcasted_iota(jnp.int32, (1, hw), 1)
    first = (lane % HEAD_DIM) < (HEAD_DIM // 2)
    blk = 256
    kscale = HEAD_DIM ** -0.5
    for cb in range(seq // blk):
        sl = slice(cb * blk, (cb + 1) * blk)
        q = q_ref[sl, :].astype(F32)
        k = k_ref[sl, :].astype(F32)
        if cb * blk >= CTX_LEN:
            ps = slice(cb * blk - CTX_LEN, (cb + 1) * blk - CTX_LEN)
            q = _rope_lanes(q, cos_ref[ps, :], sin_ref[ps, :], HEAD_DIM // 2, first)
            k = _rope_lanes(k, cos_ref[ps, :], sin_ref[ps, :], HEAD_DIM // 2, first)
        q_s[sl, :] = q
        k_s[sl, :] = k * kscale

    log_g = _log_sigmoid(dec_ref[...])
    pos_r = lax.broadcasted_iota(jnp.int32, (c, c), 0).astype(F32)
    pos_c = lax.broadcasted_iota(jnp.int32, (c, c), 1).astype(F32)
    pos = lax.broadcasted_iota(jnp.int32, (c, 1), 0).astype(F32)
    head_of_q = lax.broadcasted_iota(jnp.int32, (1, hw), 1) // HEAD_DIM
    head_of_v = lax.broadcasted_iota(jnp.int32, (1, vw), 1) // (2 * HEAD_DIM)
    row_head = lax.broadcasted_iota(jnp.int32, (hw, vw), 0) // HEAD_DIM
    col_head = lax.broadcasted_iota(jnp.int32, (hw, vw), 1) // (2 * HEAD_DIM)
    state_mask = row_head == col_head

    intra, q_dec, k_dec, c_dec = [], [], [], []
    for d in range(2):
        rel = (pos_r - pos_c) if d == 0 else (pos_c - pos_r)
        lg_q = jnp.zeros((1, hw), F32)
        lg_v = jnp.zeros((1, vw), F32)
        per_head = []
        for h in range(RET_HEADS):
            lg = log_g[d:d + 1, h:h + 1]
            per_head.append(jnp.where(rel >= 0, jnp.exp(lg * jnp.maximum(rel, 0.0)), 0.0))
            lg_q = jnp.where(head_of_q == h, lg, lg_q)
            lg_v = jnp.where(head_of_v == h, lg, lg_v)
        intra.append(per_head)
        qpos = (pos + 1.0) if d == 0 else (c - pos)
        kpos = (c - 1.0 - pos) if d == 0 else pos
        q_dec.append(jnp.exp(lg_q * qpos))
        k_dec.append(jnp.exp(lg_q * kpos))
        c_dec.append(jnp.exp(lg_v * c))

    head_mask = [head_of_q == h for h in range(RET_HEADS)]

    def chunk_start(d, n):
        if d == 0:
            return n * c
        rev_ctx = (n_ctx_chunks - 1 - n) * c
        rev_lat = CTX_LEN + (n_chunks - 1 - n) * c
        return jnp.where(n < n_ctx_chunks, rev_ctx, rev_lat)

    def step(n, states):
        new_states = []
        for d in range(2):
            r0 = pl.multiple_of(chunk_start(d, n), c)
            s = states[d]
            qi = q_s[pl.ds(r0, c), :]
            ki = k_s[pl.ds(r0, c), :]
            vi = v_ref[pl.ds(r0, c), :].astype(BF16)
            kb = ki.astype(BF16)
            cross = _dot((qi * q_dec[d]).astype(BF16), s.astype(BF16))
            parts = []
            for h in range(RET_HEADS):
                qm = jnp.where(head_mask[h], qi, 0.0).astype(BF16)
                att = _dot_nt(qm, kb) * intra[d][h]
                parts.append(_dot(att.astype(BF16), vi[:, h * 128:(h + 1) * 128]))
            o = jnp.concatenate(parts, axis=1) + cross
            o_s[pl.ds(r0, c), :] = o_s[pl.ds(r0, c), :] + o
            upd = _dot_tn((ki * k_dec[d]).astype(BF16), vi)
            new_states.append(s * c_dec[d] + jnp.where(state_mask, upd, 0.0))
        return tuple(new_states)

    o_s[...] = jnp.zeros(o_s.shape, F32)
    zero = jnp.zeros((hw, vw), F32)
    lax.fori_loop(0, n_chunks, step, (zero, zero))

    gn = gn_ref[...]
    for cb in range(seq // blk):
        sl = slice(cb * blk, (cb + 1) * blk)
        outs = []
        for h in range(RET_HEADS):
            hs = slice(h * 128, (h + 1) * 128)
            y = o_s[sl, hs]
            yc = y - jnp.mean(y, axis=-1, keepdims=True)
            yn = yc * lax.rsqrt(jnp.mean(yc * yc, axis=-1, keepdims=True) + EPS) * gn
            outs.append(yn * _silu(g_ref[sl, hs].astype(F32)))
        o_ref[sl, :] = jnp.concatenate(outs, axis=1).astype(o_ref.dtype)


def _ret_rope_tables(s):
    ang = _rope_angles(np.arange(s), HEAD_DIM)
    ang64 = np.concatenate([ang, ang], axis=1)
    sign64 = np.concatenate([-np.ones(32), np.ones(32)]).astype(np.float32)
    ang = jnp.asarray(np.tile(ang64, (1, RET_HEADS)))
    return jnp.cos(ang), jnp.sin(ang) * jnp.asarray(np.tile(sign64, RET_HEADS))[None, :]


def _ret_branch(p3, decay, gn_gain):
    b, seq, _ = p3.shape
    s = seq - CTX_LEN
    cos, sin = _ret_rope_tables(s)
    return pl.pallas_call(
        _ret_kernel,
        out_shape=jax.ShapeDtypeStruct((b, seq, BRANCH_W), BF16),
        grid=(b,),
        in_specs=[
            pl.BlockSpec((None, seq, 256), lambda i: (i, 0, 6)),
            pl.BlockSpec((None, seq, 256), lambda i: (i, 0, 7)),
            pl.BlockSpec((None, seq, 512), lambda i: (i, 0, 4)),
            pl.BlockSpec((None, seq, 512), lambda i: (i, 0, 5)),
            pl.BlockSpec((s, 256), lambda i: (0, 0)),
            pl.BlockSpec((s, 256), lambda i: (0, 0)),
            pl.BlockSpec((2, RET_HEADS), lambda i: (0, 0)),
            pl.BlockSpec((1, 128), lambda i: (0, 0)),
        ],
        out_specs=pl.BlockSpec((None, seq, BRANCH_W), lambda i: (i, 0, 0)),
        scratch_shapes=[pltpu.VMEM((seq, 256), F32), pltpu.VMEM((seq, 256), F32),
                        pltpu.VMEM((seq, 512), F32)],
        compiler_params=pltpu.CompilerParams(
            dimension_semantics=("arbitrary",), vmem_limit_bytes=VMEM_LIMIT),
        name="ret_branch",
    )(p3, p3, p3, p3, cos, sin, decay, gn_gain.reshape(1, 128))


DN_PAD = 8
DN_ROWS = 4 * DN_CHUNK


def _softplus(x):
    return jnp.maximum(x, 0.0) + jnp.log(1.0 + jnp.exp(-jnp.abs(x)))


def _dn_conv_silu(pad_s, w_ref, seq, width):
    blk = 256
    outs = []
    for cb in range(seq // blk):
        base = DN_PAD + cb * blk + (DN_PAD if cb * blk >= CTX_LEN else 0)
        ext = pad_s[base - DN_PAD:base + blk + DN_PAD, 0:width]
        n = blk + 2 * DN_PAD
        acc = ext * w_ref[DN_CONV // 2:DN_CONV // 2 + 1, :]
        for j in range(DN_CONV):
            if j == DN_CONV // 2:
                continue
            acc = acc + pltpu.roll(ext, (DN_CONV // 2 - j) % n, 0) * w_ref[j:j + 1, :]
        outs.append(_silu(acc[DN_PAD:DN_PAD + blk, :]))
    return outs


def _dn_fill_pad(pad_s, refs, seq):
    off = 0
    for ref in refs:
        w = ref.shape[1]
        pad_s[DN_PAD:DN_PAD + CTX_LEN, off:off + w] = ref[0:CTX_LEN, :].astype(F32)
        pad_s[2 * DN_PAD + CTX_LEN:2 * DN_PAD + seq, off:off + w] = ref[CTX_LEN:seq, :].astype(F32)
        off += w


def _dn_kernel(q_ref, k_ref, v_ref, g_ref, ab_ref, wq_ref, wk_ref, wv_ref, par_ref, ng_ref, o_ref,
               pad_s, qk_s, v_s, gb_s, o_s, ru_s, rw_s, rq_s, ra_s, rk_s, rg_s):
    seq = q_ref.shape[0]
    hp = pl.program_id(1)
    c = DN_CHUNK
    n_ctx_chunks = CTX_LEN // c
    n_chunks = seq // c
    blk = 256

    zeros_pad = jnp.zeros((DN_PAD, 256), F32)
    pad_s[0:DN_PAD, :] = zeros_pad
    pad_s[DN_PAD + CTX_LEN:2 * DN_PAD + CTX_LEN, :] = zeros_pad
    pad_s[2 * DN_PAD + seq:3 * DN_PAD + seq, :] = zeros_pad
    ones_bd = _group_ones(128, HEAD_DIM)

    _dn_fill_pad(pad_s, [q_ref], seq)
    for cb, x in enumerate(_dn_conv_silu(pad_s, wq_ref, seq, 128)):
        ss = _split_dot(x * x, ones_bd)
        qk_s[cb * blk:(cb + 1) * blk, 0:128] = x * lax.rsqrt(ss + EPS) * (HEAD_DIM ** -0.5)
    _dn_fill_pad(pad_s, [k_ref], seq)
    for cb, x in enumerate(_dn_conv_silu(pad_s, wk_ref, seq, 128)):
        ss = _split_dot(x * x, ones_bd)
        qk_s[cb * blk:(cb + 1) * blk, 128:256] = x * lax.rsqrt(ss + EPS)
    _dn_fill_pad(pad_s, [v_ref], seq)
    for cb, x in enumerate(_dn_conv_silu(pad_s, wv_ref, seq, 256)):
        v_s[cb * blk:(cb + 1) * blk, :] = x

    lane = lax.broadcasted_iota(jnp.int32, (1, 128), 1)
    a_log = par_ref[0:1, :]
    dt_bias = par_ref[1:2, :]
    shift = (128 - 2 * hp) % 128
    for cb in range(seq // blk):
        ab = ab_ref[cb * blk:(cb + 1) * blk, :]
        gdec = -jnp.exp(a_log) * _softplus(ab + dt_bias)
        beta = _sigmoid(ab)
        gb = jnp.where(lane < 2 * DN_HEADS, gdec, beta)
        gb_s[cb * blk:(cb + 1) * blk, :] = pltpu.roll(gb, shift, 1)

    ri = lax.broadcasted_iota(jnp.int32, (DN_ROWS, DN_ROWS), 0)
    ci = lax.broadcasted_iota(jnp.int32, (DN_ROWS, DN_ROWS), 1)
    same_blk = (ri // c) == (ci // c)
    is_fwd = ri < 2 * c
    rp, cp = ri % c, ci % c
    is_bwd = jnp.logical_not(is_fwd)
    strict = jnp.logical_and(same_blk, jnp.logical_or(jnp.logical_and(is_fwd, rp > cp),
                                                      jnp.logical_and(is_bwd, rp < cp)))
    incl = jnp.logical_or(strict, ri == ci)
    eye = jnp.where(ri == ci, 1.0, 0.0)

    r64 = lax.broadcasted_iota(jnp.int32, (c, c), 0)
    c64 = lax.broadcasted_iota(jnp.int32, (c, c), 1)
    tri_lo = jnp.where(r64 >= c64, 1.0, 0.0).astype(BF16)
    tri_up = jnp.where(r64 <= c64, 1.0, 0.0).astype(BF16)
    ones64 = jnp.ones((c, c), BF16)

    def split_left(m, x):
        hi = x.astype(BF16)
        lo = (x - hi.astype(F32)).astype(BF16)
        return _dot(m, hi) + _dot(m, lo)

    def chunk_start(d, n):
        if d == 0:
            return n * c
        rev_ctx = (n_ctx_chunks - 1 - n) * c
        rev_lat = CTX_LEN + (n_chunks - 1 - n) * c
        return jnp.where(n < n_ctx_chunks, rev_ctx, rev_lat)

    def stacked_cols(mats, cols, width):
        return jnp.concatenate(
            [jnp.broadcast_to(mats[d][:, cols[d][h]:cols[d][h] + 1], (c, width))
             for d in range(2) for h in range(2)], axis=0)

    def chunk_rows(n):
        n = jnp.asarray(n, jnp.int32)
        return [pl.multiple_of(chunk_start(d, n), c) for d in range(2)]

    def prep(n, slot):
        r0 = chunk_rows(n)
        qk = [qk_s[pl.ds(r0[d], c), :] for d in range(2)]
        vv = [v_s[pl.ds(r0[d], c), :] for d in range(2)]
        gb = [gb_s[pl.ds(r0[d], c), :] for d in range(2)]
        gc = [split_left(tri_lo, gb[0]), split_left(tri_up, gb[1])]
        gt = [split_left(ones64, gb[d]) for d in range(2)]
        g_cols = [[0, 1], [4, 5]]
        b_cols = [[8, 9], [12, 13]]
        gcb = stacked_cols(gc, g_cols, DN_ROWS)
        gtb = stacked_cols(gt, g_cols, DN_ROWS)
        bb = stacked_cols(gb, b_cols, DN_ROWS)

        qrow = jnp.concatenate([qk[d][:, 0:128] for d in range(2) for _ in range(2)], axis=0)
        krow = jnp.concatenate([qk[d][:, 128:256] for d in range(2) for _ in range(2)], axis=0)
        qx = jnp.where(same_blk, jnp.concatenate([qrow, qrow], axis=1), 0.0)
        kx = jnp.where(same_blk, jnp.concatenate([krow, krow], axis=1), 0.0)
        vx = jnp.concatenate([vv[d][:, h * 128:(h + 1) * 128] for d in range(2) for h in range(2)], axis=0)

        kxb = kx.astype(BF16)
        kk = _dot_nt(kxb, kxb)
        qkm = _dot_nt(qx.astype(BF16), kxb)
        diff = gcb - jnp.transpose(gcb)
        decay = jnp.where(incl, jnp.exp(jnp.where(incl, diff, 0.0)), 0.0)
        lower = jnp.where(strict, bb * kk * decay, 0.0)

        m = -lower
        x = eye + m
        for _ in range(5):
            mb = m.astype(BF16)
            m = _dot(mb, mb)
            x = x + _dot(x.astype(BF16), m.astype(BF16))
        tb = x.astype(BF16)

        eg = jnp.exp(gcb)
        ru_s[slot] = _dot(tb, (vx * bb[:, 0:128]).astype(BF16))
        rw_s[slot] = _dot(tb, (kx * bb * eg).astype(BF16)).astype(BF16)
        rq_s[slot] = (qx * eg).astype(BF16)
        ra_s[slot] = (qkm * decay).astype(BF16)
        rk_s[slot] = (kx * jnp.exp(gtb - gcb)).astype(BF16)
        rg_s[slot] = jnp.exp(gtb[:, 0:128])

    def apply(n, slot, s):
        r0 = chunk_rows(n)
        sb = s.astype(BF16)
        v_new = (ru_s[slot] - _dot(rw_s[slot], sb)).astype(BF16)
        o = _dot(rq_s[slot], sb) + _dot(ra_s[slot], v_new)
        s_new = s * rg_s[slot] + _dot_tn(rk_s[slot], v_new)
        for d in range(2):
            od = jnp.concatenate([o[(2 * d + h) * c:(2 * d + h + 1) * c, :] for h in range(2)], axis=1)
            o_s[d, pl.ds(r0[d], c), :] = od
        return s_new

    prep(0, 0)
    prep(1, 1)

    def group(g, s):
        n = 4 * g
        prep(n + 2, 2)
        prep(n + 3, 3)
        s = apply(n, 0, s)
        s = apply(n + 1, 1, s)
        prep(jnp.minimum(n + 4, n_chunks - 1), 0)
        prep(jnp.minimum(n + 5, n_chunks - 1), 1)
        s = apply(n + 2, 2, s)
        s = apply(n + 3, 3, s)
        return s

    assert n_chunks % 4 == 0
    lax.fori_loop(0, n_chunks // 4, group, jnp.zeros((DN_ROWS, 128), F32))

    ng = ng_ref[...]
    for cb in range(seq // blk):
        sl = slice(cb * blk, (cb + 1) * blk)
        outs = []
        for h in range(2):
            hs = slice(h * 128, (h + 1) * 128)
            outs.append(_rms_rows(o_s[0, sl, hs] + o_s[1, sl, hs], ng) * _silu(g_ref[sl, hs].astype(F32)))
        o_ref[sl, :] = jnp.concatenate(outs, axis=1).astype(o_ref.dtype)


def _dn_branch(p3, pab3, conv_w, a_log, dt_bias, norm_gain):
    b, seq, _ = p3.shape
    par = jnp.zeros((2, 128), F32)
    par = par.at[0, 0:2 * DN_HEADS].set(a_log.reshape(-1)).at[1, 0:2 * DN_HEADS].set(dt_bias.reshape(-1))
    qb, kb_, vb, gb_ = 4608 // 128, 4864 // 128, 5120 // 256, 5632 // 256
    ring = lambda r, w, dt: pltpu.VMEM((4, r, w), dt)
    return pl.pallas_call(
        _dn_kernel,
        out_shape=jax.ShapeDtypeStruct((b, seq, BRANCH_W), BF16),
        grid=(b, 2),
        in_specs=[
            pl.BlockSpec((None, seq, 128), lambda i, hp: (i, 0, qb + hp)),
            pl.BlockSpec((None, seq, 128), lambda i, hp: (i, 0, kb_ + hp)),
            pl.BlockSpec((None, seq, 256), lambda i, hp: (i, 0, vb + hp)),
            pl.BlockSpec((None, seq, 256), lambda i, hp: (i, 0, gb_ + hp)),
            pl.BlockSpec((None, seq, 128), lambda i, hp: (i, 0, 0)),
            pl.BlockSpec((DN_CONV, 128), lambda i, hp: (0, hp)),
            pl.BlockSpec((DN_CONV, 128), lambda i, hp: (0, 2 + hp)),
            pl.BlockSpec((DN_CONV, 256), lambda i, hp: (0, 2 + hp)),
            pl.BlockSpec((2, 128), lambda i, hp: (0, 0)),
            pl.BlockSpec((1, 128), lambda i, hp: (0, 0)),
        ],
        out_specs=pl.BlockSpec((None, seq, 256), lambda i, hp: (i, 0, hp)),
        scratch_shapes=[pltpu.VMEM((seq + 3 * DN_PAD, 256), F32), pltpu.VMEM((seq, 256), F32),
                        pltpu.VMEM((seq, 256), F32), pltpu.VMEM((seq, 128), F32),
                        pltpu.VMEM((2, seq, 256), F32),
                        ring(DN_ROWS, 128, F32), ring(DN_ROWS, DN_ROWS, BF16), ring(DN_ROWS, DN_ROWS, BF16),
                        ring(DN_ROWS, DN_ROWS, BF16), ring(DN_ROWS, DN_ROWS, BF16), ring(DN_ROWS, 128, F32)],
        compiler_params=pltpu.CompilerParams(
            dimension_semantics=("arbitrary", "arbitrary"), vmem_limit_bytes=VMEM_LIMIT),
        name="dn_branch",
    )(p3, p3, p3, p3, pab3, conv_w, conv_w, conv_w, par, norm_gain.reshape(1, 128))


def _combine_kernel(hz_ref, y0_ref, y1_ref, y2_ref, y3_ref, xs_ref, mod_ref, wg_ref, bg_ref, wb_ref,
                    wo_ref, gf_ref, xo_ref, hl_ref, *, tiles_per_b):
    i = pl.program_id(0)
    tm = xs_ref.shape[0]
    hz = hz_ref[...]
    acc = jnp.zeros((tm, D_MODEL), F32)
    for n, y_ref in enumerate((y0_ref, y1_ref, y2_ref, y3_ref)):
        cs = slice(n * D_MODEL, (n + 1) * D_MODEL)
        gate = _sigmoid(_dot(hz, wg_ref[:, cs]) + bg_ref[:, cs])
        acc = acc + gate * _dot(y_ref[...], wb_ref[n])
    mix = _dot(acc.astype(BF16), wo_ref[...])
    xn = xs_ref[...] + _mod_rows(mod_ref, i, tiles_per_b, tm, 2) * mix
    xo_ref[...] = xn
    h = _rms_rows(xn, gf_ref[...])
    hl_ref[...] = h * (1.0 + _mod_rows(mod_ref, i, tiles_per_b, tm, 4)) + _mod_rows(mod_ref, i, tiles_per_b, tm, 3)


def _combine(hz, ys, xs, mod, w_mgate, b_mgate, w_branch, w_out, g_ffn, seq_len):
    t = xs.shape[0]
    tiles_per_b = seq_len // TM
    tok = lambda i: (i, 0)
    fix2 = lambda i: (0, 0)
    return pl.pallas_call(
        functools.partial(_combine_kernel, tiles_per_b=tiles_per_b),
        out_shape=(jax.ShapeDtypeStruct((t, D_MODEL), F32), jax.ShapeDtypeStruct((t, D_MODEL), F32)),
        grid=(t // TM,),
        in_specs=[
            pl.BlockSpec((TM, D_MODEL), tok),
            pl.BlockSpec((TM, BRANCH_W), tok), pl.BlockSpec((TM, BRANCH_W), tok),
            pl.BlockSpec((TM, BRANCH_W), tok), pl.BlockSpec((TM, BRANCH_W), tok),
            pl.BlockSpec((TM, D_MODEL), tok),
            pl.BlockSpec((16, 6 * D_MODEL), fix2),
            pl.BlockSpec((D_MODEL, N_BRANCH * D_MODEL), fix2),
            pl.BlockSpec((1, N_BRANCH * D_MODEL), fix2),
            pl.BlockSpec((N_BRANCH, BRANCH_W, D_MODEL), lambda i: (0, 0, 0)),
            pl.BlockSpec((D_MODEL, D_MODEL), fix2),
            pl.BlockSpec((1, D_MODEL), fix2),
        ],
        out_specs=(pl.BlockSpec((TM, D_MODEL), tok), pl.BlockSpec((TM, D_MODEL), tok)),
        compiler_params=pltpu.CompilerParams(
            dimension_semantics=("arbitrary",), vmem_limit_bytes=VMEM_LIMIT),
        name="combine",
    )(hz, *ys, xs, mod, w_mgate, b_mgate.reshape(1, -1), w_branch, w_out, g_ffn.reshape(1, D_MODEL))


def _route_kernel(h_ref, w_ref, b_ref, topi_ref, gate_ref, rank_ref, cnt_ref, base_s):
    i = pl.program_id(0)

    @pl.when(i == 0)
    def _():
        base_s[...] = jnp.zeros(base_s.shape, F32)

    tm = h_ref.shape[0]
    logits = _split_dot_w(h_ref[...], w_ref[...]) + b_ref[...]
    lane = lax.broadcasted_iota(jnp.int32, (tm, 128), 1)
    l = logits
    idxs, vals, hots = [], [], []
    for _ in range(TOP_K):
        m = l.max(axis=-1, keepdims=True)
        idx = jnp.min(jnp.where(l == m, lane, 128), axis=-1, keepdims=True)
        hot = lane == idx
        idxs.append(idx)
        vals.append(m)
        hots.append(hot)
        l = jnp.where(hot, -3e38, l)
    es = [jnp.exp(v - vals[0]) for v in vals]
    den = es[0] + es[1] + es[2] + es[3]

    cnt = jnp.zeros((tm, 128), F32)
    for hot in hots:
        cnt = cnt + jnp.where(hot, 1.0, 0.0)
    r = lax.broadcasted_iota(jnp.int32, (tm, tm), 0)
    c = lax.broadcasted_iota(jnp.int32, (tm, tm), 1)
    before = jnp.where(r > c, 1.0, 0.0).astype(BF16)
    prior = _dot(before, cnt.astype(BF16)) + base_s[...]

    topi = jnp.zeros((tm, 128), jnp.int32)
    gate = jnp.zeros((tm, 128), F32)
    rank = jnp.zeros((tm, 128), jnp.int32)
    for k in range(TOP_K):
        rk = jnp.sum(jnp.where(hots[k], prior, 0.0), axis=-1, keepdims=True)
        topi = jnp.where(lane == k, idxs[k], topi)
        gate = jnp.where(lane == k, es[k] / den, gate)
        rank = jnp.where(lane == k, rk.astype(jnp.int32), rank)
    topi_ref[...] = topi
    gate_ref[...] = gate
    rank_ref[...] = rank
    base_s[...] = base_s[...] + jnp.sum(cnt, axis=0, keepdims=True)
    cnt_ref[...] = base_s[...]


def _route(hl, router_w, router_b, seq_len, latent_only):
    per_b, src = _token_tiles(seq_len, TM_ROUTE, latent_only)
    t = hl.shape[0] // (seq_len // TM_ROUTE) * per_b
    w = jnp.pad(router_w, ((0, 0), (0, 128 - N_EXPERTS)))
    b = jnp.pad(router_b, (0, 128 - N_EXPERTS), constant_values=NEG_BIG).reshape(1, 128)
    tok = lambda i: (i, 0)
    return pl.pallas_call(
        _route_kernel,
        out_shape=(jax.ShapeDtypeStruct((t, 128), jnp.int32), jax.ShapeDtypeStruct((t, 128), F32),
                   jax.ShapeDtypeStruct((t, 128), jnp.int32), jax.ShapeDtypeStruct((1, 128), F32)),
        grid=(t // TM_ROUTE,),
        in_specs=[pl.BlockSpec((TM_ROUTE, D_MODEL), lambda i: (src(i), 0)),
                  pl.BlockSpec((D_MODEL, 128), lambda i: (0, 0)),
                  pl.BlockSpec((1, 128), lambda i: (0, 0))],
        out_specs=(pl.BlockSpec((TM_ROUTE, 128), tok), pl.BlockSpec((TM_ROUTE, 128), tok),
                   pl.BlockSpec((TM_ROUTE, 128), tok), pl.BlockSpec((1, 128), lambda i: (0, 0))),
        scratch_shapes=[pltpu.VMEM((1, 128), F32)],
        compiler_params=pltpu.CompilerParams(dimension_semantics=("arbitrary",)),
        name="route",
    )(hl, w, b)


def _dispatch_kernel(last_ref, dest_ref, h_ref, rows_ref, tile_s, zero_s, sem):
    i = pl.program_id(0)
    tm = h_ref.shape[0]

    @pl.when(i == 0)
    def _():
        zero_s[...] = jnp.zeros(zero_s.shape, F32)

        def blk_copy(blk):
            r = pl.multiple_of(blk * MOE_BM, MOE_BM)
            return pltpu.make_async_copy(zero_s, rows_ref.at[pl.ds(r, MOE_BM)], sem)

        for e in range(N_EXPERTS):
            blk_copy(last_ref[e]).start()
        for e in range(N_EXPERTS):
            blk_copy(last_ref[e]).wait()

        n_blocks = rows_ref.shape[0] // MOE_BM

        def clear_tail(blk, carry):
            blk_copy(blk).start()
            blk_copy(blk).wait()
            return carry

        lax.fori_loop(last_ref[N_EXPERTS], n_blocks, clear_tail, 0)

    tile_s[...] = h_ref[...].reshape(tile_s.shape)

    def issue(t, carry):
        for k in range(TOP_K):
            d = dest_ref[t * TOP_K + k]
            pltpu.make_async_copy(tile_s.at[t], rows_ref.at[d], sem).start(priority=k % 2)
        return carry

    lax.fori_loop(0, tm, issue, 0)
    for k in range(TOP_K):
        pltpu.make_async_copy(tile_s, rows_ref.at[pl.ds(0, tm)], sem).wait()


ROW_TILE = (8, D_MODEL // 8)


def _dispatch(last_blk, dest_flat, hl, n_rows, seq_len, latent_only):
    _, src = _token_tiles(seq_len, TM_DMA, latent_only)
    t = dest_flat.shape[0] // TOP_K
    grid_spec = pltpu.PrefetchScalarGridSpec(
        num_scalar_prefetch=1,
        grid=(t // TM_DMA,),
        in_specs=[pl.BlockSpec((TM_DMA * TOP_K,), lambda i, lb: (i,), memory_space=pltpu.SMEM),
                  pl.BlockSpec((TM_DMA, D_MODEL), lambda i, lb: (src(i), 0))],
        out_specs=pl.BlockSpec(memory_space=pl.ANY),
        scratch_shapes=[pltpu.VMEM((TM_DMA,) + ROW_TILE, F32), pltpu.VMEM((MOE_BM,) + ROW_TILE, F32),
                        pltpu.SemaphoreType.DMA(())],
    )
    return pl.pallas_call(
        _dispatch_kernel,
        out_shape=jax.ShapeDtypeStruct((n_rows,) + ROW_TILE, F32),
        grid_spec=grid_spec,
        compiler_params=pltpu.CompilerParams(dimension_semantics=("arbitrary",), has_side_effects=True),
        name="moe_dispatch",
    )(last_blk, dest_flat, hl)


def _expert_kernel(be_ref, nu_ref, x_ref, wgu_ref, bgu_ref, wd_ref, bd_ref, o_ref, wgu_s, wd_s):
    j = pl.program_id(0)
    d_ff = wd_ref.shape[0]
    used = j < nu_ref[0]
    new_expert = jnp.logical_or(j == 0, be_ref[j] != be_ref[jnp.maximum(j - 1, 0)])

    @pl.when(jnp.logical_and(used, new_expert))
    def _():
        rb = 256
        for r in range(0, D_MODEL, rb):
            wgu_s[r:r + rb, :] = wgu_ref[r:r + rb, :].astype(BF16)
        for r in range(0, d_ff, rb):
            wd_s[r:r + rb, :] = wd_ref[r:r + rb, :].astype(BF16)

    @pl.when(used)
    def _():
        x = x_ref[...].reshape(x_ref.shape[0], D_MODEL)
        gu = _dot(x.astype(BF16), wgu_s[...]) + bgu_ref[...]
        gate = jnp.minimum(gu[:, :d_ff], SWIGLU_LIMIT)
        up = jnp.clip(gu[:, d_ff:], -SWIGLU_LIMIT, SWIGLU_LIMIT)
        y = (up + 1.0) * (gate * _sigmoid(SWIGLU_ALPHA * gate))
        o_ref[...] = (_dot(y.astype(BF16), wd_s[...]) + bd_ref[...]).reshape(o_ref.shape)

    @pl.when(jnp.logical_not(used))
    def _():
        o_ref[...] = jnp.zeros(o_ref.shape, F32)


def _experts(block_e, n_used, rows, layer, w_gu, b_gu, w_down, b_down):
    n_rows = rows.shape[0]
    d_ff = w_down.shape[2]
    depth = w_gu.shape[0]
    row_blk = lambda j, be, nu: (jnp.minimum(j, nu[0] - 1), 0, 0)
    grid_spec = pltpu.PrefetchScalarGridSpec(
        num_scalar_prefetch=2,
        grid=(n_rows // MOE_BM,),
        in_specs=[
            pl.BlockSpec((MOE_BM,) + ROW_TILE, row_blk),
            pl.BlockSpec((None, None, D_MODEL, 2 * d_ff), lambda j, be, nu: (layer, be[j], 0, 0)),
            pl.BlockSpec((None, None, 1, 2 * d_ff), lambda j, be, nu: (layer, be[j], 0, 0)),
            pl.BlockSpec((None, None, d_ff, D_MODEL), lambda j, be, nu: (layer, be[j], 0, 0)),
            pl.BlockSpec((None, None, 1, D_MODEL), lambda j, be, nu: (layer, be[j], 0, 0)),
        ],
        out_specs=pl.BlockSpec((MOE_BM,) + ROW_TILE, lambda j, be, nu: (j, 0, 0)),
        scratch_shapes=[pltpu.VMEM((D_MODEL, 2 * d_ff), BF16), pltpu.VMEM((d_ff, D_MODEL), BF16)],
    )
    return pl.pallas_call(
        _expert_kernel,
        out_shape=jax.ShapeDtypeStruct((n_rows,) + ROW_TILE, F32),
        grid_spec=grid_spec,
        compiler_params=pltpu.CompilerParams(
            dimension_semantics=("arbitrary",), vmem_limit_bytes=VMEM_LIMIT),
        name="moe_experts",
    )(block_e, n_used, rows, w_gu, b_gu.reshape(depth, N_EXPERTS, 1, -1), w_down,
      b_down.reshape(depth, N_EXPERTS, 1, -1))


def _finish_kernel(dest_ref, gate_ref, xs_ref, mod_ref, rows_ref, o_ref, buf, sem, *, tiles_per_b, has_ctx):
    i = pl.program_id(0)
    tm = xs_ref.shape[0]

    def issue(t, carry):
        for k in range(TOP_K):
            d = dest_ref[t * TOP_K + k]
            pltpu.make_async_copy(rows_ref.at[d], buf.at[k, t], sem).start(priority=k % 2)
        return carry

    lax.fori_loop(0, tm, issue, 0)
    for k in range(TOP_K):
        pltpu.make_async_copy(rows_ref.at[pl.ds(0, tm)], buf.at[k], sem).wait()
    gate = gate_ref[...]
    y = gate[:, 0:1] * buf[0].reshape(tm, D_MODEL)
    for k in range(1, TOP_K):
        y = y + gate[:, k:k + 1] * buf[k].reshape(tm, D_MODEL)
    o_ref[...] = xs_ref[...] + _mod_rows(mod_ref, i, tiles_per_b, tm, 5, has_ctx) * y


def _finish(dest_flat, gates, xs, mod, out_rows, seq_len, latent_only):
    tiles_per_b, src = _token_tiles(seq_len, TM_DMA, latent_only)
    t = dest_flat.shape[0] // TOP_K
    return pl.pallas_call(
        functools.partial(_finish_kernel, tiles_per_b=tiles_per_b, has_ctx=not latent_only),
        out_shape=jax.ShapeDtypeStruct((t, D_MODEL), F32),
        grid=(t // TM_DMA,),
        in_specs=[pl.BlockSpec((TM_DMA * TOP_K,), lambda i: (i,), memory_space=pltpu.SMEM),
                  pl.BlockSpec((TM_DMA, 128), lambda i: (i, 0)),
                  pl.BlockSpec((TM_DMA, D_MODEL), lambda i: (src(i), 0)),
                  pl.BlockSpec((16, 6 * D_MODEL), lambda i: (0, 0)),
                  pl.BlockSpec(memory_space=pl.ANY)],
        out_specs=pl.BlockSpec((TM_DMA, D_MODEL), lambda i: (i, 0)),
        scratch_shapes=[pltpu.VMEM((TOP_K, TM_DMA) + ROW_TILE, F32), pltpu.SemaphoreType.DMA(())],
        compiler_params=pltpu.CompilerParams(
            dimension_semantics=("arbitrary",), vmem_limit_bytes=VMEM_LIMIT),
        name="moe_finish",
    )(dest_flat, gates, xs, mod, out_rows)


def _moe_plan(topi, rank, counts):
    t = topi.shape[0]
    padded = (counts + MOE_BM - 1) // MOE_BM * MOE_BM
    pad_end = jnp.cumsum(padded)
    pad_start = pad_end - padded
    onehot = topi[:, :, None] == jnp.arange(N_EXPERTS, dtype=jnp.int32)[None, None, :]
    dest = jnp.sum(jnp.where(onehot, pad_start[None, None, :], 0), axis=-1) + rank
    n_blocks = -(-(t * TOP_K + N_EXPERTS * (MOE_BM - 1)) // MOE_BM)
    starts = jnp.arange(n_blocks, dtype=jnp.int32) * MOE_BM
    block_e = jnp.sum(starts[:, None] >= pad_end[None, :], axis=-1).astype(jnp.int32)
    block_e = jnp.minimum(block_e, N_EXPERTS - 1)
    n_used = (pad_end[-1] // MOE_BM).astype(jnp.int32).reshape(1)
    last_blk = jnp.maximum(pad_end // MOE_BM - 1, 0).astype(jnp.int32)
    last_blk = jnp.concatenate([last_blk, n_used])
    return dest.reshape(-1).astype(jnp.int32), block_e, n_used, last_blk, n_blocks * MOE_BM


def kernel(x, c, ctx, c_ctx, w_ada, b_ada, norm_mix, norm_ffn, w_in, na_qk_gain, na_rpb, ret_decay, ret_gn,
           diff_qk_gain, diff_lam, diff_subln, dn_conv, dn_a_log, dn_dt_bias, dn_norm, w_branch, w_mgate,
           b_mgate, w_out, router_w, router_b, w_gu, b_gu, w_down, b_down):
    b, s, d = x.shape
    n_ctx = ctx.shape[1]
    seq = n_ctx + s
    depth = w_ada.shape[0]
    assert (d, n_ctx, s) == (D_MODEL, CTX_LEN, 2048) and b <= 8 and seq % TM == 0

    xs = jnp.concatenate([ctx, x], axis=1).reshape(b * seq, d)
    cc = jnp.zeros((16, d), F32).at[0:b].set(c).at[8].set(c_ctx)
    mods = _ada(cc, w_ada, b_ada)

    for l in range(depth):
        lam_init = 0.8 - 0.6 * math.exp(-0.3 * l)
        w_main = w_in[l, :, :D_IN_MAIN].astype(BF16)
        w_ab = jnp.pad(w_in[l, :, D_IN_MAIN:], ((0, 0), (0, 128 - (D_IN - D_IN_MAIN)))).astype(BF16)
        p, pab, hz = _in_proj(xs, mods[l], norm_mix[l], w_main, w_ab, seq)
        p3 = p.reshape(b, seq, D_IN_MAIN)
        ys = (_na_branch(p3, na_qk_gain[l], na_rpb[l]),
              _ret_branch(p3, ret_decay[l], ret_gn[l]),
              _diff_branch(p3, diff_qk_gain[l], diff_lam[l], diff_subln[l], lam_init),
              _dn_branch(p3, pab.reshape(b, seq, 128), dn_conv[l], dn_a_log[l], dn_dt_bias[l], dn_norm[l]))
        ys = tuple(y.reshape(b * seq, BRANCH_W) for y in ys)
        xs, hl = _combine(hz, ys, xs, mods[l], w_mgate[l].astype(BF16), b_mgate[l], w_branch[l].astype(BF16),
                          w_out[l].astype(BF16), norm_ffn[l], seq)
        last = l == depth - 1
        topi, gates, rank, cnt = _route(hl, router_w[l], router_b[l], seq, last)
        counts = cnt[0, :N_EXPERTS].astype(jnp.int32)
        dest, block_e, n_used, last_blk, n_rows = _moe_plan(topi[:, :TOP_K], rank[:, :TOP_K], counts)
        rows = _dispatch(last_blk, dest, hl, n_rows, seq, last)
        out_rows = _experts(block_e, n_used, rows, l, w_gu, b_gu, w_down, b_down)
        xs = _finish(dest, gates, xs, mods[l], out_rows, seq, last)

    return xs.reshape(b, s, d)
```

```python
import functools
import math

import numpy as np
import jax
import jax.numpy as jnp
from jax import lax
from jax.experimental import pallas as pl
from jax.experimental.pallas import tpu as pltpu

F32 = jnp.float32
BF16 = jnp.bfloat16

D_MODEL = 1024
GRID_W = 64
CTX_LEN = 256
HEAD_DIM = 64
N_BRANCH = 4
BRANCH_W = 512
NA_HEADS = 8
NA_WIN_R = 8
NA_WIN_C = 16
RET_HEADS = 4
RET_CHUNK = 128
DIFF_HEADS = 4
DN_HEADS = 4
DN_CONV = 5
DN_CHUNK = 64
N_EXPERTS = 32
TOP_K = 4
SWIGLU_LIMIT = 7.0
SWIGLU_ALPHA = 1.702
ROPE_BASE = 10000.0
EPS = 1e-6
D_IN = 6160
D_IN_MAIN = 6144
NEG_BIG = -1e30

VMEM_LIMIT = 56 * 1024 * 1024

TM = 768
TM_IN = 1152
TN_IN = 3072
TM_ROUTE = 256
TM_DMA = 256
MOE_BM = 256


def _dot(a, b):
    return jnp.dot(a, b, preferred_element_type=F32)


def _dot_nt(a, b):
    return lax.dot_general(a, b, (((1,), (1,)), ((), ())), preferred_element_type=F32)


def _dot_tn(a, b):
    return lax.dot_general(a, b, (((0,), (0,)), ((), ())), preferred_element_type=F32)


def _split_dot(x, w_bf16):
    hi = x.astype(BF16)
    lo = (x - hi.astype(F32)).astype(BF16)
    return _dot(hi, w_bf16) + _dot(lo, w_bf16)


def _group_ones(n, group):
    r = lax.broadcasted_iota(jnp.int32, (n, n), 0) // group
    c = lax.broadcasted_iota(jnp.int32, (n, n), 1) // group
    return jnp.where(r == c, 1.0, 0.0).astype(BF16)


def _sigmoid(x):
    return 1.0 / (1.0 + jnp.exp(-x))


def _silu(x):
    return x * _sigmoid(x)


def _ada_kernel(c_ref, w_ref, b_ref, o_ref):
    c = c_ref[...]
    a = _silu(c)
    o_ref[...] = _split_dot_w(a, w_ref[...]) + b_ref[...]


def _split_dot_w(a, w):
    ah = a.astype(BF16)
    al = (a - ah.astype(F32)).astype(BF16)
    wh = w.astype(BF16)
    wl = (w - wh.astype(F32)).astype(BF16)
    return _dot(ah, wh) + _dot(al, wh) + _dot(ah, wl)


def _ada(cc, w_ada, b_ada):
    depth = w_ada.shape[0]
    n = w_ada.shape[2]
    tn = 768
    return pl.pallas_call(
        _ada_kernel,
        out_shape=jax.ShapeDtypeStruct((depth, 16, n), F32),
        grid=(depth, n // tn),
        in_specs=[
            pl.BlockSpec((16, D_MODEL), lambda l, j: (0, 0)),
            pl.BlockSpec((None, D_MODEL, tn), lambda l, j: (l, 0, j)),
            pl.BlockSpec((None, 1, tn), lambda l, j: (l, 0, j)),
        ],
        out_specs=pl.BlockSpec((None, 16, tn), lambda l, j: (l, 0, j)),
        compiler_params=pltpu.CompilerParams(dimension_semantics=("arbitrary", "arbitrary")),
        name="ada_mod",
    )(cc, w_ada, b_ada.reshape(depth, 1, n))


def _mod_rows(mod_ref, tile, tiles_per_b, rows, which, has_ctx=True):
    b = tile // tiles_per_b
    lo = which * D_MODEL
    ml = mod_ref[pl.ds(b, 1), lo:lo + D_MODEL]
    if not has_ctx:
        return ml
    mc = mod_ref[8:9, lo:lo + D_MODEL]
    rid = lax.broadcasted_iota(jnp.int32, (rows, 1), 0)
    is_ctx = jnp.logical_and(tile % tiles_per_b == 0, rid < CTX_LEN)
    return jnp.where(is_ctx, mc, ml)


def _token_tiles(seq_len, tile, latent_only):
    per_b = seq_len // tile
    if not latent_only:
        return per_b, (lambda i: i)
    skip = CTX_LEN // tile
    assert skip * tile == CTX_LEN
    return per_b - skip, (lambda i: (i // (per_b - skip)) * per_b + skip + i % (per_b - skip))


def _rms_rows(x, g):
    ms = jnp.mean(x * x, axis=-1, keepdims=True)
    return x * lax.rsqrt(ms + EPS) * g


def _in_proj_kernel(x_ref, mod_ref, g_ref, w_ref, wab_ref, p_ref, pab_ref, hz_ref, hz_s, *, tiles_per_b):
    i = pl.program_id(0)
    j = pl.program_id(1)

    @pl.when(j == 0)
    def _():
        tm = x_ref.shape[0]
        y = _rms_rows(x_ref[...], g_ref[...])
        shift = _mod_rows(mod_ref, i, tiles_per_b, tm, 0)
        scale = _mod_rows(mod_ref, i, tiles_per_b, tm, 1)
        hz = (y * (1.0 + scale) + shift).astype(BF16)
        hz_s[...] = hz
        hz_ref[...] = hz
        pab_ref[...] = _dot(hz, wab_ref[...])

    p_ref[...] = _dot(hz_s[...], w_ref[...]).astype(p_ref.dtype)


def _in_proj(xs, mod, g, w_main, w_ab, seq_len):
    t = xs.shape[0]
    n = w_main.shape[1]
    tm = TM_IN
    tiles_per_b = seq_len // tm
    return pl.pallas_call(
        functools.partial(_in_proj_kernel, tiles_per_b=tiles_per_b),
        out_shape=(jax.ShapeDtypeStruct((t, n), BF16),
                   jax.ShapeDtypeStruct((t, 128), F32),
                   jax.ShapeDtypeStruct((t, D_MODEL), BF16)),
        grid=(t // tm, n // TN_IN),
        in_specs=[
            pl.BlockSpec((tm, D_MODEL), lambda i, j: (i, 0)),
            pl.BlockSpec((16, 6 * D_MODEL), lambda i, j: (0, 0)),
            pl.BlockSpec((1, D_MODEL), lambda i, j: (0, 0)),
            pl.BlockSpec((D_MODEL, TN_IN), lambda i, j: (0, j)),
            pl.BlockSpec((D_MODEL, 128), lambda i, j: (0, 0)),
        ],
        out_specs=(pl.BlockSpec((tm, TN_IN), lambda i, j: (i, j)),
                   pl.BlockSpec((tm, 128), lambda i, j: (i, 0)),
                   pl.BlockSpec((tm, D_MODEL), lambda i, j: (i, 0))),
        scratch_shapes=[pltpu.VMEM((tm, D_MODEL), BF16)],
        compiler_params=pltpu.CompilerParams(
            dimension_semantics=("arbitrary", "arbitrary"), vmem_limit_bytes=VMEM_LIMIT),
        name="in_proj",
    )(xs, mod, g.reshape(1, D_MODEL), w_main, w_ab)


def _group_rms(x, g, ones_bd, group):
    ss = _split_dot(x * x, ones_bd)
    return x * lax.rsqrt(ss * (1.0 / group) + EPS) * g


def _softmax_parts(parts):
    m = parts[0].max(axis=-1, keepdims=True)
    for s in parts[1:]:
        m = jnp.maximum(m, s.max(axis=-1, keepdims=True))
    es = [jnp.exp(s - m) for s in parts]
    l = es[0].sum(axis=-1, keepdims=True)
    for e in es[1:]:
        l = l + e.sum(axis=-1, keepdims=True)
    return es, l


NA_GROUP = 4
NA_WROWS_MID = NA_WIN_R + NA_GROUP - 1
NA_KEYS_EDGE = CTX_LEN + NA_WIN_R * GRID_W
NA_KEYS_MID = CTX_LEN + NA_WROWS_MID * GRID_W


def _na_bias_tables(w_ref, top_s, mid_s, bot_s):
    c_i = lax.broadcasted_iota(jnp.int32, (GRID_W, 128), 0)
    l_i = lax.broadcasted_iota(jnp.int32, (GRID_W, 128), 1)
    c0 = jnp.clip(c_i - NA_WIN_C // 2, 0, GRID_W - NA_WIN_C)
    in_lo = jnp.logical_and(l_i >= c0, l_i < c0 + NA_WIN_C)
    in_hi = jnp.logical_and(l_i - GRID_W >= c0, l_i - GRID_W < c0 + NA_WIN_C)
    neg = jnp.full((GRID_W, 128), NEG_BIG, F32)
    plans = (
        (top_s, NA_WIN_R, lambda u, j: j - u + NA_WIN_R - 1),
        (mid_s, NA_WROWS_MID, lambda u, j: j - u + NA_WIN_R // 2 - 1 if 0 <= j - u < NA_WIN_R else None),
        (bot_s, NA_WIN_R, lambda u, j: j - u + NA_WIN_R // 2 - 1),
    )
    for h in range(2):
        lo, hi = [], []
        for dr in range(2 * NA_WIN_R - 1):
            row = jnp.broadcast_to(w_ref[h, dr:dr + 1, :], (GRID_W, 128))
            lo.append(jnp.where(in_lo, pltpu.roll(row, 128 - (GRID_W - 1), 1, stride=1, stride_axis=0), neg))
            hi.append(jnp.where(in_hi, pltpu.roll(row, 1, 1, stride=1, stride_axis=0), neg))
        for tab, wrows, dr_of in plans:
            tab[h, :, 0:CTX_LEN] = jnp.zeros((NA_GROUP * GRID_W, CTX_LEN), F32)
            for u in range(NA_GROUP):
                rs = slice(u * GRID_W, (u + 1) * GRID_W)
                for j in range(0, wrows, 2):
                    d_lo = dr_of(u, j)
                    d_hi = dr_of(u, j + 1) if j + 1 < wrows else None
                    t_lo = neg if d_lo is None else lo[d_lo]
                    t_hi = neg if d_hi is None else hi[d_hi]
                    tile = jnp.where(l_i < GRID_W, t_lo, t_hi)
                    col = CTX_LEN + j * GRID_W
                    if j + 1 < wrows:
                        tab[h, rs, col:col + 128] = tile
                    else:
                        tab[h, rs, col:col + GRID_W] = tile[:, 0:GRID_W]


def _na_kernel(q_ref, k_ref, v_ref, gain_ref, w_ref, o_ref, qn_s, kn_s, top_s, mid_s, bot_s, kcat_s, vcat_s):
    seq = q_ref.shape[0]
    rows = (seq - CTX_LEN) // GRID_W

    @pl.when(pl.program_id(1) == 0)
    def _():
        _na_bias_tables(w_ref, top_s, mid_s, bot_s)

    ones_bd = _group_ones(128, HEAD_DIM)
    gq = gain_ref[0:1, :]
    gk = gain_ref[1:2, :]
    scale = HEAD_DIM ** -0.5
    blk = 256
    for c in range(seq // blk):
        sl = slice(c * blk, (c + 1) * blk)
        qn_s[sl, :] = (_group_rms(q_ref[sl, :].astype(F32), gq, ones_bd, HEAD_DIM) * scale).astype(BF16)
        kn_s[sl, :] = _group_rms(k_ref[sl, :].astype(F32), gk, ones_bd, HEAD_DIM).astype(BF16)

    lane = lax.broadcasted_iota(jnp.int32, (1, 128), 1)
    head_mask = [lane < HEAD_DIM, lane >= HEAD_DIM]
    kc = kn_s[0:CTX_LEN, :]
    vc = v_ref[0:CTX_LEN, :]

    qc = qn_s[0:CTX_LEN, :]
    outs = []
    for h in range(2):
        qm = jnp.where(head_mask[h], qc, jnp.zeros_like(qc))
        (e,), l = _softmax_parts([_dot_nt(qm, kc)])
        outs.append(_dot(e.astype(BF16), vc) / l)
    o_ref[0:CTX_LEN, :] = jnp.where(head_mask[0], outs[0], outs[1]).astype(o_ref.dtype)

    assert NA_GROUP == NA_WIN_R // 2 and rows % NA_GROUP == 0
    kcat_s[0:CTX_LEN, :] = kc
    vcat_s[0:CTX_LEN, :] = vc
    gq_rows = NA_GROUP * GRID_W
    for g in range(rows // NA_GROUP):
        r = g * NA_GROUP
        if g == 0:
            tab, w0, nk = top_s, 0, NA_KEYS_EDGE
        elif g == rows // NA_GROUP - 1:
            tab, w0, nk = bot_s, rows - NA_WIN_R, NA_KEYS_EDGE
        else:
            tab, w0, nk = mid_s, r - NA_WIN_R // 2, NA_KEYS_MID
        k0 = CTX_LEN + w0 * GRID_W
        kcat_s[CTX_LEN:nk, :] = kn_s[k0:k0 + nk - CTX_LEN, :]
        vcat_s[CTX_LEN:nk, :] = v_ref[k0:k0 + nk - CTX_LEN, :]
        q0 = CTX_LEN + r * GRID_W
        qg = qn_s[q0:q0 + gq_rows, :]
        res = []
        for h in range(2):
            qm = jnp.where(head_mask[h], qg, jnp.zeros_like(qg))
            (e,), l = _softmax_parts([_dot_nt(qm, kcat_s[0:nk, :]) + tab[h, :, 0:nk]])
            res.append(_dot(e.astype(BF16), vcat_s[0:nk, :]) / l)
        o_ref[q0:q0 + gq_rows, :] = jnp.where(head_mask[0], res[0], res[1]).astype(o_ref.dtype)


def _na_branch(p3, qk_gain, rpb):
    b, seq, _ = p3.shape
    gain2 = jnp.tile(qk_gain, (1, 2))
    lo = GRID_W - NA_WIN_C
    w = jnp.pad(rpb, ((0, 0), (0, 0), (lo, 128 - lo - rpb.shape[2])))
    nb = BRANCH_W // 128
    return pl.pallas_call(
        _na_kernel,
        out_shape=jax.ShapeDtypeStruct((b, seq, BRANCH_W), BF16),
        grid=(nb, b),
        in_specs=[
            pl.BlockSpec((None, seq, 128), lambda hp, i: (i, 0, hp)),
            pl.BlockSpec((None, seq, 128), lambda hp, i: (i, 0, nb + hp)),
            pl.BlockSpec((None, seq, 128), lambda hp, i: (i, 0, 2 * nb + hp)),
            pl.BlockSpec((2, 128), lambda hp, i: (0, 0)),
            pl.BlockSpec((2, 2 * NA_WIN_R - 1, 128), lambda hp, i: (hp, 0, 0)),
        ],
        out_specs=pl.BlockSpec((None, seq, 128), lambda hp, i: (i, 0, hp)),
        scratch_shapes=[pltpu.VMEM((seq, 128), BF16), pltpu.VMEM((seq, 128), BF16),
                        pltpu.VMEM((2, NA_GROUP * GRID_W, NA_KEYS_EDGE), F32),
                        pltpu.VMEM((2, NA_GROUP * GRID_W, NA_KEYS_MID), F32),
                        pltpu.VMEM((2, NA_GROUP * GRID_W, NA_KEYS_EDGE), F32),
                        pltpu.VMEM((NA_KEYS_MID, 128), BF16), pltpu.VMEM((NA_KEYS_MID, 128), BF16)],
        compiler_params=pltpu.CompilerParams(
            dimension_semantics=("arbitrary", "arbitrary"), vmem_limit_bytes=VMEM_LIMIT),
        name="na_branch",
    )(p3, p3, p3, gain2, w)


DIFF_TQ = 256


def _rope_lanes(x, cos, sin_signed, half, first_mask):
    n = x.shape[-1]
    fwd = pltpu.roll(x, n - half, 1)
    bwd = pltpu.roll(x, half, 1)
    return x * cos + jnp.where(first_mask, fwd, bwd) * sin_signed


def _diff_kernel(q_ref, k_ref, v_ref, gain_ref, cos_ref, sin_ref, lam_ref, sub_ref, o_ref,
                 qn_s, kn_s, *, lam_init):
    seq = q_ref.shape[0]
    ones_bd = _group_ones(128, HEAD_DIM)
    gq = gain_ref[0:1, :]
    gk = gain_ref[1:2, :]
    scale = HEAD_DIM ** -0.5
    lane = lax.broadcasted_iota(jnp.int32, (1, 128), 1)
    first = (lane % 32) < 16
    blk = 256
    for c in range(seq // blk):
        sl = slice(c * blk, (c + 1) * blk)
        qn = _group_rms(q_ref[sl, :].astype(F32), gq, ones_bd, HEAD_DIM)
        kn = _group_rms(k_ref[sl, :].astype(F32), gk, ones_bd, HEAD_DIM)
        if c * blk >= CTX_LEN:
            ps = slice(c * blk - CTX_LEN, (c + 1) * blk - CTX_LEN)
            qn = _rope_lanes(qn, cos_ref[ps, :], sin_ref[ps, :], 16, first)
            kn = _rope_lanes(kn, cos_ref[ps, :], sin_ref[ps, :], 16, first)
        qn_s[sl, :] = (qn * scale).astype(BF16)
        kn_s[sl, :] = kn.astype(BF16)

    lp = lam_ref[...]
    lam = (jnp.exp(jnp.sum(lp[0:1, :] * lp[1:2, :], axis=-1, keepdims=True))
           - jnp.exp(jnp.sum(lp[2:3, :] * lp[3:4, :], axis=-1, keepdims=True)) + lam_init)
    comp_mask = [lane < HEAD_DIM, lane >= HEAD_DIM]
    sub = sub_ref[...]

    def attend(q, keys, vals):
        ps = []
        for c in range(2):
            qm = jnp.where(comp_mask[c], q, jnp.zeros_like(q))
            (e,), l = _softmax_parts([_dot_nt(qm, keys)])
            ps.append((e, l))
        a = ps[0][0] * (1.0 / ps[0][1]) - ps[1][0] * (lam / ps[1][1])
        o = _dot(a.astype(BF16), vals)
        return _rms_rows(o, sub) * (1.0 - lam_init)

    o_ref[0:CTX_LEN, :] = attend(qn_s[0:CTX_LEN, :], kn_s[0:CTX_LEN, :], v_ref[0:CTX_LEN, :]).astype(o_ref.dtype)

    def q_step(i, carry):
        q0 = pl.multiple_of(CTX_LEN + i * DIFF_TQ, DIFF_TQ)
        o_ref[pl.ds(q0, DIFF_TQ), :] = attend(qn_s[pl.ds(q0, DIFF_TQ), :], kn_s[...], v_ref[...]).astype(o_ref.dtype)
        return carry

    lax.fori_loop(0, (seq - CTX_LEN) // DIFF_TQ, q_step, 0)


def _rope_angles(pos, dim):
    inv = ROPE_BASE ** (-np.arange(0, dim, 2, dtype=np.float32) / dim)
    return pos.astype(np.float32)[:, None] * inv[None, :]


def _diff_rope_tables(s):
    t = np.arange(s)
    ang_r = _rope_angles(t // GRID_W, HEAD_DIM // 2)
    ang_c = _rope_angles(t % GRID_W, HEAD_DIM // 2)
    ang64 = np.concatenate([ang_r, ang_r, ang_c, ang_c], axis=1)
    sign64 = np.concatenate([-np.ones(16), np.ones(16), -np.ones(16), np.ones(16)]).astype(np.float32)
    ang = jnp.asarray(np.tile(ang64, (1, 2)))
    return jnp.cos(ang), jnp.sin(ang) * jnp.asarray(np.tile(sign64, 2))[None, :]


def _diff_branch(p3, qk_gain, lam_p, subln, lam_init):
    b, seq, _ = p3.shape
    s = seq - CTX_LEN
    cos, sin = _diff_rope_tables(s)
    gain2 = jnp.tile(qk_gain, (1, 2))
    base = 3072 // 128
    nh = DIFF_HEADS
    return pl.pallas_call(
        functools.partial(_diff_kernel, lam_init=lam_init),
        out_shape=jax.ShapeDtypeStruct((b, seq, BRANCH_W), BF16),
        grid=(b, nh),
        in_specs=[
            pl.BlockSpec((None, seq, 128), lambda i, h: (i, 0, base + h)),
            pl.BlockSpec((None, seq, 128), lambda i, h: (i, 0, base + nh + h)),
            pl.BlockSpec((None, seq, 128), lambda i, h: (i, 0, base + 2 * nh + h)),
            pl.BlockSpec((2, 128), lambda i, h: (0, 0)),
            pl.BlockSpec((s, 128), lambda i, h: (0, 0)),
            pl.BlockSpec((s, 128), lambda i, h: (0, 0)),
            pl.BlockSpec((4, HEAD_DIM), lambda i, h: (0, 0)),
            pl.BlockSpec((1, 128), lambda i, h: (0, 0)),
        ],
        out_specs=pl.BlockSpec((None, seq, 128), lambda i, h: (i, 0, h)),
        scratch_shapes=[pltpu.VMEM((seq, 128), BF16), pltpu.VMEM((seq, 128), BF16)],
        compiler_params=pltpu.CompilerParams(
            dimension_semantics=("arbitrary", "arbitrary"), vmem_limit_bytes=VMEM_LIMIT),
        name="diff_branch",
    )(p3, p3, p3, gain2, cos, sin, lam_p, subln.reshape(1, 128))


def _log_sigmoid(x):
    return jnp.minimum(x, 0.0) - jnp.log(1.0 + jnp.exp(-jnp.abs(x)))


def _ret_kernel(q_ref, k_ref, v_ref, g_ref, cos_ref, sin_ref, dec_ref, gn_ref, o_ref,
                q_s, k_s, o_s):
    seq = q_ref.shape[0]
    c = RET_CHUNK
    n_ctx_chunks = CTX_LEN // c
    n_chunks = seq // c
    hw = RET_HEADS * HEAD_DIM
    vw = RET_HEADS * 2 * HEAD_DIM
    lane = lax.broadcasted_iota(jnp.int32, (1, hw), 1)
    first = (lane % HEAD_DIM) < (HEAD_DIM // 2)
    blk = 256
    kscale = HEAD_DIM ** -0.5
    for cb in range(seq // blk):
        sl = slice(cb * blk, (cb + 1) * blk)
        q = q_ref[sl, :].astype(F32)
        k = k_ref[sl, :].astype(F32)
        if cb * blk >= CTX_LEN:
            ps = slice(cb * blk - CTX_LEN, (cb + 1) * blk - CTX_LEN)
            q = _rope_lanes(q, cos_ref[ps, :], sin_ref[ps, :], HEAD_DIM // 2, first)
            k = _rope_lanes(k, cos_ref[ps, :], sin_ref[ps, :], HEAD_DIM // 2, first)
        q_s[sl, :] = q
        k_s[sl, :] = k * kscale

    log_g = _log_sigmoid(dec_ref[...])
    pos_r = lax.broadcasted_iota(jnp.int32, (c, c), 0).astype(F32)
    pos_c = lax.broadcasted_iota(jnp.int32, (c, c), 1).astype(F32)
    pos = lax.broadcasted_iota(jnp.int32, (c, 1), 0).astype(F32)
    head_of_q = lax.broadcasted_iota(jnp.int32, (1, hw), 1) // HEAD_DIM
    head_of_v = lax.broadcasted_iota(jnp.int32, (1, vw), 1) // (2 * HEAD_DIM)
    row_head = lax.broadcasted_iota(jnp.int32, (hw, vw), 0) // HEAD_DIM
    col_head = lax.broadcasted_iota(jnp.int32, (hw, vw), 1) // (2 * HEAD_DIM)
    state_mask = row_head == col_head

    intra, q_dec, k_dec, c_dec = [], [], [], []
    for d in range(2):
        rel = (pos_r - pos_c) if d == 0 else (pos_c - pos_r)
        lg_q = jnp.zeros((1, hw), F32)
        lg_v = jnp.zeros((1, vw), F32)
        per_head = []
        for h in range(RET_HEADS):
            lg = log_g[d:d + 1, h:h + 1]
            per_head.append(jnp.where(rel >= 0, jnp.exp(lg * jnp.maximum(rel, 0.0)), 0.0))
            lg_q = jnp.where(head_of_q == h, lg, lg_q)
            lg_v = jnp.where(head_of_v == h, lg, lg_v)
        intra.append(per_head)
        qpos = (pos + 1.0) if d == 0 else (c - pos)
        kpos = (c - 1.0 - pos) if d == 0 else pos
        q_dec.append(jnp.exp(lg_q * qpos))
        k_dec.append(jnp.exp(lg_q * kpos))
        c_dec.append(jnp.exp(lg_v * c))

    head_mask = [head_of_q == h for h in range(RET_HEADS)]

    def chunk_start(d, n):
        if d == 0:
            return n * c
        rev_ctx = (n_ctx_chunks - 1 - n) * c
        rev_lat = CTX_LEN + (n_chunks - 1 - n) * c
        return jnp.where(n < n_ctx_chunks, rev_ctx, rev_lat)

    def step(n, states):
        new_states = []
        for d in range(2):
            r0 = pl.multiple_of(chunk_start(d, n), c)
            s = states[d]
            qi = q_s[pl.ds(r0, c), :]
            ki = k_s[pl.ds(r0, c), :]
            vi = v_ref[pl.ds(r0, c), :].astype(BF16)
            kb = ki.astype(BF16)
            cross = _dot((qi * q_dec[d]).astype(BF16), s.astype(BF16))
            parts = []
            for h in range(RET_HEADS):
                qm = jnp.where(head_mask[h], qi, 0.0).astype(BF16)
                att = _dot_nt(qm, kb) * intra[d][h]
                parts.append(_dot(att.astype(BF16), vi[:, h * 128:(h + 1) * 128]))
            o = jnp.concatenate(parts, axis=1) + cross
            o_s[pl.ds(r0, c), :] = o_s[pl.ds(r0, c), :] + o
            upd = _dot_tn((ki * k_dec[d]).astype(BF16), vi)
            new_states.append(s * c_dec[d] + jnp.where(state_mask, upd, 0.0))
        return tuple(new_states)

    o_s[...] = jnp.zeros(o_s.shape, F32)
    zero = jnp.zeros((hw, vw), F32)
    lax.fori_loop(0, n_chunks, step, (zero, zero))

    gn = gn_ref[...]
    for cb in range(seq // blk):
        sl = slice(cb * blk, (cb + 1) * blk)
        outs = []
        for h in range(RET_HEADS):
            hs = slice(h * 128, (h + 1) * 128)
            y = o_s[sl, hs]
            yc = y - jnp.mean(y, axis=-1, keepdims=True)
            yn = yc * lax.rsqrt(jnp.mean(yc * yc, axis=-1, keepdims=True) + EPS) * gn
            outs.append(yn * _silu(g_ref[sl, hs].astype(F32)))
        o_ref[sl, :] = jnp.concatenate(outs, axis=1).astype(o_ref.dtype)


def _ret_rope_tables(s):
    ang = _rope_angles(np.arange(s), HEAD_DIM)
    ang64 = np.concatenate([ang, ang], axis=1)
    sign64 = np.concatenate([-np.ones(32), np.ones(32)]).astype(np.float32)
    ang = jnp.asarray(np.tile(ang64, (1, RET_HEADS)))
    return jnp.cos(ang), jnp.sin(ang) * jnp.asarray(np.tile(sign64, RET_HEADS))[None, :]


def _ret_branch(p3, decay, gn_gain):
    b, seq, _ = p3.shape
    s = seq - CTX_LEN
    cos, sin = _ret_rope_tables(s)
    return pl.pallas_call(
        _ret_kernel,
        out_shape=jax.ShapeDtypeStruct((b, seq, BRANCH_W), BF16),
        grid=(b,),
        in_specs=[
            pl.BlockSpec((None, seq, 256), lambda i: (i, 0, 6)),
            pl.BlockSpec((None, seq, 256), lambda i: (i, 0, 7)),
            pl.BlockSpec((None, seq, 512), lambda i: (i, 0, 4)),
            pl.BlockSpec((None, seq, 512), lambda i: (i, 0, 5)),
            pl.BlockSpec((s, 256), lambda i: (0, 0)),
            pl.BlockSpec((s, 256), lambda i: (0, 0)),
            pl.BlockSpec((2, RET_HEADS), lambda i: (0, 0)),
            pl.BlockSpec((1, 128), lambda i: (0, 0)),
        ],
        out_specs=pl.BlockSpec((None, seq, BRANCH_W), lambda i: (i, 0, 0)),
        scratch_shapes=[pltpu.VMEM((seq, 256), F32), pltpu.VMEM((seq, 256), F32),
                        pltpu.VMEM((seq, 512), F32)],
        compiler_params=pltpu.CompilerParams(
            dimension_semantics=("arbitrary",), vmem_limit_bytes=VMEM_LIMIT),
        name="ret_branch",
    )(p3, p3, p3, p3, cos, sin, decay, gn_gain.reshape(1, 128))


DN_PAD = 8
DN_ROWS = 4 * DN_CHUNK


def _softplus(x):
    return jnp.maximum(x, 0.0) + jnp.log(1.0 + jnp.exp(-jnp.abs(x)))


def _dn_conv_silu(pad_s, w_ref, seq, width):
    blk = 256
    outs = []
    for cb in range(seq // blk):
        base = DN_PAD + cb * blk + (DN_PAD if cb * blk >= CTX_LEN else 0)
        ext = pad_s[base - DN_PAD:base + blk + DN_PAD, 0:width]
        n = blk + 2 * DN_PAD
        acc = ext * w_ref[DN_CONV // 2:DN_CONV // 2 + 1, :]
        for j in range(DN_CONV):
            if j == DN_CONV // 2:
                continue
            acc = acc + pltpu.roll(ext, (DN_CONV // 2 - j) % n, 0) * w_ref[j:j + 1, :]
        outs.append(_silu(acc[DN_PAD:DN_PAD + blk, :]))
    return outs


def _dn_fill_pad(pad_s, refs, seq):
    off = 0
    for ref in refs:
        w = ref.shape[1]
        pad_s[DN_PAD:DN_PAD + CTX_LEN, off:off + w] = ref[0:CTX_LEN, :].astype(F32)
        pad_s[2 * DN_PAD + CTX_LEN:2 * DN_PAD + seq, off:off + w] = ref[CTX_LEN:seq, :].astype(F32)
        off += w


def _dn_kernel(q_ref, k_ref, v_ref, g_ref, ab_ref, wq_ref, wk_ref, wv_ref, par_ref, ng_ref, o_ref,
               pad_s, qk_s, v_s, gb_s, o_s, ru_s, rw_s, rq_s, ra_s, rk_s, rg_s):
    seq = q_ref.shape[0]
    hp = pl.program_id(1)
    c = DN_CHUNK
    n_ctx_chunks = CTX_LEN // c
    n_chunks = seq // c
    blk = 256

    zeros_pad = jnp.zeros((DN_PAD, 256), F32)
    pad_s[0:DN_PAD, :] = zeros_pad
    pad_s[DN_PAD + CTX_LEN:2 * DN_PAD + CTX_LEN, :] = zeros_pad
    pad_s[2 * DN_PAD + seq:3 * DN_PAD + seq, :] = zeros_pad
    ones_bd = _group_ones(128, HEAD_DIM)

    _dn_fill_pad(pad_s, [q_ref], seq)
    for cb, x in enumerate(_dn_conv_silu(pad_s, wq_ref, seq, 128)):
        ss = _split_dot(x * x, ones_bd)
        qk_s[cb * blk:(cb + 1) * blk, 0:128] = x * lax.rsqrt(ss + EPS) * (HEAD_DIM ** -0.5)
    _dn_fill_pad(pad_s, [k_ref], seq)
    for cb, x in enumerate(_dn_conv_silu(pad_s, wk_ref, seq, 128)):
        ss = _split_dot(x * x, ones_bd)
        qk_s[cb * blk:(cb + 1) * blk, 128:256] = x * lax.rsqrt(ss + EPS)
    _dn_fill_pad(pad_s, [v_ref], seq)
    for cb, x in enumerate(_dn_conv_silu(pad_s, wv_ref, seq, 256)):
        v_s[cb * blk:(cb + 1) * blk, :] = x

    lane = lax.broadcasted_iota(jnp.int32, (1, 128), 1)
    a_log = par_ref[0:1, :]
    dt_bias = par_ref[1:2, :]
    shift = (128 - 2 * hp) % 128
    for cb in range(seq // blk):
        ab = ab_ref[cb * blk:(cb + 1) * blk, :]
        gdec = -jnp.exp(a_log) * _softplus(ab + dt_bias)
        beta = _sigmoid(ab)
        gb = jnp.where(lane < 2 * DN_HEADS, gdec, beta)
        gb_s[cb * blk:(cb + 1) * blk, :] = pltpu.roll(gb, shift, 1)

    ri = lax.broadcasted_iota(jnp.int32, (DN_ROWS, DN_ROWS), 0)
    ci = lax.broadcasted_iota(jnp.int32, (DN_ROWS, DN_ROWS), 1)
    same_blk = (ri // c) == (ci // c)
    is_fwd = ri < 2 * c
    rp, cp = ri % c, ci % c
    is_bwd = jnp.logical_not(is_fwd)
    strict = jnp.logical_and(same_blk, jnp.logical_or(jnp.logical_and(is_fwd, rp > cp),
                                                      jnp.logical_and(is_bwd, rp < cp)))
    incl = jnp.logical_or(strict, ri == ci)
    eye = jnp.where(ri == ci, 1.0, 0.0)

    r64 = lax.broadcasted_iota(jnp.int32, (c, c), 0)
    c64 = lax.broadcasted_iota(jnp.int32, (c, c), 1)
    tri_lo = jnp.where(r64 >= c64, 1.0, 0.0).astype(BF16)
    tri_up = jnp.where(r64 <= c64, 1.0, 0.0).astype(BF16)
    ones64 = jnp.ones((c, c), BF16)

    def split_left(m, x):
        hi = x.astype(BF16)
        lo = (x - hi.astype(F32)).astype(BF16)
        return _dot(m, hi) + _dot(m, lo)

    def chunk_start(d, n):
        if d == 0:
            return n * c
        rev_ctx = (n_ctx_chunks - 1 - n) * c
        rev_lat = CTX_LEN + (n_chunks - 1 - n) * c
        return jnp.where(n < n_ctx_chunks, rev_ctx, rev_lat)

    def stacked_cols(mats, cols, width):
        return jnp.concatenate(
            [jnp.broadcast_to(mats[d][:, cols[d][h]:cols[d][h] + 1], (c, width))
             for d in range(2) for h in range(2)], axis=0)

    def chunk_rows(n):
        n = jnp.asarray(n, jnp.int32)
        return [pl.multiple_of(chunk_start(d, n), c) for d in range(2)]

    def prep(n, slot):
        r0 = chunk_rows(n)
        qk = [qk_s[pl.ds(r0[d], c), :] for d in range(2)]
        vv = [v_s[pl.ds(r0[d], c), :] for d in range(2)]
        gb = [gb_s[pl.ds(r0[d], c), :] for d in range(2)]
        gc = [split_left(tri_lo, gb[0]), split_left(tri_up, gb[1])]
        gt = [split_left(ones64, gb[d]) for d in range(2)]
        g_cols = [[0, 1], [4, 5]]
        b_cols = [[8, 9], [12, 13]]
        gcb = stacked_cols(gc, g_cols, DN_ROWS)
        gtb = stacked_cols(gt, g_cols, DN_ROWS)
        bb = stacked_cols(gb, b_cols, DN_ROWS)

        qrow = jnp.concatenate([qk[d][:, 0:128] for d in range(2) for _ in range(2)], axis=0)
        krow = jnp.concatenate([qk[d][:, 128:256] for d in range(2) for _ in range(2)], axis=0)
        qx = jnp.where(same_blk, jnp.concatenate([qrow, qrow], axis=1), 0.0)
        kx = jnp.where(same_blk, jnp.concatenate([krow, krow], axis=1), 0.0)
        vx = jnp.concatenate([vv[d][:, h * 128:(h + 1) * 128] for d in range(2) for h in range(2)], axis=0)

        kxb = kx.astype(BF16)
        kk = _dot_nt(kxb, kxb)
        qkm = _dot_nt(qx.astype(BF16), kxb)
        diff = gcb - jnp.transpose(gcb)
        decay = jnp.where(incl, jnp.exp(jnp.where(incl, diff, 0.0)), 0.0)
        lower = jnp.where(strict, bb * kk * decay, 0.0)

        m = -lower
        x = eye + m
        for _ in range(5):
            mb = m.astype(BF16)
            m = _dot(mb, mb)
            x = x + _dot(x.astype(BF16), m.astype(BF16))
        tb = x.astype(BF16)

        eg = jnp.exp(gcb)
        ru_s[slot] = _dot(tb, (vx * bb[:, 0:128]).astype(BF16))
        rw_s[slot] = _dot(tb, (kx * bb * eg).astype(BF16)).astype(BF16)
        rq_s[slot] = (qx * eg).astype(BF16)
        ra_s[slot] = (qkm * decay).astype(BF16)
        rk_s[slot] = (kx * jnp.exp(gtb - gcb)).astype(BF16)
        rg_s[slot] = jnp.exp(gtb[:, 0:128])

    def apply(n, slot, s):
        r0 = chunk_rows(n)
        sb = s.astype(BF16)
        v_new = (ru_s[slot] - _dot(rw_s[slot], sb)).astype(BF16)
        o = _dot(rq_s[slot], sb) + _dot(ra_s[slot], v_new)
        s_new = s * rg_s[slot] + _dot_tn(rk_s[slot], v_new)
        for d in range(2):
            od = jnp.concatenate([o[(2 * d + h) * c:(2 * d + h + 1) * c, :] for h in range(2)], axis=1)
            o_s[d, pl.ds(r0[d], c), :] = od
        return s_new

    prep(0, 0)
    prep(1, 1)

    def group(g, s):
        n = 4 * g
        prep(n + 2, 2)
        prep(n + 3, 3)
        s = apply(n, 0, s)
        s = apply(n + 1, 1, s)
        prep(jnp.minimum(n + 4, n_chunks - 1), 0)
        prep(jnp.minimum(n + 5, n_chunks - 1), 1)
        s = apply(n + 2, 2, s)
        s = apply(n + 3, 3, s)
        return s

    assert n_chunks % 4 == 0
    lax.fori_loop(0, n_chunks // 4, group, jnp.zeros((DN_ROWS, 128), F32))

    ng = ng_ref[...]
    for cb in range(seq // blk):
        sl = slice(cb * blk, (cb + 1) * blk)
        outs = []
        for h in range(2):
            hs = slice(h * 128, (h + 1) * 128)
            outs.append(_rms_rows(o_s[0, sl, hs] + o_s[1, sl, hs], ng) * _silu(g_ref[sl, hs].astype(F32)))
        o_ref[sl, :] = jnp.concatenate(outs, axis=1).astype(o_ref.dtype)


def _dn_branch(p3, pab3, conv_w, a_log, dt_bias, norm_gain):
    b, seq, _ = p3.shape
    par = jnp.zeros((2, 128), F32)
    par = par.at[0, 0:2 * DN_HEADS].set(a_log.reshape(-1)).at[1, 0:2 * DN_HEADS].set(dt_bias.reshape(-1))
    qb, kb_, vb, gb_ = 4608 // 128, 4864 // 128, 5120 // 256, 5632 // 256
    ring = lambda r, w, dt: pltpu.VMEM((4, r, w), dt)
    return pl.pallas_call(
        _dn_kernel,
        out_shape=jax.ShapeDtypeStruct((b, seq, BRANCH_W), BF16),
        grid=(b, 2),
        in_specs=[
            pl.BlockSpec((None, seq, 128), lambda i, hp: (i, 0, qb + hp)),
            pl.BlockSpec((None, seq, 128), lambda i, hp: (i, 0, kb_ + hp)),
            pl.BlockSpec((None, seq, 256), lambda i, hp: (i, 0, vb + hp)),
            pl.BlockSpec((None, seq, 256), lambda i, hp: (i, 0, gb_ + hp)),
            pl.BlockSpec((None, seq, 128), lambda i, hp: (i, 0, 0)),
            pl.BlockSpec((DN_CONV, 128), lambda i, hp: (0, hp)),
            pl.BlockSpec((DN_CONV, 128), lambda i, hp: (0, 2 + hp)),
            pl.BlockSpec((DN_CONV, 256), lambda i, hp: (0, 2 + hp)),
            pl.BlockSpec((2, 128), lambda i, hp: (0, 0)),
            pl.BlockSpec((1, 128), lambda i, hp: (0, 0)),
        ],
        out_specs=pl.BlockSpec((None, seq, 256), lambda i, hp: (i, 0, hp)),
        scratch_shapes=[pltpu.VMEM((seq + 3 * DN_PAD, 256), F32), pltpu.VMEM((seq, 256), F32),
                        pltpu.VMEM((seq, 256), F32), pltpu.VMEM((seq, 128), F32),
                        pltpu.VMEM((2, seq, 256), F32),
                        ring(DN_ROWS, 128, F32), ring(DN_ROWS, DN_ROWS, BF16), ring(DN_ROWS, DN_ROWS, BF16),
                        ring(DN_ROWS, DN_ROWS, BF16), ring(DN_ROWS, DN_ROWS, BF16), ring(DN_ROWS, 128, F32)],
        compiler_params=pltpu.CompilerParams(
            dimension_semantics=("arbitrary", "arbitrary"), vmem_limit_bytes=VMEM_LIMIT),
        name="dn_branch",
    )(p3, p3, p3, p3, pab3, conv_w, conv_w, conv_w, par, norm_gain.reshape(1, 128))


def _combine_kernel(hz_ref, y0_ref, y1_ref, y2_ref, y3_ref, xs_ref, mod_ref, wg_ref, bg_ref, wb_ref,
                    wo_ref, gf_ref, xo_ref, hl_ref, *, tiles_per_b):
    i = pl.program_id(0)
    tm = xs_ref.shape[0]
    hz = hz_ref[...]
    acc = jnp.zeros((tm, D_MODEL), F32)
    for n, y_ref in enumerate((y0_ref, y1_ref, y2_ref, y3_ref)):
        cs = slice(n * D_MODEL, (n + 1) * D_MODEL)
        gate = _sigmoid(_dot(hz, wg_ref[:, cs]) + bg_ref[:, cs])
        acc = acc + gate * _dot(y_ref[...], wb_ref[n])
    mix = _dot(acc.astype(BF16), wo_ref[...])
    xn = xs_ref[...] + _mod_rows(mod_ref, i, tiles_per_b, tm, 2) * mix
    xo_ref[...] = xn
    h = _rms_rows(xn, gf_ref[...])
    hl_ref[...] = h * (1.0 + _mod_rows(mod_ref, i, tiles_per_b, tm, 4)) + _mod_rows(mod_ref, i, tiles_per_b, tm, 3)


def _combine(hz, ys, xs, mod, w_mgate, b_mgate, w_branch, w_out, g_ffn, seq_len):
    t = xs.shape[0]
    tiles_per_b = seq_len // TM
    tok = lambda i: (i, 0)
    fix2 = lambda i: (0, 0)
    return pl.pallas_call(
        functools.partial(_combine_kernel, tiles_per_b=tiles_per_b),
        out_shape=(jax.ShapeDtypeStruct((t, D_MODEL), F32), jax.ShapeDtypeStruct((t, D_MODEL), F32)),
        grid=(t // TM,),
        in_specs=[
            pl.BlockSpec((TM, D_MODEL), tok),
            pl.BlockSpec((TM, BRANCH_W), tok), pl.BlockSpec((TM, BRANCH_W), tok),
            pl.BlockSpec((TM, BRANCH_W), tok), pl.BlockSpec((TM, BRANCH_W), tok),
            pl.BlockSpec((TM, D_MODEL), tok),
            pl.BlockSpec((16, 6 * D_MODEL), fix2),
            pl.BlockSpec((D_MODEL, N_BRANCH * D_MODEL), fix2),
            pl.BlockSpec((1, N_BRANCH * D_MODEL), fix2),
            pl.BlockSpec((N_BRANCH, BRANCH_W, D_MODEL), lambda i: (0, 0, 0)),
            pl.BlockSpec((D_MODEL, D_MODEL), fix2),
            pl.BlockSpec((1, D_MODEL), fix2),
        ],
        out_specs=(pl.BlockSpec((TM, D_MODEL), tok), pl.BlockSpec((TM, D_MODEL), tok)),
        compiler_params=pltpu.CompilerParams(
            dimension_semantics=("arbitrary",), vmem_limit_bytes=VMEM_LIMIT),
        name="combine",
    )(hz, *ys, xs, mod, w_mgate, b_mgate.reshape(1, -1), w_branch, w_out, g_ffn.reshape(1, D_MODEL))


def _route_kernel(h_ref, w_ref, b_ref, topi_ref, gate_ref, rank_ref, cnt_ref, base_s):
    i = pl.program_id(0)

    @pl.when(i == 0)
    def _():
        base_s[...] = jnp.zeros(base_s.shape, F32)

    tm = h_ref.shape[0]
    logits = _split_dot_w(h_ref[...], w_ref[...]) + b_ref[...]
    lane = lax.broadcasted_iota(jnp.int32, (tm, 128), 1)
    l = logits
    idxs, vals, hots = [], [], []
    for _ in range(TOP_K):
        m = l.max(axis=-1, keepdims=True)
        idx = jnp.min(jnp.where(l == m, lane, 128), axis=-1, keepdims=True)
        hot = lane == idx
        idxs.append(idx)
        vals.append(m)
        hots.append(hot)
        l = jnp.where(hot, -3e38, l)
    es = [jnp.exp(v - vals[0]) for v in vals]
    den = es[0] + es[1] + es[2] + es[3]

    cnt = jnp.zeros((tm, 128), F32)
    for hot in hots:
        cnt = cnt + jnp.where(hot, 1.0, 0.0)
    r = lax.broadcasted_iota(jnp.int32, (tm, tm), 0)
    c = lax.broadcasted_iota(jnp.int32, (tm, tm), 1)
    before = jnp.where(r > c, 1.0, 0.0).astype(BF16)
    prior = _dot(before, cnt.astype(BF16)) + base_s[...]

    topi = jnp.zeros((tm, 128), jnp.int32)
    gate = jnp.zeros((tm, 128), F32)
    rank = jnp.zeros((tm, 128), jnp.int32)
    for k in range(TOP_K):
        rk = jnp.sum(jnp.where(hots[k], prior, 0.0), axis=-1, keepdims=True)
        topi = jnp.where(lane == k, idxs[k], topi)
        gate = jnp.where(lane == k, es[k] / den, gate)
        rank = jnp.where(lane == k, rk.astype(jnp.int32), rank)
    topi_ref[...] = topi
    gate_ref[...] = gate
    rank_ref[...] = rank
    base_s[...] = base_s[...] + jnp.sum(cnt, axis=0, keepdims=True)
    cnt_ref[...] = base_s[...]


def _route(hl, router_w, router_b, seq_len, latent_only):
    per_b, src = _token_tiles(seq_len, TM_ROUTE, latent_only)
    t = hl.shape[0] // (seq_len // TM_ROUTE) * per_b
    w = jnp.pad(router_w, ((0, 0), (0, 128 - N_EXPERTS)))
    b = jnp.pad(router_b, (0, 128 - N_EXPERTS), constant_values=NEG_BIG).reshape(1, 128)
    tok = lambda i: (i, 0)
    return pl.pallas_call(
        _route_kernel,
        out_shape=(jax.ShapeDtypeStruct((t, 128), jnp.int32), jax.ShapeDtypeStruct((t, 128), F32),
                   jax.ShapeDtypeStruct((t, 128), jnp.int32), jax.ShapeDtypeStruct((1, 128), F32)),
        grid=(t // TM_ROUTE,),
        in_specs=[pl.BlockSpec((TM_ROUTE, D_MODEL), lambda i: (src(i), 0)),
                  pl.BlockSpec((D_MODEL, 128), lambda i: (0, 0)),
                  pl.BlockSpec((1, 128), lambda i: (0, 0))],
        out_specs=(pl.BlockSpec((TM_ROUTE, 128), tok), pl.BlockSpec((TM_ROUTE, 128), tok),
                   pl.BlockSpec((TM_ROUTE, 128), tok), pl.BlockSpec((1, 128), lambda i: (0, 0))),
        scratch_shapes=[pltpu.VMEM((1, 128), F32)],
        compiler_params=pltpu.CompilerParams(dimension_semantics=("arbitrary",)),
        name="route",
    )(hl, w, b)


def _dispatch_kernel(last_ref, dest_ref, h_ref, rows_ref, tile_s, zero_s, sem):
    i = pl.program_id(0)
    tm = h_ref.shape[0]

    @pl.when(i == 0)
    def _():
        zero_s[...] = jnp.zeros(zero_s.shape, F32)

        def blk_copy(blk):
            r = pl.multiple_of(blk * MOE_BM, MOE_BM)
            return pltpu.make_async_copy(zero_s, rows_ref.at[pl.ds(r, MOE_BM)], sem)

        for e in range(N_EXPERTS):
            blk_copy(last_ref[e]).start()
        for e in range(N_EXPERTS):
            blk_copy(last_ref[e]).wait()

        n_blocks = rows_ref.shape[0] // MOE_BM

        def clear_tail(blk, carry):
            blk_copy(blk).start()
            blk_copy(blk).wait()
            return carry

        lax.fori_loop(last_ref[N_EXPERTS], n_blocks, clear_tail, 0)

    tile_s[...] = h_ref[...].reshape(tile_s.shape)

    def issue(t, carry):
        for k in range(TOP_K):
            d = dest_ref[t * TOP_K + k]
            pltpu.make_async_copy(tile_s.at[t], rows_ref.at[d], sem).start(priority=k % 2)
        return carry

    lax.fori_loop(0, tm, issue, 0)
    for k in range(TOP_K):
        pltpu.make_async_copy(tile_s, rows_ref.at[pl.ds(0, tm)], sem).wait()


ROW_TILE = (8, D_MODEL // 8)


def _dispatch(last_blk, dest_flat, hl, n_rows, seq_len, latent_only):
    _, src = _token_tiles(seq_len, TM_DMA, latent_only)
    t = dest_flat.shape[0] // TOP_K
    grid_spec = pltpu.PrefetchScalarGridSpec(
        num_scalar_prefetch=1,
        grid=(t // TM_DMA,),
        in_specs=[pl.BlockSpec((TM_DMA * TOP_K,), lambda i, lb: (i,), memory_space=pltpu.SMEM),
                  pl.BlockSpec((TM_DMA, D_MODEL), lambda i, lb: (src(i), 0))],
        out_specs=pl.BlockSpec(memory_space=pl.ANY),
        scratch_shapes=[pltpu.VMEM((TM_DMA,) + ROW_TILE, F32), pltpu.VMEM((MOE_BM,) + ROW_TILE, F32),
                        pltpu.SemaphoreType.DMA(())],
    )
    return pl.pallas_call(
        _dispatch_kernel,
        out_shape=jax.ShapeDtypeStruct((n_rows,) + ROW_TILE, F32),
        grid_spec=grid_spec,
        compiler_params=pltpu.CompilerParams(dimension_semantics=("arbitrary",), has_side_effects=True),
        name="moe_dispatch",
    )(last_blk, dest_flat, hl)


def _expert_kernel(be_ref, nu_ref, x_ref, wgu_ref, bgu_ref, wd_ref, bd_ref, o_ref, wgu_s, wd_s):
    j = pl.program_id(0)
    d_ff = wd_ref.shape[0]
    used = j < nu_ref[0]
    new_expert = jnp.logical_or(j == 0, be_ref[j] != be_ref[jnp.maximum(j - 1, 0)])

    @pl.when(jnp.logical_and(used, new_expert))
    def _():
        rb = 256
        for r in range(0, D_MODEL, rb):
            wgu_s[r:r + rb, :] = wgu_ref[r:r + rb, :].astype(BF16)
        for r in range(0, d_ff, rb):
            wd_s[r:r + rb, :] = wd_ref[r:r + rb, :].astype(BF16)

    @pl.when(used)
    def _():
        x = x_ref[...].reshape(x_ref.shape[0], D_MODEL)
        gu = _dot(x.astype(BF16), wgu_s[...]) + bgu_ref[...]
        gate = jnp.minimum(gu[:, :d_ff], SWIGLU_LIMIT)
        up = jnp.clip(gu[:, d_ff:], -SWIGLU_LIMIT, SWIGLU_LIMIT)
        y = (up + 1.0) * (gate * _sigmoid(SWIGLU_ALPHA * gate))
        o_ref[...] = (_dot(y.astype(BF16), wd_s[...]) + bd_ref[...]).reshape(o_ref.shape)

    @pl.when(jnp.logical_not(used))
    def _():
        o_ref[...] = jnp.zeros(o_ref.shape, F32)


def _experts(block_e, n_used, rows, layer, w_gu, b_gu, w_down, b_down):
    n_rows = rows.shape[0]
    d_ff = w_down.shape[2]
    depth = w_gu.shape[0]
    row_blk = lambda j, be, nu: (jnp.minimum(j, nu[0] - 1), 0, 0)
    grid_spec = pltpu.PrefetchScalarGridSpec(
        num_scalar_prefetch=2,
        grid=(n_rows // MOE_BM,),
        in_specs=[
            pl.BlockSpec((MOE_BM,) + ROW_TILE, row_blk),
            pl.BlockSpec((None, None, D_MODEL, 2 * d_ff), lambda j, be, nu: (layer, be[j], 0, 0)),
            pl.BlockSpec((None, None, 1, 2 * d_ff), lambda j, be, nu: (layer, be[j], 0, 0)),
            pl.BlockSpec((None, None, d_ff, D_MODEL), lambda j, be, nu: (layer, be[j], 0, 0)),
            pl.BlockSpec((None, None, 1, D_MODEL), lambda j, be, nu: (layer, be[j], 0, 0)),
        ],
        out_specs=pl.BlockSpec((MOE_BM,) + ROW_TILE, lambda j, be, nu: (j, 0, 0)),
        scratch_shapes=[pltpu.VMEM((D_MODEL, 2 * d_ff), BF16), pltpu.VMEM((d_ff, D_MODEL), BF16)],
    )
    return pl.pallas_call(
        _expert_kernel,
        out_shape=jax.ShapeDtypeStruct((n_rows,) + ROW_TILE, F32),
        grid_spec=grid_spec,
        compiler_params=pltpu.CompilerParams(
            dimension_semantics=("arbitrary",), vmem_limit_bytes=VMEM_LIMIT),
        name="moe_experts",
    )(block_e, n_used, rows, w_gu, b_gu.reshape(depth, N_EXPERTS, 1, -1), w_down,
      b_down.reshape(depth, N_EXPERTS, 1, -1))


def _finish_kernel(dest_ref, gate_ref, xs_ref, mod_ref, rows_ref, o_ref, buf, sem, *, tiles_per_b, has_ctx):
    i = pl.program_id(0)
    tm = xs_ref.shape[0]

    half = tm // 2

    def issue_half(p):
        def issue(t, carry):
            for k in range(TOP_K):
                d = dest_ref[t * TOP_K + k]
                pltpu.make_async_copy(rows_ref.at[d], buf.at[k, t], sem.at[p]).start(priority=k % 2)
            return carry
        lax.fori_loop(p * half, (p + 1) * half, issue, 0)

    issue_half(0)
    issue_half(1)
    mod = _mod_rows(mod_ref, i, tiles_per_b, tm, 5, has_ctx)
    for p in range(2):
        rs = slice(p * half, (p + 1) * half)
        for k in range(TOP_K):
            pltpu.make_async_copy(rows_ref.at[pl.ds(0, half)], buf.at[k, rs], sem.at[p]).wait()
        gate = gate_ref[rs, :]
        y = gate[:, 0:1] * buf[0, rs].reshape(half, D_MODEL)
        for k in range(1, TOP_K):
            y = y + gate[:, k:k + 1] * buf[k, rs].reshape(half, D_MODEL)
        o_ref[rs, :] = xs_ref[rs, :] + (mod if mod.shape[0] == 1 else mod[rs]) * y


def _finish(dest_flat, gates, xs, mod, out_rows, seq_len, latent_only):
    tiles_per_b, src = _token_tiles(seq_len, TM_DMA, latent_only)
    t = dest_flat.shape[0] // TOP_K
    return pl.pallas_call(
        functools.partial(_finish_kernel, tiles_per_b=tiles_per_b, has_ctx=not latent_only),
        out_shape=jax.ShapeDtypeStruct((t, D_MODEL), F32),
        grid=(t // TM_DMA,),
        in_specs=[pl.BlockSpec((TM_DMA * TOP_K,), lambda i: (i,), memory_space=pltpu.SMEM),
                  pl.BlockSpec((TM_DMA, 128), lambda i: (i, 0)),
                  pl.BlockSpec((TM_DMA, D_MODEL), lambda i: (src(i), 0)),
                  pl.BlockSpec((16, 6 * D_MODEL), lambda i: (0, 0)),
                  pl.BlockSpec(memory_space=pl.ANY)],
        out_specs=pl.BlockSpec((TM_DMA, D_MODEL), lambda i: (i, 0)),
        scratch_shapes=[pltpu.VMEM((TOP_K, TM_DMA) + ROW_TILE, F32), pltpu.SemaphoreType.DMA((2,))],
        compiler_params=pltpu.CompilerParams(
            dimension_semantics=("arbitrary",), vmem_limit_bytes=VMEM_LIMIT),
        name="moe_finish",
    )(dest_flat, gates, xs, mod, out_rows)


def _moe_plan(topi, rank, counts):
    t = topi.shape[0]
    padded = (counts + MOE_BM - 1) // MOE_BM * MOE_BM
    pad_end = jnp.cumsum(padded)
    pad_start = pad_end - padded
    onehot = topi[:, :, None] == jnp.arange(N_EXPERTS, dtype=jnp.int32)[None, None, :]
    dest = jnp.sum(jnp.where(onehot, pad_start[None, None, :], 0), axis=-1) + rank
    n_blocks = -(-(t * TOP_K + N_EXPERTS * (MOE_BM - 1)) // MOE_BM)
    starts = jnp.arange(n_blocks, dtype=jnp.int32) * MOE_BM
    block_e = jnp.sum(starts[:, None] >= pad_end[None, :], axis=-1).astype(jnp.int32)
    block_e = jnp.minimum(block_e, N_EXPERTS - 1)
    n_used = (pad_end[-1] // MOE_BM).astype(jnp.int32).reshape(1)
    last_blk = jnp.maximum(pad_end // MOE_BM - 1, 0).astype(jnp.int32)
    last_blk = jnp.concatenate([last_blk, n_used])
    return dest.reshape(-1).astype(jnp.int32), block_e, n_used, last_blk, n_blocks * MOE_BM


def kernel(x, c, ctx, c_ctx, w_ada, b_ada, norm_mix, norm_ffn, w_in, na_qk_gain, na_rpb, ret_decay, ret_gn,
           diff_qk_gain, diff_lam, diff_subln, dn_conv, dn_a_log, dn_dt_bias, dn_norm, w_branch, w_mgate,
           b_mgate, w_out, router_w, router_b, w_gu, b_gu, w_down, b_down):
    b, s, d = x.shape
    n_ctx = ctx.shape[1]
    seq = n_ctx + s
    depth = w_ada.shape[0]
    assert (d, n_ctx, s) == (D_MODEL, CTX_LEN, 2048) and b <= 8 and seq % TM == 0

    xs = jnp.concatenate([ctx, x], axis=1).reshape(b * seq, d)
    cc = jnp.zeros((16, d), F32).at[0:b].set(c).at[8].set(c_ctx)
    mods = _ada(cc, w_ada, b_ada)

    for l in range(depth):
        lam_init = 0.8 - 0.6 * math.exp(-0.3 * l)
        w_main = w_in[l, :, :D_IN_MAIN].astype(BF16)
        w_ab = jnp.pad(w_in[l, :, D_IN_MAIN:], ((0, 0), (0, 128 - (D_IN - D_IN_MAIN)))).astype(BF16)
        p, pab, hz = _in_proj(xs, mods[l], norm_mix[l], w_main, w_ab, seq)
        p3 = p.reshape(b, seq, D_IN_MAIN)
        ys = (_na_branch(p3, na_qk_gain[l], na_rpb[l]),
              _ret_branch(p3, ret_decay[l], ret_gn[l]),
              _diff_branch(p3, diff_qk_gain[l], diff_lam[l], diff_subln[l], lam_init),
              _dn_branch(p3, pab.reshape(b, seq, 128), dn_conv[l], dn_a_log[l], dn_dt_bias[l], dn_norm[l]))
        ys = tuple(y.reshape(b * seq, BRANCH_W) for y in ys)
        xs, hl = _combine(hz, ys, xs, mods[l], w_mgate[l].astype(BF16), b_mgate[l], w_branch[l].astype(BF16),
                          w_out[l].astype(BF16), norm_ffn[l], seq)
        last = l == depth - 1
        topi, gates, rank, cnt = _route(hl, router_w[l], router_b[l], seq, last)
        counts = cnt[0, :N_EXPERTS].astype(jnp.int32)
        dest, block_e, n_used, last_blk, n_rows = _moe_plan(topi[:, :TOP_K], rank[:, :TOP_K], counts)
        rows = _dispatch(last_blk, dest, hl, n_rows, seq, last)
        out_rows = _experts(block_e, n_used, rows, l, w_gu, b_gu, w_down, b_down)
        xs = _finish(dest, gates, xs, mods[l], out_rows, seq, last)

    return xs.reshape(b, s, d)
```

```python
import functools
import math

import numpy as np
import jax
import jax.numpy as jnp
from jax import lax
from jax.experimental import pallas as pl
from jax.experimental.pallas import tpu as pltpu

F32 = jnp.float32
BF16 = jnp.bfloat16

D_MODEL = 1024
GRID_W = 64
CTX_LEN = 256
HEAD_DIM = 64
N_BRANCH = 4
BRANCH_W = 512
NA_HEADS = 8
NA_WIN_R = 8
NA_WIN_C = 16
RET_HEADS = 4
RET_CHUNK = 128
DIFF_HEADS = 4
DN_HEADS = 4
DN_CONV = 5
DN_CHUNK = 64
N_EXPERTS = 32
TOP_K = 4
SWIGLU_LIMIT = 7.0
SWIGLU_ALPHA = 1.702
ROPE_BASE = 10000.0
EPS = 1e-6
D_IN = 6160
D_IN_MAIN = 6144
NEG_BIG = -1e30

VMEM_LIMIT = 56 * 1024 * 1024

TM = 768
TM_IN = 1152
TN_IN = 3072
TM_ROUTE = 256
TM_DMA = 256
MOE_BM = 256


def _dot(a, b):
    return jnp.dot(a, b, preferred_element_type=F32)


def _dot_nt(a, b):
    return lax.dot_general(a, b, (((1,), (1,)), ((), ())), preferred_element_type=F32)


def _dot_tn(a, b):
    return lax.dot_general(a, b, (((0,), (0,)), ((), ())), preferred_element_type=F32)


def _split_dot(x, w_bf16):
    hi = x.astype(BF16)
    lo = (x - hi.astype(F32)).astype(BF16)
    return _dot(hi, w_bf16) + _dot(lo, w_bf16)


def _group_ones(n, group):
    r = lax.broadcasted_iota(jnp.int32, (n, n), 0) // group
    c = lax.broadcasted_iota(jnp.int32, (n, n), 1) // group
    return jnp.where(r == c, 1.0, 0.0).astype(BF16)


def _sigmoid(x):
    return 1.0 / (1.0 + jnp.exp(-x))


def _silu(x):
    return x * _sigmoid(x)


def _ada_kernel(c_ref, w_ref, b_ref, o_ref):
    c = c_ref[...]
    a = _silu(c)
    o_ref[...] = _split_dot_w(a, w_ref[...]) + b_ref[...]


def _split_dot_w(a, w):
    ah = a.astype(BF16)
    al = (a - ah.astype(F32)).astype(BF16)
    wh = w.astype(BF16)
    wl = (w - wh.astype(F32)).astype(BF16)
    return _dot(ah, wh) + _dot(al, wh) + _dot(ah, wl)


def _ada(cc, w_ada, b_ada):
    depth = w_ada.shape[0]
    n = w_ada.shape[2]
    tn = 768
    return pl.pallas_call(
        _ada_kernel,
        out_shape=jax.ShapeDtypeStruct((depth, 16, n), F32),
        grid=(depth, n // tn),
        in_specs=[
            pl.BlockSpec((16, D_MODEL), lambda l, j: (0, 0)),
            pl.BlockSpec((None, D_MODEL, tn), lambda l, j: (l, 0, j)),
            pl.BlockSpec((None, 1, tn), lambda l, j: (l, 0, j)),
        ],
        out_specs=pl.BlockSpec((None, 16, tn), lambda l, j: (l, 0, j)),
        compiler_params=pltpu.CompilerParams(dimension_semantics=("arbitrary", "arbitrary")),
        name="ada_mod",
    )(cc, w_ada, b_ada.reshape(depth, 1, n))


def _mod_rows(mod_ref, tile, tiles_per_b, rows, which, has_ctx=True):
    b = tile // tiles_per_b
    lo = which * D_MODEL
    ml = mod_ref[pl.ds(b, 1), lo:lo + D_MODEL]
    if not has_ctx:
        return ml
    mc = mod_ref[8:9, lo:lo + D_MODEL]
    rid = lax.broadcasted_iota(jnp.int32, (rows, 1), 0)
    is_ctx = jnp.logical_and(tile % tiles_per_b == 0, rid < CTX_LEN)
    return jnp.where(is_ctx, mc, ml)


def _token_tiles(seq_len, tile, latent_only):
    per_b = seq_len // tile
    if not latent_only:
        return per_b, (lambda i: i)
    skip = CTX_LEN // tile
    assert skip * tile == CTX_LEN
    return per_b - skip, (lambda i: (i // (per_b - skip)) * per_b + skip + i % (per_b - skip))


def _rms_rows(x, g):
    ms = jnp.mean(x * x, axis=-1, keepdims=True)
    return x * lax.rsqrt(ms + EPS) * g


def _in_proj_kernel(x_ref, mod_ref, g_ref, w_ref, wab_ref, p_ref, pab_ref, hz_ref, hz_s, *, tiles_per_b):
    i = pl.program_id(0)
    j = pl.program_id(1)

    @pl.when(j == 0)
    def _():
        tm = x_ref.shape[0]
        y = _rms_rows(x_ref[...], g_ref[...])
        shift = _mod_rows(mod_ref, i, tiles_per_b, tm, 0)
        scale = _mod_rows(mod_ref, i, tiles_per_b, tm, 1)
        hz = (y * (1.0 + scale) + shift).astype(BF16)
        hz_s[...] = hz
        hz_ref[...] = hz
        pab_ref[...] = _dot(hz, wab_ref[...])

    p_ref[...] = _dot(hz_s[...], w_ref[...]).astype(p_ref.dtype)


def _in_proj(xs, mod, g, w_main, w_ab, seq_len):
    t = xs.shape[0]
    n = w_main.shape[1]
    tm = TM_IN
    tiles_per_b = seq_len // tm
    return pl.pallas_call(
        functools.partial(_in_proj_kernel, tiles_per_b=tiles_per_b),
        out_shape=(jax.ShapeDtypeStruct((t, n), BF16),
                   jax.ShapeDtypeStruct((t, 128), F32),
                   jax.ShapeDtypeStruct((t, D_MODEL), BF16)),
        grid=(t // tm, n // TN_IN),
        in_specs=[
            pl.BlockSpec((tm, D_MODEL), lambda i, j: (i, 0)),
            pl.BlockSpec((16, 6 * D_MODEL), lambda i, j: (0, 0)),
            pl.BlockSpec((1, D_MODEL), lambda i, j: (0, 0)),
            pl.BlockSpec((D_MODEL, TN_IN), lambda i, j: (0, j)),
            pl.BlockSpec((D_MODEL, 128), lambda i, j: (0, 0)),
        ],
        out_specs=(pl.BlockSpec((tm, TN_IN), lambda i, j: (i, j)),
                   pl.BlockSpec((tm, 128), lambda i, j: (i, 0)),
                   pl.BlockSpec((tm, D_MODEL), lambda i, j: (i, 0))),
        scratch_shapes=[pltpu.VMEM((tm, D_MODEL), BF16)],
        compiler_params=pltpu.CompilerParams(
            dimension_semantics=("arbitrary", "arbitrary"), vmem_limit_bytes=VMEM_LIMIT),
        name="in_proj",
    )(xs, mod, g.reshape(1, D_MODEL), w_main, w_ab)


def _group_rms(x, g, ones_bd, group):
    ss = _split_dot(x * x, ones_bd)
    return x * lax.rsqrt(ss * (1.0 / group) + EPS) * g


def _softmax_parts(parts):
    m = parts[0].max(axis=-1, keepdims=True)
    for s in parts[1:]:
        m = jnp.maximum(m, s.max(axis=-1, keepdims=True))
    es = [jnp.exp(s - m) for s in parts]
    l = es[0].sum(axis=-1, keepdims=True)
    for e in es[1:]:
        l = l + e.sum(axis=-1, keepdims=True)
    return es, l


NA_GROUP = 4
NA_WROWS_MID = NA_WIN_R + NA_GROUP - 1
NA_KEYS_EDGE = CTX_LEN + NA_WIN_R * GRID_W
NA_KEYS_MID = CTX_LEN + NA_WROWS_MID * GRID_W


def _na_bias_tables(w_ref, top_s, mid_s, bot_s):
    c_i = lax.broadcasted_iota(jnp.int32, (GRID_W, 128), 0)
    l_i = lax.broadcasted_iota(jnp.int32, (GRID_W, 128), 1)
    c0 = jnp.clip(c_i - NA_WIN_C // 2, 0, GRID_W - NA_WIN_C)
    in_lo = jnp.logical_and(l_i >= c0, l_i < c0 + NA_WIN_C)
    in_hi = jnp.logical_and(l_i - GRID_W >= c0, l_i - GRID_W < c0 + NA_WIN_C)
    neg = jnp.full((GRID_W, 128), NEG_BIG, F32)
    plans = (
        (top_s, NA_WIN_R, lambda u, j: j - u + NA_WIN_R - 1),
        (mid_s, NA_WROWS_MID, lambda u, j: j - u + NA_WIN_R // 2 - 1 if 0 <= j - u < NA_WIN_R else None),
        (bot_s, NA_WIN_R, lambda u, j: j - u + NA_WIN_R // 2 - 1),
    )
    for h in range(2):
        lo, hi = [], []
        for dr in range(2 * NA_WIN_R - 1):
            row = jnp.broadcast_to(w_ref[h, dr:dr + 1, :], (GRID_W, 128))
            lo.append(jnp.where(in_lo, pltpu.roll(row, 128 - (GRID_W - 1), 1, stride=1, stride_axis=0), neg))
            hi.append(jnp.where(in_hi, pltpu.roll(row, 1, 1, stride=1, stride_axis=0), neg))
        for tab, wrows, dr_of in plans:
            tab[h, :, 0:CTX_LEN] = jnp.zeros((NA_GROUP * GRID_W, CTX_LEN), F32)
            for u in range(NA_GROUP):
                rs = slice(u * GRID_W, (u + 1) * GRID_W)
                for j in range(0, wrows, 2):
                    d_lo = dr_of(u, j)
                    d_hi = dr_of(u, j + 1) if j + 1 < wrows else None
                    t_lo = neg if d_lo is None else lo[d_lo]
                    t_hi = neg if d_hi is None else hi[d_hi]
                    tile = jnp.where(l_i < GRID_W, t_lo, t_hi)
                    col = CTX_LEN + j * GRID_W
                    if j + 1 < wrows:
                        tab[h, rs, col:col + 128] = tile
                    else:
                        tab[h, rs, col:col + GRID_W] = tile[:, 0:GRID_W]


def _na_kernel(q_ref, k_ref, v_ref, gain_ref, w_ref, o_ref, qn_s, kn_s, top_s, mid_s, bot_s, kcat_s, vcat_s):
    seq = q_ref.shape[0]
    rows = (seq - CTX_LEN) // GRID_W

    @pl.when(pl.program_id(1) == 0)
    def _():
        _na_bias_tables(w_ref, top_s, mid_s, bot_s)

    ones_bd = _group_ones(128, HEAD_DIM)
    gq = gain_ref[0:1, :]
    gk = gain_ref[1:2, :]
    scale = HEAD_DIM ** -0.5
    blk = 256
    for c in range(seq // blk):
        sl = slice(c * blk, (c + 1) * blk)
        qn_s[sl, :] = (_group_rms(q_ref[sl, :].astype(F32), gq, ones_bd, HEAD_DIM) * scale).astype(BF16)
        kn_s[sl, :] = _group_rms(k_ref[sl, :].astype(F32), gk, ones_bd, HEAD_DIM).astype(BF16)

    lane = lax.broadcasted_iota(jnp.int32, (1, 128), 1)
    head_mask = [lane < HEAD_DIM, lane >= HEAD_DIM]
    kc = kn_s[0:CTX_LEN, :]
    vc = v_ref[0:CTX_LEN, :]

    qc = qn_s[0:CTX_LEN, :]
    outs = []
    for h in range(2):
        qm = jnp.where(head_mask[h], qc, jnp.zeros_like(qc))
        (e,), l = _softmax_parts([_dot_nt(qm, kc)])
        outs.append(_dot(e.astype(BF16), vc) / l)
    o_ref[0:CTX_LEN, :] = jnp.where(head_mask[0], outs[0], outs[1]).astype(o_ref.dtype)

    assert NA_GROUP == NA_WIN_R // 2 and rows % NA_GROUP == 0
    kcat_s[0:CTX_LEN, :] = kc
    vcat_s[0:CTX_LEN, :] = vc
    gq_rows = NA_GROUP * GRID_W
    for g in range(rows // NA_GROUP):
        r = g * NA_GROUP
        if g == 0:
            tab, w0, nk = top_s, 0, NA_KEYS_EDGE
        elif g == rows // NA_GROUP - 1:
            tab, w0, nk = bot_s, rows - NA_WIN_R, NA_KEYS_EDGE
        else:
            tab, w0, nk = mid_s, r - NA_WIN_R // 2, NA_KEYS_MID
        k0 = CTX_LEN + w0 * GRID_W
        kcat_s[CTX_LEN:nk, :] = kn_s[k0:k0 + nk - CTX_LEN, :]
        vcat_s[CTX_LEN:nk, :] = v_ref[k0:k0 + nk - CTX_LEN, :]
        q0 = CTX_LEN + r * GRID_W
        qg = qn_s[q0:q0 + gq_rows, :]
        res = []
        for h in range(2):
            qm = jnp.where(head_mask[h], qg, jnp.zeros_like(qg))
            (e,), l = _softmax_parts([_dot_nt(qm, kcat_s[0:nk, :]) + tab[h, :, 0:nk]])
            res.append(_dot(e.astype(BF16), vcat_s[0:nk, :]) / l)
        o_ref[q0:q0 + gq_rows, :] = jnp.where(head_mask[0], res[0], res[1]).astype(o_ref.dtype)


def _na_branch(p3, qk_gain, rpb):
    b, seq, _ = p3.shape
    gain2 = jnp.tile(qk_gain, (1, 2))
    lo = GRID_W - NA_WIN_C
    w = jnp.pad(rpb, ((0, 0), (0, 0), (lo, 128 - lo - rpb.shape[2])))
    nb = BRANCH_W // 128
    return pl.pallas_call(
        _na_kernel,
        out_shape=jax.ShapeDtypeStruct((b, seq, BRANCH_W), BF16),
        grid=(nb, b),
        in_specs=[
            pl.BlockSpec((None, seq, 128), lambda hp, i: (i, 0, hp)),
            pl.BlockSpec((None, seq, 128), lambda hp, i: (i, 0, nb + hp)),
            pl.BlockSpec((None, seq, 128), lambda hp, i: (i, 0, 2 * nb + hp)),
            pl.BlockSpec((2, 128), lambda hp, i: (0, 0)),
            pl.BlockSpec((2, 2 * NA_WIN_R - 1, 128), lambda hp, i: (hp, 0, 0)),
        ],
        out_specs=pl.BlockSpec((None, seq, 128), lambda hp, i: (i, 0, hp)),
        scratch_shapes=[pltpu.VMEM((seq, 128), BF16), pltpu.VMEM((seq, 128), BF16),
                        pltpu.VMEM((2, NA_GROUP * GRID_W, NA_KEYS_EDGE), F32),
                        pltpu.VMEM((2, NA_GROUP * GRID_W, NA_KEYS_MID), F32),
                        pltpu.VMEM((2, NA_GROUP * GRID_W, NA_KEYS_EDGE), F32),
                        pltpu.VMEM((NA_KEYS_MID, 128), BF16), pltpu.VMEM((NA_KEYS_MID, 128), BF16)],
        compiler_params=pltpu.CompilerParams(
            dimension_semantics=("arbitrary", "arbitrary"), vmem_limit_bytes=VMEM_LIMIT),
        name="na_branch",
    )(p3, p3, p3, gain2, w)


DIFF_TQ = 256


def _rope_lanes(x, cos, sin_signed, half, first_mask):
    n = x.shape[-1]
    fwd = pltpu.roll(x, n - half, 1)
    bwd = pltpu.roll(x, half, 1)
    return x * cos + jnp.where(first_mask, fwd, bwd) * sin_signed


def _diff_kernel(q_ref, k_ref, v_ref, gain_ref, cos_ref, sin_ref, lam_ref, sub_ref, o_ref,
                 qn_s, kn_s, *, lam_init):
    seq = q_ref.shape[0]
    ones_bd = _group_ones(128, HEAD_DIM)
    gq = gain_ref[0:1, :]
    gk = gain_ref[1:2, :]
    scale = HEAD_DIM ** -0.5
    lane = lax.broadcasted_iota(jnp.int32, (1, 128), 1)
    first = (lane % 32) < 16
    blk = 256
    for c in range(seq // blk):
        sl = slice(c * blk, (c + 1) * blk)
        qn = _group_rms(q_ref[sl, :].astype(F32), gq, ones_bd, HEAD_DIM)
        kn = _group_rms(k_ref[sl, :].astype(F32), gk, ones_bd, HEAD_DIM)
        if c * blk >= CTX_LEN:
            ps = slice(c * blk - CTX_LEN, (c + 1) * blk - CTX_LEN)
            qn = _rope_lanes(qn, cos_ref[ps, :], sin_ref[ps, :], 16, first)
            kn = _rope_lanes(kn, cos_ref[ps, :], sin_ref[ps, :], 16, first)
        qn_s[sl, :] = (qn * scale).astype(BF16)
        kn_s[sl, :] = kn.astype(BF16)

    lp = lam_ref[...]
    lam = (jnp.exp(jnp.sum(lp[0:1, :] * lp[1:2, :], axis=-1, keepdims=True))
           - jnp.exp(jnp.sum(lp[2:3, :] * lp[3:4, :], axis=-1, keepdims=True)) + lam_init)
    comp_mask = [lane < HEAD_DIM, lane >= HEAD_DIM]
    sub = sub_ref[...]

    def attend(q, keys, vals):
        ps = []
        for c in range(2):
            qm = jnp.where(comp_mask[c], q, jnp.zeros_like(q))
            (e,), l = _softmax_parts([_dot_nt(qm, keys)])
            ps.append((e, l))
        a = ps[0][0] * (1.0 / ps[0][1]) - ps[1][0] * (lam / ps[1][1])
        o = _dot(a.astype(BF16), vals)
        return _rms_rows(o, sub) * (1.0 - lam_init)

    o_ref[0:CTX_LEN, :] = attend(qn_s[0:CTX_LEN, :], kn_s[0:CTX_LEN, :], v_ref[0:CTX_LEN, :]).astype(o_ref.dtype)

    def q_step(i, carry):
        q0 = pl.multiple_of(CTX_LEN + i * DIFF_TQ, DIFF_TQ)
        o_ref[pl.ds(q0, DIFF_TQ), :] = attend(qn_s[pl.ds(q0, DIFF_TQ), :], kn_s[...], v_ref[...]).astype(o_ref.dtype)
        return carry

    lax.fori_loop(0, (seq - CTX_LEN) // DIFF_TQ, q_step, 0)


def _rope_angles(pos, dim):
    inv = ROPE_BASE ** (-np.arange(0, dim, 2, dtype=np.float32) / dim)
    return pos.astype(np.float32)[:, None] * inv[None, :]


def _cos_sin_tables(ang, sign):
    a = ang.astype(np.float64)
    return (jnp.asarray(np.cos(a).astype(np.float32)),
            jnp.asarray((np.sin(a) * sign[None, :]).astype(np.float32)))


def _diff_rope_tables(s):
    t = np.arange(s)
    ang_r = _rope_angles(t // GRID_W, HEAD_DIM // 2)
    ang_c = _rope_angles(t % GRID_W, HEAD_DIM // 2)
    ang64 = np.concatenate([ang_r, ang_r, ang_c, ang_c], axis=1)
    sign64 = np.concatenate([-np.ones(16), np.ones(16), -np.ones(16), np.ones(16)]).astype(np.float32)
    return _cos_sin_tables(np.tile(ang64, (1, 2)), np.tile(sign64, 2))


def _diff_branch(p3, qk_gain, lam_p, subln, lam_init):
    b, seq, _ = p3.shape
    s = seq - CTX_LEN
    cos, sin = _diff_rope_tables(s)
    gain2 = jnp.tile(qk_gain, (1, 2))
    base = 3072 // 128
    nh = DIFF_HEADS
    return pl.pallas_call(
        functools.partial(_diff_kernel, lam_init=lam_init),
        out_shape=jax.ShapeDtypeStruct((b, seq, BRANCH_W), BF16),
        grid=(b, nh),
        in_specs=[
            pl.BlockSpec((None, seq, 128), lambda i, h: (i, 0, base + h)),
            pl.BlockSpec((None, seq, 128), lambda i, h: (i, 0, base + nh + h)),
            pl.BlockSpec((None, seq, 128), lambda i, h: (i, 0, base + 2 * nh + h)),
            pl.BlockSpec((2, 128), lambda i, h: (0, 0)),
            pl.BlockSpec((s, 128), lambda i, h: (0, 0)),
            pl.BlockSpec((s, 128), lambda i, h: (0, 0)),
            pl.BlockSpec((4, HEAD_DIM), lambda i, h: (0, 0)),
            pl.BlockSpec((1, 128), lambda i, h: (0, 0)),
        ],
        out_specs=pl.BlockSpec((None, seq, 128), lambda i, h: (i, 0, h)),
        scratch_shapes=[pltpu.VMEM((seq, 128), BF16), pltpu.VMEM((seq, 128), BF16)],
        compiler_params=pltpu.CompilerParams(
            dimension_semantics=("arbitrary", "arbitrary"), vmem_limit_bytes=VMEM_LIMIT),
        name="diff_branch",
    )(p3, p3, p3, gain2, cos, sin, lam_p, subln.reshape(1, 128))


def _log_sigmoid(x):
    return jnp.minimum(x, 0.0) - jnp.log(1.0 + jnp.exp(-jnp.abs(x)))


def _ret_kernel(q_ref, k_ref, v_ref, g_ref, cos_ref, sin_ref, dec_ref, gn_ref, o_ref,
                q_s, k_s, o_s):
    seq = q_ref.shape[0]
    c = RET_CHUNK
    n_ctx_chunks = CTX_LEN // c
    n_chunks = seq // c
    hw = RET_HEADS * HEAD_DIM
    vw = RET_HEADS * 2 * HEAD_DIM
    lane = lax.broadcasted_iota(jnp.int32, (1, hw), 1)
    first = (lane % HEAD_DIM) < (HEAD_DIM // 2)
    blk = 256
    kscale = HEAD_DIM ** -0.5
    for cb in range(seq // blk):
        sl = slice(cb * blk, (cb + 1) * blk)
        q = q_ref[sl, :].astype(F32)
        k = k_ref[sl, :].astype(F32)
        if cb * blk >= CTX_LEN:
            ps = slice(cb * blk - CTX_LEN, (cb + 1) * blk - CTX_LEN)
            q = _rope_lanes(q, cos_ref[ps, :], sin_ref[ps, :], HEAD_DIM // 2, first)
            k = _rope_lanes(k, cos_ref[ps, :], sin_ref[ps, :], HEAD_DIM // 2, first)
        q_s[sl, :] = q
        k_s[sl, :] = k * kscale

    log_g = _log_sigmoid(dec_ref[...])
    pos_r = lax.broadcasted_iota(jnp.int32, (c, c), 0).astype(F32)
    pos_c = lax.broadcasted_iota(jnp.int32, (c, c), 1).astype(F32)
    pos = lax.broadcasted_iota(jnp.int32, (c, 1), 0).astype(F32)
    head_of_q = lax.broadcasted_iota(jnp.int32, (1, hw), 1) // HEAD_DIM
    head_of_v = lax.broadcasted_iota(jnp.int32, (1, vw), 1) // (2 * HEAD_DIM)
    row_head = lax.broadcasted_iota(jnp.int32, (hw, vw), 0) // HEAD_DIM
    col_head = lax.broadcasted_iota(jnp.int32, (hw, vw), 1) // (2 * HEAD_DIM)
    state_mask = row_head == col_head

    intra, q_dec, k_dec, c_dec = [], [], [], []
    for d in range(2):
        rel = (pos_r - pos_c) if d == 0 else (pos_c - pos_r)
        lg_q = jnp.zeros((1, hw), F32)
        lg_v = jnp.zeros((1, vw), F32)
        per_head = []
        for h in range(RET_HEADS):
            lg = log_g[d:d + 1, h:h + 1]
            per_head.append(jnp.where(rel >= 0, jnp.exp(lg * jnp.maximum(rel, 0.0)), 0.0))
            lg_q = jnp.where(head_of_q == h, lg, lg_q)
            lg_v = jnp.where(head_of_v == h, lg, lg_v)
        intra.append(per_head)
        qpos = (pos + 1.0) if d == 0 else (c - pos)
        kpos = (c - 1.0 - pos) if d == 0 else pos
        q_dec.append(jnp.exp(lg_q * qpos))
        k_dec.append(jnp.exp(lg_q * kpos))
        c_dec.append(jnp.exp(lg_v * c))

    head_mask = [head_of_q == h for h in range(RET_HEADS)]

    def chunk_start(d, n):
        if d == 0:
            return n * c
        rev_ctx = (n_ctx_chunks - 1 - n) * c
        rev_lat = CTX_LEN + (n_chunks - 1 - n) * c
        return jnp.where(n < n_ctx_chunks, rev_ctx, rev_lat)

    def step(n, states):
        new_states = []
        for d in range(2):
            r0 = pl.multiple_of(chunk_start(d, n), c)
            s = states[d]
            qi = q_s[pl.ds(r0, c), :]
            ki = k_s[pl.ds(r0, c), :]
            vi = v_ref[pl.ds(r0, c), :].astype(BF16)
            kb = ki.astype(BF16)
            cross = _dot((qi * q_dec[d]).astype(BF16), s.astype(BF16))
            parts = []
            for h in range(RET_HEADS):
                qm = jnp.where(head_mask[h], qi, 0.0).astype(BF16)
                att = _dot_nt(qm, kb) * intra[d][h]
                parts.append(_dot(att.astype(BF16), vi[:, h * 128:(h + 1) * 128]))
            o = jnp.concatenate(parts, axis=1) + cross
            o_s[pl.ds(r0, c), :] = o_s[pl.ds(r0, c), :] + o
            upd = _dot_tn((ki * k_dec[d]).astype(BF16), vi)
            new_states.append(s * c_dec[d] + jnp.where(state_mask, upd, 0.0))
        return tuple(new_states)

    o_s[...] = jnp.zeros(o_s.shape, F32)
    zero = jnp.zeros((hw, vw), F32)
    lax.fori_loop(0, n_chunks, step, (zero, zero))

    gn = gn_ref[...]
    for cb in range(seq // blk):
        sl = slice(cb * blk, (cb + 1) * blk)
        outs = []
        for h in range(RET_HEADS):
            hs = slice(h * 128, (h + 1) * 128)
            y = o_s[sl, hs]
            yc = y - jnp.mean(y, axis=-1, keepdims=True)
            yn = yc * lax.rsqrt(jnp.mean(yc * yc, axis=-1, keepdims=True) + EPS) * gn
            outs.append(yn * _silu(g_ref[sl, hs].astype(F32)))
        o_ref[sl, :] = jnp.concatenate(outs, axis=1).astype(o_ref.dtype)


def _ret_rope_tables(s):
    ang = _rope_angles(np.arange(s), HEAD_DIM)
    ang64 = np.concatenate([ang, ang], axis=1)
    sign64 = np.concatenate([-np.ones(32), np.ones(32)]).astype(np.float32)
    return _cos_sin_tables(np.tile(ang64, (1, RET_HEADS)), np.tile(sign64, RET_HEADS))


def _ret_branch(p3, decay, gn_gain):
    b, seq, _ = p3.shape
    s = seq - CTX_LEN
    cos, sin = _ret_rope_tables(s)
    return pl.pallas_call(
        _ret_kernel,
        out_shape=jax.ShapeDtypeStruct((b, seq, BRANCH_W), BF16),
        grid=(b,),
        in_specs=[
            pl.BlockSpec((None, seq, 256), lambda i: (i, 0, 6)),
            pl.BlockSpec((None, seq, 256), lambda i: (i, 0, 7)),
            pl.BlockSpec((None, seq, 512), lambda i: (i, 0, 4)),
            pl.BlockSpec((None, seq, 512), lambda i: (i, 0, 5)),
            pl.BlockSpec((s, 256), lambda i: (0, 0)),
            pl.BlockSpec((s, 256), lambda i: (0, 0)),
            pl.BlockSpec((2, RET_HEADS), lambda i: (0, 0)),
            pl.BlockSpec((1, 128), lambda i: (0, 0)),
        ],
        out_specs=pl.BlockSpec((None, seq, BRANCH_W), lambda i: (i, 0, 0)),
        scratch_shapes=[pltpu.VMEM((seq, 256), F32), pltpu.VMEM((seq, 256), F32),
                        pltpu.VMEM((seq, 512), F32)],
        compiler_params=pltpu.CompilerParams(
            dimension_semantics=("arbitrary",), vmem_limit_bytes=VMEM_LIMIT),
        name="ret_branch",
    )(p3, p3, p3, p3, cos, sin, decay, gn_gain.reshape(1, 128))


DN_PAD = 8
DN_ROWS = 4 * DN_CHUNK


def _softplus(x):
    return jnp.maximum(x, 0.0) + jnp.log(1.0 + jnp.exp(-jnp.abs(x)))


def _dn_conv_silu(pad_s, w_ref, seq, width):
    blk = 256
    outs = []
    for cb in range(seq // blk):
        base = DN_PAD + cb * blk + (DN_PAD if cb * blk >= CTX_LEN else 0)
        ext = pad_s[base - DN_PAD:base + blk + DN_PAD, 0:width]
        n = blk + 2 * DN_PAD
        acc = ext * w_ref[DN_CONV // 2:DN_CONV // 2 + 1, :]
        for j in range(DN_CONV):
            if j == DN_CONV // 2:
                continue
            acc = acc + pltpu.roll(ext, (DN_CONV // 2 - j) % n, 0) * w_ref[j:j + 1, :]
        outs.append(_silu(acc[DN_PAD:DN_PAD + blk, :]))
    return outs


def _dn_fill_pad(pad_s, refs, seq):
    off = 0
    for ref in refs:
        w = ref.shape[1]
        pad_s[DN_PAD:DN_PAD + CTX_LEN, off:off + w] = ref[0:CTX_LEN, :].astype(F32)
        pad_s[2 * DN_PAD + CTX_LEN:2 * DN_PAD + seq, off:off + w] = ref[CTX_LEN:seq, :].astype(F32)
        off += w


def _dn_kernel(q_ref, k_ref, v_ref, g_ref, ab_ref, wq_ref, wk_ref, wv_ref, par_ref, ng_ref, o_ref,
               pad_s, qk_s, v_s, gb_s, o_s, ru_s, rw_s, rq_s, ra_s, rk_s, rg_s):
    seq = q_ref.shape[0]
    hp = pl.program_id(1)
    c = DN_CHUNK
    n_ctx_chunks = CTX_LEN // c
    n_chunks = seq // c
    blk = 256

    zeros_pad = jnp.zeros((DN_PAD, 256), F32)
    pad_s[0:DN_PAD, :] = zeros_pad
    pad_s[DN_PAD + CTX_LEN:2 * DN_PAD + CTX_LEN, :] = zeros_pad
    pad_s[2 * DN_PAD + seq:3 * DN_PAD + seq, :] = zeros_pad
    ones_bd = _group_ones(128, HEAD_DIM)

    _dn_fill_pad(pad_s, [q_ref], seq)
    for cb, x in enumerate(_dn_conv_silu(pad_s, wq_ref, seq, 128)):
        ss = _split_dot(x * x, ones_bd)
        qk_s[cb * blk:(cb + 1) * blk, 0:128] = x * lax.rsqrt(ss + EPS) * (HEAD_DIM ** -0.5)
    _dn_fill_pad(pad_s, [k_ref], seq)
    for cb, x in enumerate(_dn_conv_silu(pad_s, wk_ref, seq, 128)):
        ss = _split_dot(x * x, ones_bd)
        qk_s[cb * blk:(cb + 1) * blk, 128:256] = x * lax.rsqrt(ss + EPS)
    _dn_fill_pad(pad_s, [v_ref], seq)
    for cb, x in enumerate(_dn_conv_silu(pad_s, wv_ref, seq, 256)):
        v_s[cb * blk:(cb + 1) * blk, :] = x

    lane = lax.broadcasted_iota(jnp.int32, (1, 128), 1)
    a_log = par_ref[0:1, :]
    dt_bias = par_ref[1:2, :]
    shift = (128 - 2 * hp) % 128
    for cb in range(seq // blk):
        ab = ab_ref[cb * blk:(cb + 1) * blk, :]
        gdec = -jnp.exp(a_log) * _softplus(ab + dt_bias)
        beta = _sigmoid(ab)
        gb = jnp.where(lane < 2 * DN_HEADS, gdec, beta)
        gb_s[cb * blk:(cb + 1) * blk, :] = pltpu.roll(gb, shift, 1)

    ri = lax.broadcasted_iota(jnp.int32, (DN_ROWS, DN_ROWS), 0)
    ci = lax.broadcasted_iota(jnp.int32, (DN_ROWS, DN_ROWS), 1)
    same_blk = (ri // c) == (ci // c)
    is_fwd = ri < 2 * c
    rp, cp = ri % c, ci % c
    is_bwd = jnp.logical_not(is_fwd)
    strict = jnp.logical_and(same_blk, jnp.logical_or(jnp.logical_and(is_fwd, rp > cp),
                                                      jnp.logical_and(is_bwd, rp < cp)))
    incl = jnp.logical_or(strict, ri == ci)
    eye = jnp.where(ri == ci, 1.0, 0.0)

    r64 = lax.broadcasted_iota(jnp.int32, (c, c), 0)
    c64 = lax.broadcasted_iota(jnp.int32, (c, c), 1)
    tri_lo = jnp.where(r64 >= c64, 1.0, 0.0).astype(BF16)
    tri_up = jnp.where(r64 <= c64, 1.0, 0.0).astype(BF16)
    ones64 = jnp.ones((c, c), BF16)

    def split_left(m, x):
        hi = x.astype(BF16)
        lo = (x - hi.astype(F32)).astype(BF16)
        return _dot(m, hi) + _dot(m, lo)

    def chunk_start(d, n):
        if d == 0:
            return n * c
        rev_ctx = (n_ctx_chunks - 1 - n) * c
        rev_lat = CTX_LEN + (n_chunks - 1 - n) * c
        return jnp.where(n < n_ctx_chunks, rev_ctx, rev_lat)

    def stacked_cols(mats, cols, width):
        return jnp.concatenate(
            [jnp.broadcast_to(mats[d][:, cols[d][h]:cols[d][h] + 1], (c, width))
             for d in range(2) for h in range(2)], axis=0)

    def chunk_rows(n):
        n = jnp.asarray(n, jnp.int32)
        return [pl.multiple_of(chunk_start(d, n), c) for d in range(2)]

    def prep(n, slot):
        r0 = chunk_rows(n)
        qk = [qk_s[pl.ds(r0[d], c), :] for d in range(2)]
        vv = [v_s[pl.ds(r0[d], c), :] for d in range(2)]
        gb = [gb_s[pl.ds(r0[d], c), :] for d in range(2)]
        gc = [split_left(tri_lo, gb[0]), split_left(tri_up, gb[1])]
        gt = [split_left(ones64, gb[d]) for d in range(2)]
        g_cols = [[0, 1], [4, 5]]
        b_cols = [[8, 9], [12, 13]]
        gcb = stacked_cols(gc, g_cols, DN_ROWS)
        gtb = stacked_cols(gt, g_cols, DN_ROWS)
        bb = stacked_cols(gb, b_cols, DN_ROWS)

        qrow = jnp.concatenate([qk[d][:, 0:128] for d in range(2) for _ in range(2)], axis=0)
        krow = jnp.concatenate([qk[d][:, 128:256] for d in range(2) for _ in range(2)], axis=0)
        qx = jnp.where(same_blk, jnp.concatenate([qrow, qrow], axis=1), 0.0)
        kx = jnp.where(same_blk, jnp.concatenate([krow, krow], axis=1), 0.0)
        vx = jnp.concatenate([vv[d][:, h * 128:(h + 1) * 128] for d in range(2) for h in range(2)], axis=0)

        kxb = kx.astype(BF16)
        kk = _dot_nt(kxb, kxb)
        qkm = _dot_nt(qx.astype(BF16), kxb)
        diff = gcb - jnp.transpose(gcb)
        decay = jnp.where(incl, jnp.exp(jnp.where(incl, diff, 0.0)), 0.0)
        lower = jnp.where(strict, bb * kk * decay, 0.0)

        m = -lower
        x = eye + m
        for _ in range(5):
            mb = m.astype(BF16)
            m = _dot(mb, mb)
            x = x + _dot(x.astype(BF16), m.astype(BF16))
        tb = x.astype(BF16)

        eg = jnp.exp(gcb)
        ru_s[slot] = _dot(tb, (vx * bb[:, 0:128]).astype(BF16))
        rw_s[slot] = _dot(tb, (kx * bb * eg).astype(BF16)).astype(BF16)
        rq_s[slot] = (qx * eg).astype(BF16)
        ra_s[slot] = (qkm * decay).astype(BF16)
        rk_s[slot] = (kx * jnp.exp(gtb - gcb)).astype(BF16)
        rg_s[slot] = jnp.exp(gtb[:, 0:128])

    def apply(n, slot, s):
        r0 = chunk_rows(n)
        sb = s.astype(BF16)
        v_new = (ru_s[slot] - _dot(rw_s[slot], sb)).astype(BF16)
        o = _dot(rq_s[slot], sb) + _dot(ra_s[slot], v_new)
        s_new = s * rg_s[slot] + _dot_tn(rk_s[slot], v_new)
        for d in range(2):
            od = jnp.concatenate([o[(2 * d + h) * c:(2 * d + h + 1) * c, :] for h in range(2)], axis=1)
            o_s[d, pl.ds(r0[d], c), :] = od
        return s_new

    prep(0, 0)
    prep(1, 1)

    def group(g, s):
        n = 4 * g
        prep(n + 2, 2)
        prep(n + 3, 3)
        s = apply(n, 0, s)
        s = apply(n + 1, 1, s)
        prep(jnp.minimum(n + 4, n_chunks - 1), 0)
        prep(jnp.minimum(n + 5, n_chunks - 1), 1)
        s = apply(n + 2, 2, s)
        s = apply(n + 3, 3, s)
        return s

    assert n_chunks % 4 == 0
    lax.fori_loop(0, n_chunks // 4, group, jnp.zeros((DN_ROWS, 128), F32))

    ng = ng_ref[...]
    for cb in range(seq // blk):
        sl = slice(cb * blk, (cb + 1) * blk)
        outs = []
        for h in range(2):
            hs = slice(h * 128, (h + 1) * 128)
            outs.append(_rms_rows(o_s[0, sl, hs] + o_s[1, sl, hs], ng) * _silu(g_ref[sl, hs].astype(F32)))
        o_ref[sl, :] = jnp.concatenate(outs, axis=1).astype(o_ref.dtype)


def _dn_branch(p3, pab3, conv_w, a_log, dt_bias, norm_gain):
    b, seq, _ = p3.shape
    par = jnp.zeros((2, 128), F32)
    par = par.at[0, 0:2 * DN_HEADS].set(a_log.reshape(-1)).at[1, 0:2 * DN_HEADS].set(dt_bias.reshape(-1))
    qb, kb_, vb, gb_ = 4608 // 128, 4864 // 128, 5120 // 256, 5632 // 256
    ring = lambda r, w, dt: pltpu.VMEM((4, r, w), dt)
    return pl.pallas_call(
        _dn_kernel,
        out_shape=jax.ShapeDtypeStruct((b, seq, BRANCH_W), BF16),
        grid=(b, 2),
        in_specs=[
            pl.BlockSpec((None, seq, 128), lambda i, hp: (i, 0, qb + hp)),
            pl.BlockSpec((None, seq, 128), lambda i, hp: (i, 0, kb_ + hp)),
            pl.BlockSpec((None, seq, 256), lambda i, hp: (i, 0, vb + hp)),
            pl.BlockSpec((None, seq, 256), lambda i, hp: (i, 0, gb_ + hp)),
            pl.BlockSpec((None, seq, 128), lambda i, hp: (i, 0, 0)),
            pl.BlockSpec((DN_CONV, 128), lambda i, hp: (0, hp)),
            pl.BlockSpec((DN_CONV, 128), lambda i, hp: (0, 2 + hp)),
            pl.BlockSpec((DN_CONV, 256), lambda i, hp: (0, 2 + hp)),
            pl.BlockSpec((2, 128), lambda i, hp: (0, 0)),
            pl.BlockSpec((1, 128), lambda i, hp: (0, 0)),
        ],
        out_specs=pl.BlockSpec((None, seq, 256), lambda i, hp: (i, 0, hp)),
        scratch_shapes=[pltpu.VMEM((seq + 3 * DN_PAD, 256), F32), pltpu.VMEM((seq, 256), F32),
                        pltpu.VMEM((seq, 256), F32), pltpu.VMEM((seq, 128), F32),
                        pltpu.VMEM((2, seq, 256), F32),
                        ring(DN_ROWS, 128, F32), ring(DN_ROWS, DN_ROWS, BF16), ring(DN_ROWS, DN_ROWS, BF16),
                        ring(DN_ROWS, DN_ROWS, BF16), ring(DN_ROWS, DN_ROWS, BF16), ring(DN_ROWS, 128, F32)],
        compiler_params=pltpu.CompilerParams(
            dimension_semantics=("arbitrary", "arbitrary"), vmem_limit_bytes=VMEM_LIMIT),
        name="dn_branch",
    )(p3, p3, p3, p3, pab3, conv_w, conv_w, conv_w, par, norm_gain.reshape(1, 128))


def _combine_kernel(hz_ref, y0_ref, y1_ref, y2_ref, y3_ref, xs_ref, mod_ref, wg_ref, bg_ref, wb_ref,
                    wo_ref, gf_ref, xo_ref, hl_ref, *, tiles_per_b):
    i = pl.program_id(0)
    tm = xs_ref.shape[0]
    hz = hz_ref[...]
    acc = jnp.zeros((tm, D_MODEL), F32)
    for n, y_ref in enumerate((y0_ref, y1_ref, y2_ref, y3_ref)):
        cs = slice(n * D_MODEL, (n + 1) * D_MODEL)
        gate = _sigmoid(_dot(hz, wg_ref[:, cs]) + bg_ref[:, cs])
        acc = acc + gate * _dot(y_ref[...], wb_ref[n])
    mix = _dot(acc.astype(BF16), wo_ref[...])
    xn = xs_ref[...] + _mod_rows(mod_ref, i, tiles_per_b, tm, 2) * mix
    xo_ref[...] = xn
    h = _rms_rows(xn, gf_ref[...])
    hl_ref[...] = h * (1.0 + _mod_rows(mod_ref, i, tiles_per_b, tm, 4)) + _mod_rows(mod_ref, i, tiles_per_b, tm, 3)


def _combine(hz, ys, xs, mod, w_mgate, b_mgate, w_branch, w_out, g_ffn, seq_len):
    t = xs.shape[0]
    tiles_per_b = seq_len // TM
    tok = lambda i: (i, 0)
    fix2 = lambda i: (0, 0)
    return pl.pallas_call(
        functools.partial(_combine_kernel, tiles_per_b=tiles_per_b),
        out_shape=(jax.ShapeDtypeStruct((t, D_MODEL), F32), jax.ShapeDtypeStruct((t, D_MODEL), F32)),
        grid=(t // TM,),
        in_specs=[
            pl.BlockSpec((TM, D_MODEL), tok),
            pl.BlockSpec((TM, BRANCH_W), tok), pl.BlockSpec((TM, BRANCH_W), tok),
            pl.BlockSpec((TM, BRANCH_W), tok), pl.BlockSpec((TM, BRANCH_W), tok),
            pl.BlockSpec((TM, D_MODEL), tok),
            pl.BlockSpec((16, 6 * D_MODEL), fix2),
            pl.BlockSpec((D_MODEL, N_BRANCH * D_MODEL), fix2),
            pl.BlockSpec((1, N_BRANCH * D_MODEL), fix2),
            pl.BlockSpec((N_BRANCH, BRANCH_W, D_MODEL), lambda i: (0, 0, 0)),
            pl.BlockSpec((D_MODEL, D_MODEL), fix2),
            pl.BlockSpec((1, D_MODEL), fix2),
        ],
        out_specs=(pl.BlockSpec((TM, D_MODEL), tok), pl.BlockSpec((TM, D_MODEL), tok)),
        compiler_params=pltpu.CompilerParams(
            dimension_semantics=("arbitrary",), vmem_limit_bytes=VMEM_LIMIT),
        name="combine",
    )(hz, *ys, xs, mod, w_mgate, b_mgate.reshape(1, -1), w_branch, w_out, g_ffn.reshape(1, D_MODEL))


def _route_kernel(h_ref, w_ref, b_ref, topi_ref, gate_ref, rank_ref, cnt_ref, base_s):
    i = pl.program_id(0)

    @pl.when(i == 0)
    def _():
        base_s[...] = jnp.zeros(base_s.shape, F32)

    tm = h_ref.shape[0]
    logits = _split_dot_w(h_ref[...], w_ref[...]) + b_ref[...]
    lane = lax.broadcasted_iota(jnp.int32, (tm, 128), 1)
    l = logits
    idxs, vals, hots = [], [], []
    for _ in range(TOP_K):
        m = l.max(axis=-1, keepdims=True)
        idx = jnp.min(jnp.where(l == m, lane, 128), axis=-1, keepdims=True)
        hot = lane == idx
        idxs.append(idx)
        vals.append(m)
        hots.append(hot)
        l = jnp.where(hot, -3e38, l)
    es = [jnp.exp(v - vals[0]) for v in vals]
    den = es[0] + es[1] + es[2] + es[3]

    cnt = jnp.zeros((tm, 128), F32)
    for hot in hots:
        cnt = cnt + jnp.where(hot, 1.0, 0.0)
    r = lax.broadcasted_iota(jnp.int32, (tm, tm), 0)
    c = lax.broadcasted_iota(jnp.int32, (tm, tm), 1)
    before = jnp.where(r > c, 1.0, 0.0).astype(BF16)
    prior = _dot(before, cnt.astype(BF16)) + base_s[...]

    topi = jnp.zeros((tm, 128), jnp.int32)
    gate = jnp.zeros((tm, 128), F32)
    rank = jnp.zeros((tm, 128), jnp.int32)
    for k in range(TOP_K):
        rk = jnp.sum(jnp.where(hots[k], prior, 0.0), axis=-1, keepdims=True)
        topi = jnp.where(lane == k, idxs[k], topi)
        gate = jnp.where(lane == k, es[k] / den, gate)
        rank = jnp.where(lane == k, rk.astype(jnp.int32), rank)
    topi_ref[...] = topi
    gate_ref[...] = gate
    rank_ref[...] = rank
    base_s[...] = base_s[...] + jnp.sum(cnt, axis=0, keepdims=True)
    cnt_ref[...] = base_s[...]


def _route(hl, router_w, router_b, seq_len, latent_only):
    per_b, src = _token_tiles(seq_len, TM_ROUTE, latent_only)
    t = hl.shape[0] // (seq_len // TM_ROUTE) * per_b
    w = jnp.pad(router_w, ((0, 0), (0, 128 - N_EXPERTS)))
    b = jnp.pad(router_b, (0, 128 - N_EXPERTS), constant_values=NEG_BIG).reshape(1, 128)
    tok = lambda i: (i, 0)
    return pl.pallas_call(
        _route_kernel,
        out_shape=(jax.ShapeDtypeStruct((t, 128), jnp.int32), jax.ShapeDtypeStruct((t, 128), F32),
                   jax.ShapeDtypeStruct((t, 128), jnp.int32), jax.ShapeDtypeStruct((1, 128), F32)),
        grid=(t // TM_ROUTE,),
        in_specs=[pl.BlockSpec((TM_ROUTE, D_MODEL), lambda i: (src(i), 0)),
                  pl.BlockSpec((D_MODEL, 128), lambda i: (0, 0)),
                  pl.BlockSpec((1, 128), lambda i: (0, 0))],
        out_specs=(pl.BlockSpec((TM_ROUTE, 128), tok), pl.BlockSpec((TM_ROUTE, 128), tok),
                   pl.BlockSpec((TM_ROUTE, 128), tok), pl.BlockSpec((1, 128), lambda i: (0, 0))),
        scratch_shapes=[pltpu.VMEM((1, 128), F32)],
        compiler_params=pltpu.CompilerParams(dimension_semantics=("arbitrary",)),
        name="route",
    )(hl, w, b)


def _dispatch_kernel(last_ref, dest_ref, h_ref, rows_ref, tile_s, zero_s, sem):
    i = pl.program_id(0)
    tm = h_ref.shape[0]

    @pl.when(i == 0)
    def _():
        zero_s[...] = jnp.zeros(zero_s.shape, F32)

        def blk_copy(blk):
            r = pl.multiple_of(blk * MOE_BM, MOE_BM)
            return pltpu.make_async_copy(zero_s, rows_ref.at[pl.ds(r, MOE_BM)], sem)

        for e in range(N_EXPERTS):
            blk_copy(last_ref[e]).start()
        for e in range(N_EXPERTS):
            blk_copy(last_ref[e]).wait()

        n_blocks = rows_ref.shape[0] // MOE_BM

        def clear_tail(blk, carry):
            blk_copy(blk).start()
            blk_copy(blk).wait()
            return carry

        lax.fori_loop(last_ref[N_EXPERTS], n_blocks, clear_tail, 0)

    tile_s[...] = h_ref[...].reshape(tile_s.shape)

    def issue(t, carry):
        for k in range(TOP_K):
            d = dest_ref[t * TOP_K + k]
            pltpu.make_async_copy(tile_s.at[t], rows_ref.at[d], sem).start(priority=k % 2)
        return carry

    lax.fori_loop(0, tm, issue, 0)
    for k in range(TOP_K):
        pltpu.make_async_copy(tile_s, rows_ref.at[pl.ds(0, tm)], sem).wait()


ROW_TILE = (8, D_MODEL // 8)


def _dispatch(last_blk, dest_flat, hl, n_rows, seq_len, latent_only):
    _, src = _token_tiles(seq_len, TM_DMA, latent_only)
    t = dest_flat.shape[0] // TOP_K
    grid_spec = pltpu.PrefetchScalarGridSpec(
        num_scalar_prefetch=1,
        grid=(t // TM_DMA,),
        in_specs=[pl.BlockSpec((TM_DMA * TOP_K,), lambda i, lb: (i,), memory_space=pltpu.SMEM),
                  pl.BlockSpec((TM_DMA, D_MODEL), lambda i, lb: (src(i), 0))],
        out_specs=pl.BlockSpec(memory_space=pl.ANY),
        scratch_shapes=[pltpu.VMEM((TM_DMA,) + ROW_TILE, F32), pltpu.VMEM((MOE_BM,) + ROW_TILE, F32),
                        pltpu.SemaphoreType.DMA(())],
    )
    return pl.pallas_call(
        _dispatch_kernel,
        out_shape=jax.ShapeDtypeStruct((n_rows,) + ROW_TILE, F32),
        grid_spec=grid_spec,
        compiler_params=pltpu.CompilerParams(dimension_semantics=("arbitrary",), has_side_effects=True),
        name="moe_dispatch",
    )(last_blk, dest_flat, hl)


def _expert_kernel(be_ref, nu_ref, x_ref, wgu_ref, bgu_ref, wd_ref, bd_ref, o_ref, wgu_s, wd_s):
    j = pl.program_id(0)
    d_ff = wd_ref.shape[0]
    used = j < nu_ref[0]
    new_expert = jnp.logical_or(j == 0, be_ref[j] != be_ref[jnp.maximum(j - 1, 0)])

    @pl.when(jnp.logical_and(used, new_expert))
    def _():
        rb = 256
        for r in range(0, D_MODEL, rb):
            wgu_s[r:r + rb, :] = wgu_ref[r:r + rb, :].astype(BF16)
        for r in range(0, d_ff, rb):
            wd_s[r:r + rb, :] = wd_ref[r:r + rb, :].astype(BF16)

    @pl.when(used)
    def _():
        x = x_ref[...].reshape(x_ref.shape[0], D_MODEL)
        gu = _dot(x.astype(BF16), wgu_s[...]) + bgu_ref[...]
        gate = jnp.minimum(gu[:, :d_ff], SWIGLU_LIMIT)
        up = jnp.clip(gu[:, d_ff:], -SWIGLU_LIMIT, SWIGLU_LIMIT)
        y = (up + 1.0) * (gate * _sigmoid(SWIGLU_ALPHA * gate))
        o_ref[...] = (_dot(y.astype(BF16), wd_s[...]) + bd_ref[...]).reshape(o_ref.shape)

    @pl.when(jnp.logical_not(used))
    def _():
        o_ref[...] = jnp.zeros(o_ref.shape, F32)


def _experts(block_e, n_used, rows, layer, w_gu, b_gu, w_down, b_down):
    n_rows = rows.shape[0]
    d_ff = w_down.shape[2]
    depth = w_gu.shape[0]
    row_blk = lambda j, be, nu: (jnp.minimum(j, nu[0] - 1), 0, 0)
    grid_spec = pltpu.PrefetchScalarGridSpec(
        num_scalar_prefetch=2,
        grid=(n_rows // MOE_BM,),
        in_specs=[
            pl.BlockSpec((MOE_BM,) + ROW_TILE, row_blk),
            pl.BlockSpec((None, None, D_MODEL, 2 * d_ff), lambda j, be, nu: (layer, be[j], 0, 0)),
            pl.BlockSpec((None, None, 1, 2 * d_ff), lambda j, be, nu: (layer, be[j], 0, 0)),
            pl.BlockSpec((None, None, d_ff, D_MODEL), lambda j, be, nu: (layer, be[j], 0, 0)),
            pl.BlockSpec((None, None, 1, D_MODEL), lambda j, be, nu: (layer, be[j], 0, 0)),
        ],
        out_specs=pl.BlockSpec((MOE_BM,) + ROW_TILE, lambda j, be, nu: (j, 0, 0)),
        scratch_shapes=[pltpu.VMEM((D_MODEL, 2 * d_ff), BF16), pltpu.VMEM((d_ff, D_MODEL), BF16)],
    )
    return pl.pallas_call(
        _expert_kernel,
        out_shape=jax.ShapeDtypeStruct((n_rows,) + ROW_TILE, F32),
        grid_spec=grid_spec,
        compiler_params=pltpu.CompilerParams(
            dimension_semantics=("arbitrary",), vmem_limit_bytes=VMEM_LIMIT),
        name="moe_experts",
    )(block_e, n_used, rows, w_gu, b_gu.reshape(depth, N_EXPERTS, 1, -1), w_down,
      b_down.reshape(depth, N_EXPERTS, 1, -1))


def _finish_kernel(dest_ref, gate_ref, xs_ref, mod_ref, rows_ref, o_ref, buf, sem, *, tiles_per_b, has_ctx):
    i = pl.program_id(0)
    tm = xs_ref.shape[0]

    half = tm // 2

    def issue_half(p):
        def issue(t, carry):
            for k in range(TOP_K):
                d = dest_ref[t * TOP_K + k]
                pltpu.make_async_copy(rows_ref.at[d], buf.at[k, t], sem.at[p]).start(priority=k % 2)
            return carry
        lax.fori_loop(p * half, (p + 1) * half, issue, 0)

    issue_half(0)
    issue_half(1)
    mod = _mod_rows(mod_ref, i, tiles_per_b, tm, 5, has_ctx)
    for p in range(2):
        rs = slice(p * half, (p + 1) * half)
        for k in range(TOP_K):
            pltpu.make_async_copy(rows_ref.at[pl.ds(0, half)], buf.at[k, rs], sem.at[p]).wait()
        gate = gate_ref[rs, :]
        y = gate[:, 0:1] * buf[0, rs].reshape(half, D_MODEL)
        for k in range(1, TOP_K):
            y = y + gate[:, k:k + 1] * buf[k, rs].reshape(half, D_MODEL)
        o_ref[rs, :] = xs_ref[rs, :] + (mod if mod.shape[0] == 1 else mod[rs]) * y


def _finish(dest_flat, gates, xs, mod, out_rows, seq_len, latent_only):
    tiles_per_b, src = _token_tiles(seq_len, TM_DMA, latent_only)
    t = dest_flat.shape[0] // TOP_K
    return pl.pallas_call(
        functools.partial(_finish_kernel, tiles_per_b=tiles_per_b, has_ctx=not latent_only),
        out_shape=jax.ShapeDtypeStruct((t, D_MODEL), F32),
        grid=(t // TM_DMA,),
        in_specs=[pl.BlockSpec((TM_DMA * TOP_K,), lambda i: (i,), memory_space=pltpu.SMEM),
                  pl.BlockSpec((TM_DMA, 128), lambda i: (i, 0)),
                  pl.BlockSpec((TM_DMA, D_MODEL), lambda i: (src(i), 0)),
                  pl.BlockSpec((16, 6 * D_MODEL), lambda i: (0, 0)),
                  pl.BlockSpec(memory_space=pl.ANY)],
        out_specs=pl.BlockSpec((TM_DMA, D_MODEL), lambda i: (i, 0)),
        scratch_shapes=[pltpu.VMEM((TOP_K, TM_DMA) + ROW_TILE, F32), pltpu.SemaphoreType.DMA((2,))],
        compiler_params=pltpu.CompilerParams(
            dimension_semantics=("arbitrary",), vmem_limit_bytes=VMEM_LIMIT),
        name="moe_finish",
    )(dest_flat, gates, xs, mod, out_rows)


def _moe_plan(topi, rank, counts):
    t = topi.shape[0]
    padded = (counts + MOE_BM - 1) // MOE_BM * MOE_BM
    pad_end = jnp.cumsum(padded)
    pad_start = pad_end - padded
    onehot = topi[:, :, None] == jnp.arange(N_EXPERTS, dtype=jnp.int32)[None, None, :]
    dest = jnp.sum(jnp.where(onehot, pad_start[None, None, :], 0), axis=-1) + rank
    n_blocks = -(-(t * TOP_K + N_EXPERTS * (MOE_BM - 1)) // MOE_BM)
    starts = jnp.arange(n_blocks, dtype=jnp.int32) * MOE_BM
    block_e = jnp.sum(starts[:, None] >= pad_end[None, :], axis=-1).astype(jnp.int32)
    block_e = jnp.minimum(block_e, N_EXPERTS - 1)
    n_used = (pad_end[-1] // MOE_BM).astype(jnp.int32).reshape(1)
    last_blk = jnp.maximum(pad_end // MOE_BM - 1, 0).astype(jnp.int32)
    last_blk = jnp.concatenate([last_blk, n_used])
    return dest.reshape(-1).astype(jnp.int32), block_e, n_used, last_blk, n_blocks * MOE_BM


def kernel(x, c, ctx, c_ctx, w_ada, b_ada, norm_mix, norm_ffn, w_in, na_qk_gain, na_rpb, ret_decay, ret_gn,
           diff_qk_gain, diff_lam, diff_subln, dn_conv, dn_a_log, dn_dt_bias, dn_norm, w_branch, w_mgate,
           b_mgate, w_out, router_w, router_b, w_gu, b_gu, w_down, b_down):
    b, s, d = x.shape
    n_ctx = ctx.shape[1]
    seq = n_ctx + s
    depth = w_ada.shape[0]
    assert (d, n_ctx, s) == (D_MODEL, CTX_LEN, 2048) and b <= 8 and seq % TM == 0

    xs = jnp.concatenate([ctx, x], axis=1).reshape(b * seq, d)
    cc = jnp.zeros((16, d), F32).at[0:b].set(c).at[8].set(c_ctx)
    mods = _ada(cc, w_ada, b_ada)

    for l in range(depth):
        lam_init = 0.8 - 0.6 * math.exp(-0.3 * l)
        w_main = w_in[l, :, :D_IN_MAIN].astype(BF16)
        w_ab = jnp.pad(w_in[l, :, D_IN_MAIN:], ((0, 0), (0, 128 - (D_IN - D_IN_MAIN)))).astype(BF16)
        p, pab, hz = _in_proj(xs, mods[l], norm_mix[l], w_main, w_ab, seq)
        p3 = p.reshape(b, seq, D_IN_MAIN)
        ys = (_na_branch(p3, na_qk_gain[l], na_rpb[l]),
              _ret_branch(p3, ret_decay[l], ret_gn[l]),
              _diff_branch(p3, diff_qk_gain[l], diff_lam[l], diff_subln[l], lam_init),
              _dn_branch(p3, pab.reshape(b, seq, 128), dn_conv[l], dn_a_log[l], dn_dt_bias[l], dn_norm[l]))
        ys = tuple(y.reshape(b * seq, BRANCH_W) for y in ys)
        xs, hl = _combine(hz, ys, xs, mods[l], w_mgate[l].astype(BF16), b_mgate[l], w_branch[l].astype(BF16),
                          w_out[l].astype(BF16), norm_ffn[l], seq)
        last = l == depth - 1
        topi, gates, rank, cnt = _route(hl, router_w[l], router_b[l], seq, last)
        counts = cnt[0, :N_EXPERTS].astype(jnp.int32)
        dest, block_e, n_used, last_blk, n_rows = _moe_plan(topi[:, :TOP_K], rank[:, :TOP_K], counts)
        rows = _dispatch(last_blk, dest, hl, n_rows, seq, last)
        out_rows = _experts(block_e, n_used, rows, l, w_gu, b_gu, w_down, b_down)
        xs = _finish(dest, gates, xs, mods[l], out_rows, seq, last)

    return xs.reshape(b, s, d)
```

```python
import functools
import math

import numpy as np
import jax
import jax.numpy as jnp
from jax import lax
from jax.experimental import pallas as pl
from jax.experimental.pallas import tpu as pltpu

F32 = jnp.float32
BF16 = jnp.bfloat16

D_MODEL = 1024
GRID_W = 64
CTX_LEN = 256
HEAD_DIM = 64
N_BRANCH = 4
BRANCH_W = 512
NA_HEADS = 8
NA_WIN_R = 8
NA_WIN_C = 16
RET_HEADS = 4
RET_CHUNK = 128
DIFF_HEADS = 4
DN_HEADS = 4
DN_CONV = 5
DN_CHUNK = 64
N_EXPERTS = 32
TOP_K = 4
SWIGLU_LIMIT = 7.0
SWIGLU_ALPHA = 1.702
ROPE_BASE = 10000.0
EPS = 1e-6
D_IN = 6160
D_IN_MAIN = 6144
NEG_BIG = -1e30

VMEM_LIMIT = 56 * 1024 * 1024

TM = 768
TM_IN = 1152
TN_IN = 3072
TM_ROUTE = 256
TM_DMA = 256
MOE_BM = 256


def _dot(a, b):
    return jnp.dot(a, b, preferred_element_type=F32)


def _dot_nt(a, b):
    return lax.dot_general(a, b, (((1,), (1,)), ((), ())), preferred_element_type=F32)


def _dot_tn(a, b):
    return lax.dot_general(a, b, (((0,), (0,)), ((), ())), preferred_element_type=F32)


def _split_dot(x, w_bf16):
    hi = x.astype(BF16)
    lo = (x - hi.astype(F32)).astype(BF16)
    return _dot(hi, w_bf16) + _dot(lo, w_bf16)


def _group_ones(n, group):
    r = lax.broadcasted_iota(jnp.int32, (n, n), 0) // group
    c = lax.broadcasted_iota(jnp.int32, (n, n), 1) // group
    return jnp.where(r == c, 1.0, 0.0).astype(BF16)


def _sigmoid(x):
    return 1.0 / (1.0 + jnp.exp(-x))


def _silu(x):
    return x * _sigmoid(x)


def _ada_kernel(c_ref, w_ref, b_ref, o_ref):
    c = c_ref[...]
    a = _silu(c)
    o_ref[...] = _split_dot_w(a, w_ref[...]) + b_ref[...]


def _split_dot_w(a, w):
    ah = a.astype(BF16)
    al = (a - ah.astype(F32)).astype(BF16)
    wh = w.astype(BF16)
    wl = (w - wh.astype(F32)).astype(BF16)
    return _dot(ah, wh) + _dot(al, wh) + _dot(ah, wl)


def _ada(cc, w_ada, b_ada):
    depth = w_ada.shape[0]
    n = w_ada.shape[2]
    tn = 768
    return pl.pallas_call(
        _ada_kernel,
        out_shape=jax.ShapeDtypeStruct((depth, 16, n), F32),
        grid=(depth, n // tn),
        in_specs=[
            pl.BlockSpec((16, D_MODEL), lambda l, j: (0, 0)),
            pl.BlockSpec((None, D_MODEL, tn), lambda l, j: (l, 0, j)),
            pl.BlockSpec((None, 1, tn), lambda l, j: (l, 0, j)),
        ],
        out_specs=pl.BlockSpec((None, 16, tn), lambda l, j: (l, 0, j)),
        compiler_params=pltpu.CompilerParams(dimension_semantics=("arbitrary", "arbitrary")),
        name="ada_mod",
    )(cc, w_ada, b_ada.reshape(depth, 1, n))


def _mod_rows(mod_ref, tile, tiles_per_b, rows, which, has_ctx=True):
    b = tile // tiles_per_b
    lo = which * D_MODEL
    ml = mod_ref[pl.ds(b, 1), lo:lo + D_MODEL]
    if not has_ctx:
        return ml
    mc = mod_ref[8:9, lo:lo + D_MODEL]
    rid = lax.broadcasted_iota(jnp.int32, (rows, 1), 0)
    is_ctx = jnp.logical_and(tile % tiles_per_b == 0, rid < CTX_LEN)
    return jnp.where(is_ctx, mc, ml)


def _token_tiles(seq_len, tile, latent_only):
    per_b = seq_len // tile
    if not latent_only:
        return per_b, (lambda i: i)
    skip = CTX_LEN // tile
    assert skip * tile == CTX_LEN
    return per_b - skip, (lambda i: (i // (per_b - skip)) * per_b + skip + i % (per_b - skip))


def _rms_rows(x, g):
    ms = jnp.mean(x * x, axis=-1, keepdims=True)
    return x * lax.rsqrt(ms + EPS) * g


def _in_proj_kernel(x_ref, mod_ref, g_ref, w_ref, wab_ref, p_ref, pab_ref, hz_ref, hz_s, *, tiles_per_b):
    i = pl.program_id(0)
    j = pl.program_id(1)

    @pl.when(j == 0)
    def _():
        tm = x_ref.shape[0]
        y = _rms_rows(x_ref[...], g_ref[...])
        shift = _mod_rows(mod_ref, i, tiles_per_b, tm, 0)
        scale = _mod_rows(mod_ref, i, tiles_per_b, tm, 1)
        hz = (y * (1.0 + scale) + shift).astype(BF16)
        hz_s[...] = hz
        hz_ref[...] = hz
        pab_ref[...] = _dot(hz, wab_ref[...])

    p_ref[...] = _dot(hz_s[...], w_ref[...]).astype(p_ref.dtype)


def _in_proj(xs, mod, g, w_main, w_ab, seq_len):
    t = xs.shape[0]
    n = w_main.shape[1]
    tm = TM_IN
    tiles_per_b = seq_len // tm
    return pl.pallas_call(
        functools.partial(_in_proj_kernel, tiles_per_b=tiles_per_b),
        out_shape=(jax.ShapeDtypeStruct((t, n), BF16),
                   jax.ShapeDtypeStruct((t, 128), F32),
                   jax.ShapeDtypeStruct((t, D_MODEL), BF16)),
        grid=(t // tm, n // TN_IN),
        in_specs=[
            pl.BlockSpec((tm, D_MODEL), lambda i, j: (i, 0)),
            pl.BlockSpec((16, 6 * D_MODEL), lambda i, j: (0, 0)),
            pl.BlockSpec((1, D_MODEL), lambda i, j: (0, 0)),
            pl.BlockSpec((D_MODEL, TN_IN), lambda i, j: (0, j)),
            pl.BlockSpec((D_MODEL, 128), lambda i, j: (0, 0)),
        ],
        out_specs=(pl.BlockSpec((tm, TN_IN), lambda i, j: (i, j)),
                   pl.BlockSpec((tm, 128), lambda i, j: (i, 0)),
                   pl.BlockSpec((tm, D_MODEL), lambda i, j: (i, 0))),
        scratch_shapes=[pltpu.VMEM((tm, D_MODEL), BF16)],
        compiler_params=pltpu.CompilerParams(
            dimension_semantics=("arbitrary", "arbitrary"), vmem_limit_bytes=VMEM_LIMIT),
        name="in_proj",
    )(xs, mod, g.reshape(1, D_MODEL), w_main, w_ab)


def _group_rms(x, g, ones_bd, group):
    ss = _split_dot(x * x, ones_bd)
    return x * lax.rsqrt(ss * (1.0 / group) + EPS) * g


def _softmax_parts(parts):
    m = parts[0].max(axis=-1, keepdims=True)
    for s in parts[1:]:
        m = jnp.maximum(m, s.max(axis=-1, keepdims=True))
    es = [jnp.exp(s - m) for s in parts]
    l = es[0].sum(axis=-1, keepdims=True)
    for e in es[1:]:
        l = l + e.sum(axis=-1, keepdims=True)
    return es, l


NA_GROUP = 4
NA_WROWS_MID = NA_WIN_R + NA_GROUP - 1
NA_KEYS_EDGE = CTX_LEN + NA_WIN_R * GRID_W
NA_KEYS_MID = CTX_LEN + NA_WROWS_MID * GRID_W


def _na_bias_tables(w_ref, top_s, mid_s, bot_s):
    c_i = lax.broadcasted_iota(jnp.int32, (GRID_W, 128), 0)
    l_i = lax.broadcasted_iota(jnp.int32, (GRID_W, 128), 1)
    c0 = jnp.clip(c_i - NA_WIN_C // 2, 0, GRID_W - NA_WIN_C)
    in_lo = jnp.logical_and(l_i >= c0, l_i < c0 + NA_WIN_C)
    in_hi = jnp.logical_and(l_i - GRID_W >= c0, l_i - GRID_W < c0 + NA_WIN_C)
    neg = jnp.full((GRID_W, 128), NEG_BIG, F32)
    plans = (
        (top_s, NA_WIN_R, lambda u, j: j - u + NA_WIN_R - 1),
        (mid_s, NA_WROWS_MID, lambda u, j: j - u + NA_WIN_R // 2 - 1 if 0 <= j - u < NA_WIN_R else None),
        (bot_s, NA_WIN_R, lambda u, j: j - u + NA_WIN_R // 2 - 1),
    )
    for h in range(2):
        lo, hi = [], []
        for dr in range(2 * NA_WIN_R - 1):
            row = jnp.broadcast_to(w_ref[h, dr:dr + 1, :], (GRID_W, 128))
            lo.append(jnp.where(in_lo, pltpu.roll(row, 128 - (GRID_W - 1), 1, stride=1, stride_axis=0), neg))
            hi.append(jnp.where(in_hi, pltpu.roll(row, 1, 1, stride=1, stride_axis=0), neg))
        for tab, wrows, dr_of in plans:
            tab[h, :, 0:CTX_LEN] = jnp.zeros((NA_GROUP * GRID_W, CTX_LEN), F32)
            for u in range(NA_GROUP):
                rs = slice(u * GRID_W, (u + 1) * GRID_W)
                for j in range(0, wrows, 2):
                    d_lo = dr_of(u, j)
                    d_hi = dr_of(u, j + 1) if j + 1 < wrows else None
                    t_lo = neg if d_lo is None else lo[d_lo]
                    t_hi = neg if d_hi is None else hi[d_hi]
                    tile = jnp.where(l_i < GRID_W, t_lo, t_hi)
                    col = CTX_LEN + j * GRID_W
                    if j + 1 < wrows:
                        tab[h, rs, col:col + 128] = tile
                    else:
                        tab[h, rs, col:col + GRID_W] = tile[:, 0:GRID_W]


def _na_kernel(q_ref, k_ref, v_ref, gain_ref, w_ref, o_ref, qn_s, kn_s, top_s, mid_s, bot_s, kcat_s, vcat_s):
    seq = q_ref.shape[0]
    rows = (seq - CTX_LEN) // GRID_W

    @pl.when(pl.program_id(1) == 0)
    def _():
        _na_bias_tables(w_ref, top_s, mid_s, bot_s)

    ones_bd = _group_ones(128, HEAD_DIM)
    gq = gain_ref[0:1, :]
    gk = gain_ref[1:2, :]
    scale = HEAD_DIM ** -0.5
    blk = 256
    for c in range(seq // blk):
        sl = slice(c * blk, (c + 1) * blk)
        qn_s[sl, :] = (_group_rms(q_ref[sl, :].astype(F32), gq, ones_bd, HEAD_DIM) * scale).astype(BF16)
        kn_s[sl, :] = _group_rms(k_ref[sl, :].astype(F32), gk, ones_bd, HEAD_DIM).astype(BF16)

    lane = lax.broadcasted_iota(jnp.int32, (1, 128), 1)
    head_mask = [lane < HEAD_DIM, lane >= HEAD_DIM]
    kc = kn_s[0:CTX_LEN, :]
    vc = v_ref[0:CTX_LEN, :]

    qc = qn_s[0:CTX_LEN, :]
    outs = []
    for h in range(2):
        qm = jnp.where(head_mask[h], qc, jnp.zeros_like(qc))
        (e,), l = _softmax_parts([_dot_nt(qm, kc)])
        outs.append(_dot(e.astype(BF16), vc) / l)
    o_ref[0:CTX_LEN, :] = jnp.where(head_mask[0], outs[0], outs[1]).astype(o_ref.dtype)

    assert NA_GROUP == NA_WIN_R // 2 and rows % NA_GROUP == 0
    kcat_s[0:CTX_LEN, :] = kc
    vcat_s[0:CTX_LEN, :] = vc
    gq_rows = NA_GROUP * GRID_W
    for g in range(rows // NA_GROUP):
        r = g * NA_GROUP
        if g == 0:
            tab, w0, nk = top_s, 0, NA_KEYS_EDGE
        elif g == rows // NA_GROUP - 1:
            tab, w0, nk = bot_s, rows - NA_WIN_R, NA_KEYS_EDGE
        else:
            tab, w0, nk = mid_s, r - NA_WIN_R // 2, NA_KEYS_MID
        k0 = CTX_LEN + w0 * GRID_W
        kcat_s[CTX_LEN:nk, :] = kn_s[k0:k0 + nk - CTX_LEN, :]
        vcat_s[CTX_LEN:nk, :] = v_ref[k0:k0 + nk - CTX_LEN, :]
        q0 = CTX_LEN + r * GRID_W
        qg = qn_s[q0:q0 + gq_rows, :]
        res = []
        for h in range(2):
            qm = jnp.where(head_mask[h], qg, jnp.zeros_like(qg))
            (e,), l = _softmax_parts([_dot_nt(qm, kcat_s[0:nk, :]) + tab[h, :, 0:nk]])
            res.append(_dot(e.astype(BF16), vcat_s[0:nk, :]) / l)
        o_ref[q0:q0 + gq_rows, :] = jnp.where(head_mask[0], res[0], res[1]).astype(o_ref.dtype)


def _na_branch(p3, qk_gain, rpb):
    b, seq, _ = p3.shape
    gain2 = jnp.tile(qk_gain, (1, 2))
    lo = GRID_W - NA_WIN_C
    w = jnp.pad(rpb, ((0, 0), (0, 0), (lo, 128 - lo - rpb.shape[2])))
    nb = BRANCH_W // 128
    return pl.pallas_call(
        _na_kernel,
        out_shape=jax.ShapeDtypeStruct((b, seq, BRANCH_W), BF16),
        grid=(nb, b),
        in_specs=[
            pl.BlockSpec((None, seq, 128), lambda hp, i: (i, 0, hp)),
            pl.BlockSpec((None, seq, 128), lambda hp, i: (i, 0, nb + hp)),
            pl.BlockSpec((None, seq, 128), lambda hp, i: (i, 0, 2 * nb + hp)),
            pl.BlockSpec((2, 128), lambda hp, i: (0, 0)),
            pl.BlockSpec((2, 2 * NA_WIN_R - 1, 128), lambda hp, i: (hp, 0, 0)),
        ],
        out_specs=pl.BlockSpec((None, seq, 128), lambda hp, i: (i, 0, hp)),
        scratch_shapes=[pltpu.VMEM((seq, 128), BF16), pltpu.VMEM((seq, 128), BF16),
                        pltpu.VMEM((2, NA_GROUP * GRID_W, NA_KEYS_EDGE), F32),
                        pltpu.VMEM((2, NA_GROUP * GRID_W, NA_KEYS_MID), F32),
                        pltpu.VMEM((2, NA_GROUP * GRID_W, NA_KEYS_EDGE), F32),
                        pltpu.VMEM((NA_KEYS_MID, 128), BF16), pltpu.VMEM((NA_KEYS_MID, 128), BF16)],
        compiler_params=pltpu.CompilerParams(
            dimension_semantics=("arbitrary", "arbitrary"), vmem_limit_bytes=VMEM_LIMIT),
        name="na_branch",
    )(p3, p3, p3, gain2, w)


DIFF_TQ = 256


def _rope_lanes(x, cos, sin_signed, half, first_mask):
    n = x.shape[-1]
    fwd = pltpu.roll(x, n - half, 1)
    bwd = pltpu.roll(x, half, 1)
    return x * cos + jnp.where(first_mask, fwd, bwd) * sin_signed


def _diff_kernel(q_ref, k_ref, v_ref, gain_ref, cos_ref, sin_ref, lam_ref, sub_ref, o_ref,
                 qn_s, kn_s, *, lam_init):
    seq = q_ref.shape[0]
    ones_bd = _group_ones(128, HEAD_DIM)
    gq = gain_ref[0:1, :]
    gk = gain_ref[1:2, :]
    scale = HEAD_DIM ** -0.5
    lane = lax.broadcasted_iota(jnp.int32, (1, 128), 1)
    first = (lane % 32) < 16
    blk = 256
    for c in range(seq // blk):
        sl = slice(c * blk, (c + 1) * blk)
        qn = _group_rms(q_ref[sl, :].astype(F32), gq, ones_bd, HEAD_DIM)
        kn = _group_rms(k_ref[sl, :].astype(F32), gk, ones_bd, HEAD_DIM)
        if c * blk >= CTX_LEN:
            ps = slice(c * blk - CTX_LEN, (c + 1) * blk - CTX_LEN)
            qn = _rope_lanes(qn, cos_ref[ps, :], sin_ref[ps, :], 16, first)
            kn = _rope_lanes(kn, cos_ref[ps, :], sin_ref[ps, :], 16, first)
        qn_s[sl, :] = (qn * scale).astype(BF16)
        kn_s[sl, :] = kn.astype(BF16)

    lp = lam_ref[...]
    lam = (jnp.exp(jnp.sum(lp[0:1, :] * lp[1:2, :], axis=-1, keepdims=True))
           - jnp.exp(jnp.sum(lp[2:3, :] * lp[3:4, :], axis=-1, keepdims=True)) + lam_init)
    comp_mask = [lane < HEAD_DIM, lane >= HEAD_DIM]
    sub = sub_ref[...]

    def attend(q, keys, vals):
        ps = []
        for c in range(2):
            qm = jnp.where(comp_mask[c], q, jnp.zeros_like(q))
            (e,), l = _softmax_parts([_dot_nt(qm, keys)])
            ps.append((e, l))
        a = ps[0][0] * (1.0 / ps[0][1]) - ps[1][0] * (lam / ps[1][1])
        o = _dot(a.astype(BF16), vals)
        return _rms_rows(o, sub) * (1.0 - lam_init)

    o_ref[0:CTX_LEN, :] = attend(qn_s[0:CTX_LEN, :], kn_s[0:CTX_LEN, :], v_ref[0:CTX_LEN, :]).astype(o_ref.dtype)

    def q_step(i, carry):
        q0 = pl.multiple_of(CTX_LEN + i * DIFF_TQ, DIFF_TQ)
        o_ref[pl.ds(q0, DIFF_TQ), :] = attend(qn_s[pl.ds(q0, DIFF_TQ), :], kn_s[...], v_ref[...]).astype(o_ref.dtype)
        return carry

    lax.fori_loop(0, (seq - CTX_LEN) // DIFF_TQ, q_step, 0)


def _rope_angles(pos, dim):
    inv = ROPE_BASE ** (-np.arange(0, dim, 2, dtype=np.float32) / dim)
    return pos.astype(np.float32)[:, None] * inv[None, :]


def _cos_sin_tables(ang, sign):
    a = ang.astype(np.float64)
    return (jnp.asarray(np.cos(a).astype(np.float32)),
            jnp.asarray((np.sin(a) * sign[None, :]).astype(np.float32)))


def _diff_rope_tables(s):
    t = np.arange(s)
    ang_r = _rope_angles(t // GRID_W, HEAD_DIM // 2)
    ang_c = _rope_angles(t % GRID_W, HEAD_DIM // 2)
    ang64 = np.concatenate([ang_r, ang_r, ang_c, ang_c], axis=1)
    sign64 = np.concatenate([-np.ones(16), np.ones(16), -np.ones(16), np.ones(16)]).astype(np.float32)
    return _cos_sin_tables(np.tile(ang64, (1, 2)), np.tile(sign64, 2))


def _diff_branch(p3, qk_gain, lam_p, subln, lam_init):
    b, seq, _ = p3.shape
    s = seq - CTX_LEN
    cos, sin = _diff_rope_tables(s)
    gain2 = jnp.tile(qk_gain, (1, 2))
    base = 3072 // 128
    nh = DIFF_HEADS
    return pl.pallas_call(
        functools.partial(_diff_kernel, lam_init=lam_init),
        out_shape=jax.ShapeDtypeStruct((b, seq, BRANCH_W), BF16),
        grid=(b, nh),
        in_specs=[
            pl.BlockSpec((None, seq, 128), lambda i, h: (i, 0, base + h)),
            pl.BlockSpec((None, seq, 128), lambda i, h: (i, 0, base + nh + h)),
            pl.BlockSpec((None, seq, 128), lambda i, h: (i, 0, base + 2 * nh + h)),
            pl.BlockSpec((2, 128), lambda i, h: (0, 0)),
            pl.BlockSpec((s, 128), lambda i, h: (0, 0)),
            pl.BlockSpec((s, 128), lambda i, h: (0, 0)),
            pl.BlockSpec((4, HEAD_DIM), lambda i, h: (0, 0)),
            pl.BlockSpec((1, 128), lambda i, h: (0, 0)),
        ],
        out_specs=pl.BlockSpec((None, seq, 128), lambda i, h: (i, 0, h)),
        scratch_shapes=[pltpu.VMEM((seq, 128), BF16), pltpu.VMEM((seq, 128), BF16)],
        compiler_params=pltpu.CompilerParams(
            dimension_semantics=("arbitrary", "arbitrary"), vmem_limit_bytes=VMEM_LIMIT),
        name="diff_branch",
    )(p3, p3, p3, gain2, cos, sin, lam_p, subln.reshape(1, 128))


def _log_sigmoid(x):
    return jnp.minimum(x, 0.0) - jnp.log(1.0 + jnp.exp(-jnp.abs(x)))


def _ret_kernel(q_ref, k_ref, v_ref, g_ref, cos_ref, sin_ref, dec_ref, gn_ref, o_ref,
                q_s, k_s, o_s):
    seq = q_ref.shape[0]
    c = RET_CHUNK
    n_ctx_chunks = CTX_LEN // c
    n_chunks = seq // c
    hw = RET_HEADS * HEAD_DIM
    vw = RET_HEADS * 2 * HEAD_DIM
    lane = lax.broadcasted_iota(jnp.int32, (1, hw), 1)
    first = (lane % HEAD_DIM) < (HEAD_DIM // 2)
    blk = 256
    kscale = HEAD_DIM ** -0.5
    for cb in range(seq // blk):
        sl = slice(cb * blk, (cb + 1) * blk)
        q = q_ref[sl, :].astype(F32)
        k = k_ref[sl, :].astype(F32)
        if cb * blk >= CTX_LEN:
            ps = slice(cb * blk - CTX_LEN, (cb + 1) * blk - CTX_LEN)
            q = _rope_lanes(q, cos_ref[ps, :], sin_ref[ps, :], HEAD_DIM // 2, first)
            k = _rope_lanes(k, cos_ref[ps, :], sin_ref[ps, :], HEAD_DIM // 2, first)
        q_s[sl, :] = q
        k_s[sl, :] = k * kscale

    log_g = _log_sigmoid(dec_ref[...])
    pos_r = lax.broadcasted_iota(jnp.int32, (c, c), 0).astype(F32)
    pos_c = lax.broadcasted_iota(jnp.int32, (c, c), 1).astype(F32)
    pos = lax.broadcasted_iota(jnp.int32, (c, 1), 0).astype(F32)
    head_of_q = lax.broadcasted_iota(jnp.int32, (1, hw), 1) // HEAD_DIM
    head_of_v = lax.broadcasted_iota(jnp.int32, (1, vw), 1) // (2 * HEAD_DIM)
    row_head = lax.broadcasted_iota(jnp.int32, (hw, vw), 0) // HEAD_DIM
    col_head = lax.broadcasted_iota(jnp.int32, (hw, vw), 1) // (2 * HEAD_DIM)
    state_mask = row_head == col_head

    intra, q_dec, k_dec, c_dec = [], [], [], []
    for d in range(2):
        rel = (pos_r - pos_c) if d == 0 else (pos_c - pos_r)
        lg_q = jnp.zeros((1, hw), F32)
        lg_v = jnp.zeros((1, vw), F32)
        per_head = []
        for h in range(RET_HEADS):
            lg = log_g[d:d + 1, h:h + 1]
            per_head.append(jnp.where(rel >= 0, jnp.exp(lg * jnp.maximum(rel, 0.0)), 0.0))
            lg_q = jnp.where(head_of_q == h, lg, lg_q)
            lg_v = jnp.where(head_of_v == h, lg, lg_v)
        intra.append(per_head)
        qpos = (pos + 1.0) if d == 0 else (c - pos)
        kpos = (c - 1.0 - pos) if d == 0 else pos
        q_dec.append(jnp.exp(lg_q * qpos))
        k_dec.append(jnp.exp(lg_q * kpos))
        c_dec.append(jnp.exp(lg_v * c))

    head_mask = [head_of_q == h for h in range(RET_HEADS)]

    def chunk_start(d, n):
        if d == 0:
            return n * c
        rev_ctx = (n_ctx_chunks - 1 - n) * c
        rev_lat = CTX_LEN + (n_chunks - 1 - n) * c
        return jnp.where(n < n_ctx_chunks, rev_ctx, rev_lat)

    def step(n, states):
        new_states = []
        for d in range(2):
            r0 = pl.multiple_of(chunk_start(d, n), c)
            s = states[d]
            qi = q_s[pl.ds(r0, c), :]
            ki = k_s[pl.ds(r0, c), :]
            vi = v_ref[pl.ds(r0, c), :].astype(BF16)
            kb = ki.astype(BF16)
            cross = _dot((qi * q_dec[d]).astype(BF16), s.astype(BF16))
            parts = []
            for h in range(RET_HEADS):
                qm = jnp.where(head_mask[h], qi, 0.0).astype(BF16)
                att = _dot_nt(qm, kb) * intra[d][h]
                parts.append(_dot(att.astype(BF16), vi[:, h * 128:(h + 1) * 128]))
            o = jnp.concatenate(parts, axis=1) + cross
            o_s[pl.ds(r0, c), :] = o_s[pl.ds(r0, c), :] + o
            upd = _dot_tn((ki * k_dec[d]).astype(BF16), vi)
            new_states.append(s * c_dec[d] + jnp.where(state_mask, upd, 0.0))
        return tuple(new_states)

    o_s[...] = jnp.zeros(o_s.shape, F32)
    zero = jnp.zeros((hw, vw), F32)
    lax.fori_loop(0, n_chunks, step, (zero, zero))

    gn = gn_ref[...]
    for cb in range(seq // blk):
        sl = slice(cb * blk, (cb + 1) * blk)
        outs = []
        for h in range(RET_HEADS):
            hs = slice(h * 128, (h + 1) * 128)
            y = o_s[sl, hs]
            yc = y - jnp.mean(y, axis=-1, keepdims=True)
            yn = yc * lax.rsqrt(jnp.mean(yc * yc, axis=-1, keepdims=True) + EPS) * gn
            outs.append(yn * _silu(g_ref[sl, hs].astype(F32)))
        o_ref[sl, :] = jnp.concatenate(outs, axis=1).astype(o_ref.dtype)


def _ret_rope_tables(s):
    ang = _rope_angles(np.arange(s), HEAD_DIM)
    ang64 = np.concatenate([ang, ang], axis=1)
    sign64 = np.concatenate([-np.ones(32), np.ones(32)]).astype(np.float32)
    return _cos_sin_tables(np.tile(ang64, (1, RET_HEADS)), np.tile(sign64, RET_HEADS))


def _ret_branch(p3, decay, gn_gain):
    b, seq, _ = p3.shape
    s = seq - CTX_LEN
    cos, sin = _ret_rope_tables(s)
    return pl.pallas_call(
        _ret_kernel,
        out_shape=jax.ShapeDtypeStruct((b, seq, BRANCH_W), BF16),
        grid=(b,),
        in_specs=[
            pl.BlockSpec((None, seq, 256), lambda i: (i, 0, 6)),
            pl.BlockSpec((None, seq, 256), lambda i: (i, 0, 7)),
            pl.BlockSpec((None, seq, 512), lambda i: (i, 0, 4)),
            pl.BlockSpec((None, seq, 512), lambda i: (i, 0, 5)),
            pl.BlockSpec((s, 256), lambda i: (0, 0)),
            pl.BlockSpec((s, 256), lambda i: (0, 0)),
            pl.BlockSpec((2, RET_HEADS), lambda i: (0, 0)),
            pl.BlockSpec((1, 128), lambda i: (0, 0)),
        ],
        out_specs=pl.BlockSpec((None, seq, BRANCH_W), lambda i: (i, 0, 0)),
        scratch_shapes=[pltpu.VMEM((seq, 256), F32), pltpu.VMEM((seq, 256), F32),
                        pltpu.VMEM((seq, 512), F32)],
        compiler_params=pltpu.CompilerParams(
            dimension_semantics=("arbitrary",), vmem_limit_bytes=VMEM_LIMIT),
        name="ret_branch",
    )(p3, p3, p3, p3, cos, sin, decay, gn_gain.reshape(1, 128))


DN_PAD = 8
DN_ROWS = 4 * DN_CHUNK


def _softplus(x):
    return jnp.maximum(x, 0.0) + jnp.log(1.0 + jnp.exp(-jnp.abs(x)))


def _dn_conv_silu(pad_s, w_ref, seq, width):
    blk = 256
    outs = []
    for cb in range(seq // blk):
        base = DN_PAD + cb * blk + (DN_PAD if cb * blk >= CTX_LEN else 0)
        ext = pad_s[base - DN_PAD:base + blk + DN_PAD, 0:width]
        n = blk + 2 * DN_PAD
        acc = ext * w_ref[DN_CONV // 2:DN_CONV // 2 + 1, :]
        for j in range(DN_CONV):
            if j == DN_CONV // 2:
                continue
            acc = acc + pltpu.roll(ext, (DN_CONV // 2 - j) % n, 0) * w_ref[j:j + 1, :]
        outs.append(_silu(acc[DN_PAD:DN_PAD + blk, :]))
    return outs


def _dn_fill_pad(pad_s, refs, seq):
    off = 0
    for ref in refs:
        w = ref.shape[1]
        pad_s[DN_PAD:DN_PAD + CTX_LEN, off:off + w] = ref[0:CTX_LEN, :].astype(F32)
        pad_s[2 * DN_PAD + CTX_LEN:2 * DN_PAD + seq, off:off + w] = ref[CTX_LEN:seq, :].astype(F32)
        off += w


def _dn_kernel(q_ref, k_ref, v_ref, g_ref, ab_ref, wq_ref, wk_ref, wv_ref, par_ref, ng_ref, o_ref,
               pad_s, qk_s, v_s, gb_s, o_s, ru_s, rw_s, rq_s, ra_s, rk_s, rg_s):
    seq = q_ref.shape[0]
    hp = pl.program_id(1)
    c = DN_CHUNK
    n_ctx_chunks = CTX_LEN // c
    n_chunks = seq // c
    blk = 256

    zeros_pad = jnp.zeros((DN_PAD, 256), F32)
    pad_s[0:DN_PAD, :] = zeros_pad
    pad_s[DN_PAD + CTX_LEN:2 * DN_PAD + CTX_LEN, :] = zeros_pad
    pad_s[2 * DN_PAD + seq:3 * DN_PAD + seq, :] = zeros_pad
    ones_bd = _group_ones(128, HEAD_DIM)

    _dn_fill_pad(pad_s, [q_ref], seq)
    for cb, x in enumerate(_dn_conv_silu(pad_s, wq_ref, seq, 128)):
        ss = _split_dot(x * x, ones_bd)
        qk_s[cb * blk:(cb + 1) * blk, 0:128] = x * lax.rsqrt(ss + EPS) * (HEAD_DIM ** -0.5)
    _dn_fill_pad(pad_s, [k_ref], seq)
    for cb, x in enumerate(_dn_conv_silu(pad_s, wk_ref, seq, 128)):
        ss = _split_dot(x * x, ones_bd)
        qk_s[cb * blk:(cb + 1) * blk, 128:256] = x * lax.rsqrt(ss + EPS)
    _dn_fill_pad(pad_s, [v_ref], seq)
    for cb, x in enumerate(_dn_conv_silu(pad_s, wv_ref, seq, 256)):
        v_s[cb * blk:(cb + 1) * blk, :] = x

    lane = lax.broadcasted_iota(jnp.int32, (1, 128), 1)
    a_log = par_ref[0:1, :]
    dt_bias = par_ref[1:2, :]
    shift = (128 - 2 * hp) % 128
    for cb in range(seq // blk):
        ab = ab_ref[cb * blk:(cb + 1) * blk, :]
        gdec = -jnp.exp(a_log) * _softplus(ab + dt_bias)
        beta = _sigmoid(ab)
        gb = jnp.where(lane < 2 * DN_HEADS, gdec, beta)
        gb_s[cb * blk:(cb + 1) * blk, :] = pltpu.roll(gb, shift, 1)

    ri = lax.broadcasted_iota(jnp.int32, (DN_ROWS, DN_ROWS), 0)
    ci = lax.broadcasted_iota(jnp.int32, (DN_ROWS, DN_ROWS), 1)
    same_blk = (ri // c) == (ci // c)
    is_fwd = ri < 2 * c
    rp, cp = ri % c, ci % c
    is_bwd = jnp.logical_not(is_fwd)
    strict = jnp.logical_and(same_blk, jnp.logical_or(jnp.logical_and(is_fwd, rp > cp),
                                                      jnp.logical_and(is_bwd, rp < cp)))
    incl = jnp.logical_or(strict, ri == ci)
    eye = jnp.where(ri == ci, 1.0, 0.0)

    r64 = lax.broadcasted_iota(jnp.int32, (c, c), 0)
    c64 = lax.broadcasted_iota(jnp.int32, (c, c), 1)
    tri_lo = jnp.where(r64 >= c64, 1.0, 0.0).astype(BF16)
    tri_up = jnp.where(r64 <= c64, 1.0, 0.0).astype(BF16)
    ones64 = jnp.ones((c, c), BF16)

    def split_left(m, x):
        hi = x.astype(BF16)
        lo = (x - hi.astype(F32)).astype(BF16)
        return _dot(m, hi) + _dot(m, lo)

    def chunk_start(d, n):
        if d == 0:
            return n * c
        rev_ctx = (n_ctx_chunks - 1 - n) * c
        rev_lat = CTX_LEN + (n_chunks - 1 - n) * c
        return jnp.where(n < n_ctx_chunks, rev_ctx, rev_lat)

    def stacked_cols(mats, cols, width):
        return jnp.concatenate(
            [jnp.broadcast_to(mats[d][:, cols[d][h]:cols[d][h] + 1], (c, width))
             for d in range(2) for h in range(2)], axis=0)

    def chunk_rows(n):
        n = jnp.asarray(n, jnp.int32)
        return [pl.multiple_of(chunk_start(d, n), c) for d in range(2)]

    def prep(n, slot):
        r0 = chunk_rows(n)
        qk = [qk_s[pl.ds(r0[d], c), :] for d in range(2)]
        vv = [v_s[pl.ds(r0[d], c), :] for d in range(2)]
        gb = [gb_s[pl.ds(r0[d], c), :] for d in range(2)]
        gc = [split_left(tri_lo, gb[0]), split_left(tri_up, gb[1])]
        gt = [split_left(ones64, gb[d]) for d in range(2)]
        g_cols = [[0, 1], [4, 5]]
        b_cols = [[8, 9], [12, 13]]
        gcb = stacked_cols(gc, g_cols, DN_ROWS)
        gtb = stacked_cols(gt, g_cols, DN_ROWS)
        bb = stacked_cols(gb, b_cols, DN_ROWS)

        qrow = jnp.concatenate([qk[d][:, 0:128] for d in range(2) for _ in range(2)], axis=0)
        krow = jnp.concatenate([qk[d][:, 128:256] for d in range(2) for _ in range(2)], axis=0)
        qx = jnp.where(same_blk, jnp.concatenate([qrow, qrow], axis=1), 0.0)
        kx = jnp.where(same_blk, jnp.concatenate([krow, krow], axis=1), 0.0)
        vx = jnp.concatenate([vv[d][:, h * 128:(h + 1) * 128] for d in range(2) for h in range(2)], axis=0)

        kxb = kx.astype(BF16)
        kk = _dot_nt(kxb, kxb)
        qkm = _dot_nt(qx.astype(BF16), kxb)
        diff = gcb - jnp.transpose(gcb)
        decay = jnp.where(incl, jnp.exp(jnp.where(incl, diff, 0.0)), 0.0)
        lower = jnp.where(strict, bb * kk * decay, 0.0)

        m = -lower
        x = eye + m
        for _ in range(5):
            mb = m.astype(BF16)
            m = _dot(mb, mb)
            x = x + _dot(x.astype(BF16), m.astype(BF16))
        tb = x.astype(BF16)

        eg = jnp.exp(gcb)
        ru_s[slot] = _dot(tb, (vx * bb[:, 0:128]).astype(BF16))
        rw_s[slot] = _dot(tb, (kx * bb * eg).astype(BF16)).astype(BF16)
        rq_s[slot] = (qx * eg).astype(BF16)
        ra_s[slot] = (qkm * decay).astype(BF16)
        rk_s[slot] = (kx * jnp.exp(gtb - gcb)).astype(BF16)
        rg_s[slot] = jnp.exp(gtb[:, 0:128])

    def apply(n, slot, s):
        r0 = chunk_rows(n)
        sb = s.astype(BF16)
        v_new = (ru_s[slot] - _dot(rw_s[slot], sb)).astype(BF16)
        o = _dot(rq_s[slot], sb) + _dot(ra_s[slot], v_new)
        s_new = s * rg_s[slot] + _dot_tn(rk_s[slot], v_new)
        for d in range(2):
            od = jnp.concatenate([o[(2 * d + h) * c:(2 * d + h + 1) * c, :] for h in range(2)], axis=1)
            o_s[d, pl.ds(r0[d], c), :] = od
        return s_new

    prep(0, 0)
    prep(1, 1)

    def group(g, s):
        n = 4 * g
        prep(n + 2, 2)
        prep(n + 3, 3)
        s = apply(n, 0, s)
        s = apply(n + 1, 1, s)
        prep(jnp.minimum(n + 4, n_chunks - 1), 0)
        prep(jnp.minimum(n + 5, n_chunks - 1), 1)
        s = apply(n + 2, 2, s)
        s = apply(n + 3, 3, s)
        return s

    assert n_chunks % 4 == 0
    lax.fori_loop(0, n_chunks // 4, group, jnp.zeros((DN_ROWS, 128), F32))

    ng = ng_ref[...]
    for cb in range(seq // blk):
        sl = slice(cb * blk, (cb + 1) * blk)
        outs = []
        for h in range(2):
            hs = slice(h * 128, (h + 1) * 128)
            outs.append(_rms_rows(o_s[0, sl, hs] + o_s[1, sl, hs], ng) * _silu(g_ref[sl, hs].astype(F32)))
        o_ref[sl, :] = jnp.concatenate(outs, axis=1).astype(o_ref.dtype)


def _dn_branch(p3, pab3, conv_w, a_log, dt_bias, norm_gain):
    b, seq, _ = p3.shape
    par = jnp.zeros((2, 128), F32)
    par = par.at[0, 0:2 * DN_HEADS].set(a_log.reshape(-1)).at[1, 0:2 * DN_HEADS].set(dt_bias.reshape(-1))
    qb, kb_, vb, gb_ = 4608 // 128, 4864 // 128, 5120 // 256, 5632 // 256
    ring = lambda r, w, dt: pltpu.VMEM((4, r, w), dt)
    return pl.pallas_call(
        _dn_kernel,
        out_shape=jax.ShapeDtypeStruct((b, seq, BRANCH_W), BF16),
        grid=(b, 2),
        in_specs=[
            pl.BlockSpec((None, seq, 128), lambda i, hp: (i, 0, qb + hp)),
            pl.BlockSpec((None, seq, 128), lambda i, hp: (i, 0, kb_ + hp)),
            pl.BlockSpec((None, seq, 256), lambda i, hp: (i, 0, vb + hp)),
            pl.BlockSpec((None, seq, 256), lambda i, hp: (i, 0, gb_ + hp)),
            pl.BlockSpec((None, seq, 128), lambda i, hp: (i, 0, 0)),
            pl.BlockSpec((DN_CONV, 128), lambda i, hp: (0, hp)),
            pl.BlockSpec((DN_CONV, 128), lambda i, hp: (0, 2 + hp)),
            pl.BlockSpec((DN_CONV, 256), lambda i, hp: (0, 2 + hp)),
            pl.BlockSpec((2, 128), lambda i, hp: (0, 0)),
            pl.BlockSpec((1, 128), lambda i, hp: (0, 0)),
        ],
        out_specs=pl.BlockSpec((None, seq, 256), lambda i, hp: (i, 0, hp)),
        scratch_shapes=[pltpu.VMEM((seq + 3 * DN_PAD, 256), F32), pltpu.VMEM((seq, 256), F32),
                        pltpu.VMEM((seq, 256), F32), pltpu.VMEM((seq, 128), F32),
                        pltpu.VMEM((2, seq, 256), F32),
                        ring(DN_ROWS, 128, F32), ring(DN_ROWS, DN_ROWS, BF16), ring(DN_ROWS, DN_ROWS, BF16),
                        ring(DN_ROWS, DN_ROWS, BF16), ring(DN_ROWS, DN_ROWS, BF16), ring(DN_ROWS, 128, F32)],
        compiler_params=pltpu.CompilerParams(
            dimension_semantics=("arbitrary", "arbitrary"), vmem_limit_bytes=VMEM_LIMIT),
        name="dn_branch",
    )(p3, p3, p3, p3, pab3, conv_w, conv_w, conv_w, par, norm_gain.reshape(1, 128))


def _combine_kernel(hz_ref, y0_ref, y1_ref, y2_ref, y3_ref, xs_ref, mod_ref, wg_ref, bg_ref, wb_ref,
                    wo_ref, gf_ref, xo_ref, hl_ref, *, tiles_per_b):
    i = pl.program_id(0)
    tm = xs_ref.shape[0]
    hz = hz_ref[...]
    acc = jnp.zeros((tm, D_MODEL), F32)
    for n, y_ref in enumerate((y0_ref, y1_ref, y2_ref, y3_ref)):
        cs = slice(n * D_MODEL, (n + 1) * D_MODEL)
        gate = _sigmoid(_dot(hz, wg_ref[:, cs]) + bg_ref[:, cs])
        acc = acc + gate * _dot(y_ref[...], wb_ref[n])
    mix = _dot(acc.astype(BF16), wo_ref[...])
    xn = xs_ref[...] + _mod_rows(mod_ref, i, tiles_per_b, tm, 2) * mix
    xo_ref[...] = xn
    h = _rms_rows(xn, gf_ref[...])
    hl_ref[...] = h * (1.0 + _mod_rows(mod_ref, i, tiles_per_b, tm, 4)) + _mod_rows(mod_ref, i, tiles_per_b, tm, 3)


def _combine(hz, ys, xs, mod, w_mgate, b_mgate, w_branch, w_out, g_ffn, seq_len):
    t = xs.shape[0]
    tiles_per_b = seq_len // TM
    tok = lambda i: (i, 0)
    fix2 = lambda i: (0, 0)
    return pl.pallas_call(
        functools.partial(_combine_kernel, tiles_per_b=tiles_per_b),
        out_shape=(jax.ShapeDtypeStruct((t, D_MODEL), F32), jax.ShapeDtypeStruct((t, D_MODEL), F32)),
        grid=(t // TM,),
        in_specs=[
            pl.BlockSpec((TM, D_MODEL), tok),
            pl.BlockSpec((TM, BRANCH_W), tok), pl.BlockSpec((TM, BRANCH_W), tok),
            pl.BlockSpec((TM, BRANCH_W), tok), pl.BlockSpec((TM, BRANCH_W), tok),
            pl.BlockSpec((TM, D_MODEL), tok),
            pl.BlockSpec((16, 6 * D_MODEL), fix2),
            pl.BlockSpec((D_MODEL, N_BRANCH * D_MODEL), fix2),
            pl.BlockSpec((1, N_BRANCH * D_MODEL), fix2),
            pl.BlockSpec((N_BRANCH, BRANCH_W, D_MODEL), lambda i: (0, 0, 0)),
            pl.BlockSpec((D_MODEL, D_MODEL), fix2),
            pl.BlockSpec((1, D_MODEL), fix2),
        ],
        out_specs=(pl.BlockSpec((TM, D_MODEL), tok), pl.BlockSpec((TM, D_MODEL), tok)),
        compiler_params=pltpu.CompilerParams(
            dimension_semantics=("arbitrary",), vmem_limit_bytes=VMEM_LIMIT),
        name="combine",
    )(hz, *ys, xs, mod, w_mgate, b_mgate.reshape(1, -1), w_branch, w_out, g_ffn.reshape(1, D_MODEL))


def _route_kernel(h_ref, w_ref, b_ref, topi_ref, gate_ref, rank_ref, cnt_ref, base_s):
    i = pl.program_id(0)

    @pl.when(i == 0)
    def _():
        base_s[...] = jnp.zeros(base_s.shape, F32)

    tm = h_ref.shape[0]
    logits = _split_dot_w(h_ref[...], w_ref[...]) + b_ref[...]
    lane = lax.broadcasted_iota(jnp.int32, (tm, 128), 1)
    l = logits
    idxs, vals, hots = [], [], []
    for _ in range(TOP_K):
        m = l.max(axis=-1, keepdims=True)
        idx = jnp.min(jnp.where(l == m, lane, 128), axis=-1, keepdims=True)
        hot = lane == idx
        idxs.append(idx)
        vals.append(m)
        hots.append(hot)
        l = jnp.where(hot, -3e38, l)
    es = [jnp.exp(v - vals[0]) for v in vals]
    den = es[0] + es[1] + es[2] + es[3]

    cnt = jnp.zeros((tm, 128), F32)
    for hot in hots:
        cnt = cnt + jnp.where(hot, 1.0, 0.0)
    r = lax.broadcasted_iota(jnp.int32, (tm, tm), 0)
    c = lax.broadcasted_iota(jnp.int32, (tm, tm), 1)
    before = jnp.where(r > c, 1.0, 0.0).astype(BF16)
    prior = _dot(before, cnt.astype(BF16)) + base_s[...]

    topi = jnp.zeros((tm, 128), jnp.int32)
    gate = jnp.zeros((tm, 128), F32)
    rank = jnp.zeros((tm, 128), jnp.int32)
    for k in range(TOP_K):
        rk = jnp.sum(jnp.where(hots[k], prior, 0.0), axis=-1, keepdims=True)
        topi = jnp.where(lane == k, idxs[k], topi)
        gate = jnp.where(lane == k, es[k] / den, gate)
        rank = jnp.where(lane == k, rk.astype(jnp.int32), rank)
    topi_ref[...] = topi
    gate_ref[...] = gate
    rank_ref[...] = rank
    base_s[...] = base_s[...] + jnp.sum(cnt, axis=0, keepdims=True)
    cnt_ref[...] = base_s[...]


def _route(hl, router_w, router_b, seq_len, latent_only):
    tm = TM_ROUTE if latent_only else TM
    per_b, src = _token_tiles(seq_len, tm, latent_only)
    t = hl.shape[0] // (seq_len // tm) * per_b
    w = jnp.pad(router_w, ((0, 0), (0, 128 - N_EXPERTS)))
    b = jnp.pad(router_b, (0, 128 - N_EXPERTS), constant_values=NEG_BIG).reshape(1, 128)
    tok = lambda i: (i, 0)
    return pl.pallas_call(
        _route_kernel,
        out_shape=(jax.ShapeDtypeStruct((t, 128), jnp.int32), jax.ShapeDtypeStruct((t, 128), F32),
                   jax.ShapeDtypeStruct((t, 128), jnp.int32), jax.ShapeDtypeStruct((1, 128), F32)),
        grid=(t // tm,),
        in_specs=[pl.BlockSpec((tm, D_MODEL), lambda i: (src(i), 0)),
                  pl.BlockSpec((D_MODEL, 128), lambda i: (0, 0)),
                  pl.BlockSpec((1, 128), lambda i: (0, 0))],
        out_specs=(pl.BlockSpec((tm, 128), tok), pl.BlockSpec((tm, 128), tok),
                   pl.BlockSpec((tm, 128), tok), pl.BlockSpec((1, 128), lambda i: (0, 0))),
        scratch_shapes=[pltpu.VMEM((1, 128), F32)],
        compiler_params=pltpu.CompilerParams(dimension_semantics=("arbitrary",)),
        name="route",
    )(hl, w, b)


def _dispatch_kernel(last_ref, dest_ref, h_ref, rows_ref, tile_s, zero_s, sem):
    i = pl.program_id(0)
    tm = h_ref.shape[0]

    @pl.when(i == 0)
    def _():
        zero_s[...] = jnp.zeros(zero_s.shape, F32)

        def blk_copy(blk):
            r = pl.multiple_of(blk * MOE_BM, MOE_BM)
            return pltpu.make_async_copy(zero_s, rows_ref.at[pl.ds(r, MOE_BM)], sem)

        for e in range(N_EXPERTS):
            blk_copy(last_ref[e]).start()
        for e in range(N_EXPERTS):
            blk_copy(last_ref[e]).wait()

        n_blocks = rows_ref.shape[0] // MOE_BM

        def clear_tail(blk, carry):
            blk_copy(blk).start()
            blk_copy(blk).wait()
            return carry

        lax.fori_loop(last_ref[N_EXPERTS], n_blocks, clear_tail, 0)

    tile_s[...] = h_ref[...].reshape(tile_s.shape)

    def issue(t, carry):
        for k in range(TOP_K):
            d = dest_ref[t * TOP_K + k]
            pltpu.make_async_copy(tile_s.at[t], rows_ref.at[d], sem).start(priority=k % 2)
        return carry

    lax.fori_loop(0, tm, issue, 0)
    for k in range(TOP_K):
        pltpu.make_async_copy(tile_s, rows_ref.at[pl.ds(0, tm)], sem).wait()


ROW_TILE = (8, D_MODEL // 8)


def _dispatch(last_blk, dest_flat, hl, n_rows, seq_len, latent_only):
    _, src = _token_tiles(seq_len, TM_DMA, latent_only)
    t = dest_flat.shape[0] // TOP_K
    grid_spec = pltpu.PrefetchScalarGridSpec(
        num_scalar_prefetch=1,
        grid=(t // TM_DMA,),
        in_specs=[pl.BlockSpec((TM_DMA * TOP_K,), lambda i, lb: (i,), memory_space=pltpu.SMEM),
                  pl.BlockSpec((TM_DMA, D_MODEL), lambda i, lb: (src(i), 0))],
        out_specs=pl.BlockSpec(memory_space=pl.ANY),
        scratch_shapes=[pltpu.VMEM((TM_DMA,) + ROW_TILE, F32), pltpu.VMEM((MOE_BM,) + ROW_TILE, F32),
                        pltpu.SemaphoreType.DMA(())],
    )
    return pl.pallas_call(
        _dispatch_kernel,
        out_shape=jax.ShapeDtypeStruct((n_rows,) + ROW_TILE, F32),
        grid_spec=grid_spec,
        compiler_params=pltpu.CompilerParams(dimension_semantics=("arbitrary",), has_side_effects=True),
        name="moe_dispatch",
    )(last_blk, dest_flat, hl)


def _expert_kernel(be_ref, nu_ref, x_ref, wgu_ref, bgu_ref, wd_ref, bd_ref, o_ref, wgu_s, wd_s):
    j = pl.program_id(0)
    d_ff = wd_ref.shape[0]
    used = j < nu_ref[0]
    new_expert = jnp.logical_or(j == 0, be_ref[j] != be_ref[jnp.maximum(j - 1, 0)])

    @pl.when(jnp.logical_and(used, new_expert))
    def _():
        rb = 256
        for r in range(0, D_MODEL, rb):
            wgu_s[r:r + rb, :] = wgu_ref[r:r + rb, :].astype(BF16)
        for r in range(0, d_ff, rb):
            wd_s[r:r + rb, :] = wd_ref[r:r + rb, :].astype(BF16)

    @pl.when(used)
    def _():
        x = x_ref[...].reshape(x_ref.shape[0], D_MODEL)
        gu = _dot(x.astype(BF16), wgu_s[...]) + bgu_ref[...]
        gate = jnp.minimum(gu[:, :d_ff], SWIGLU_LIMIT)
        up = jnp.clip(gu[:, d_ff:], -SWIGLU_LIMIT, SWIGLU_LIMIT)
        y = (up + 1.0) * (gate * _sigmoid(SWIGLU_ALPHA * gate))
        o_ref[...] = (_dot(y.astype(BF16), wd_s[...]) + bd_ref[...]).reshape(o_ref.shape)

    @pl.when(jnp.logical_not(used))
    def _():
        o_ref[...] = jnp.zeros(o_ref.shape, F32)


def _experts(block_e, n_used, rows, layer, w_gu, b_gu, w_down, b_down):
    n_rows = rows.shape[0]
    d_ff = w_down.shape[2]
    depth = w_gu.shape[0]
    row_blk = lambda j, be, nu: (jnp.minimum(j, nu[0] - 1), 0, 0)
    grid_spec = pltpu.PrefetchScalarGridSpec(
        num_scalar_prefetch=2,
        grid=(n_rows // MOE_BM,),
        in_specs=[
            pl.BlockSpec((MOE_BM,) + ROW_TILE, row_blk),
            pl.BlockSpec((None, None, D_MODEL, 2 * d_ff), lambda j, be, nu: (layer, be[j], 0, 0)),
            pl.BlockSpec((None, None, 1, 2 * d_ff), lambda j, be, nu: (layer, be[j], 0, 0)),
            pl.BlockSpec((None, None, d_ff, D_MODEL), lambda j, be, nu: (layer, be[j], 0, 0)),
            pl.BlockSpec((None, None, 1, D_MODEL), lambda j, be, nu: (layer, be[j], 0, 0)),
        ],
        out_specs=pl.BlockSpec((MOE_BM,) + ROW_TILE, lambda j, be, nu: (j, 0, 0)),
        scratch_shapes=[pltpu.VMEM((D_MODEL, 2 * d_ff), BF16), pltpu.VMEM((d_ff, D_MODEL), BF16)],
    )
    return pl.pallas_call(
        _expert_kernel,
        out_shape=jax.ShapeDtypeStruct((n_rows,) + ROW_TILE, F32),
        grid_spec=grid_spec,
        compiler_params=pltpu.CompilerParams(
            dimension_semantics=("arbitrary",), vmem_limit_bytes=VMEM_LIMIT),
        name="moe_experts",
    )(block_e, n_used, rows, w_gu, b_gu.reshape(depth, N_EXPERTS, 1, -1), w_down,
      b_down.reshape(depth, N_EXPERTS, 1, -1))


def _finish_kernel(dest_ref, gate_ref, xs_ref, mod_ref, rows_ref, o_ref, buf, sem, *, tiles_per_b, has_ctx):
    i = pl.program_id(0)
    tm = xs_ref.shape[0]

    half = tm // 2

    def issue_half(p):
        def issue(t, carry):
            for k in range(TOP_K):
                d = dest_ref[t * TOP_K + k]
                pltpu.make_async_copy(rows_ref.at[d], buf.at[k, t], sem.at[p]).start(priority=k % 2)
            return carry
        lax.fori_loop(p * half, (p + 1) * half, issue, 0)

    issue_half(0)
    issue_half(1)
    mod = _mod_rows(mod_ref, i, tiles_per_b, tm, 5, has_ctx)
    for p in range(2):
        rs = slice(p * half, (p + 1) * half)
        for k in range(TOP_K):
            pltpu.make_async_copy(rows_ref.at[pl.ds(0, half)], buf.at[k, rs], sem.at[p]).wait()
        gate = gate_ref[rs, :]
        y = gate[:, 0:1] * buf[0, rs].reshape(half, D_MODEL)
        for k in range(1, TOP_K):
            y = y + gate[:, k:k + 1] * buf[k, rs].reshape(half, D_MODEL)
        o_ref[rs, :] = xs_ref[rs, :] + (mod if mod.shape[0] == 1 else mod[rs]) * y


def _finish(dest_flat, gates, xs, mod, out_rows, seq_len, latent_only):
    tiles_per_b, src = _token_tiles(seq_len, TM_DMA, latent_only)
    t = dest_flat.shape[0] // TOP_K
    return pl.pallas_call(
        functools.partial(_finish_kernel, tiles_per_b=tiles_per_b, has_ctx=not latent_only),
        out_shape=jax.ShapeDtypeStruct((t, D_MODEL), F32),
        grid=(t // TM_DMA,),
        in_specs=[pl.BlockSpec((TM_DMA * TOP_K,), lambda i: (i,), memory_space=pltpu.SMEM),
                  pl.BlockSpec((TM_DMA, 128), lambda i: (i, 0)),
                  pl.BlockSpec((TM_DMA, D_MODEL), lambda i: (src(i), 0)),
                  pl.BlockSpec((16, 6 * D_MODEL), lambda i: (0, 0)),
                  pl.BlockSpec(memory_space=pl.ANY)],
        out_specs=pl.BlockSpec((TM_DMA, D_MODEL), lambda i: (i, 0)),
        scratch_shapes=[pltpu.VMEM((TOP_K, TM_DMA) + ROW_TILE, F32), pltpu.SemaphoreType.DMA((2,))],
        compiler_params=pltpu.CompilerParams(
            dimension_semantics=("arbitrary",), vmem_limit_bytes=VMEM_LIMIT),
        name="moe_finish",
    )(dest_flat, gates, xs, mod, out_rows)


def _moe_plan(topi, rank, counts):
    t = topi.shape[0]
    padded = (counts + MOE_BM - 1) // MOE_BM * MOE_BM
    pad_end = jnp.cumsum(padded)
    pad_start = pad_end - padded
    onehot = topi[:, :, None] == jnp.arange(N_EXPERTS, dtype=jnp.int32)[None, None, :]
    dest = jnp.sum(jnp.where(onehot, pad_start[None, None, :], 0), axis=-1) + rank
    n_blocks = -(-(t * TOP_K + N_EXPERTS * (MOE_BM - 1)) // MOE_BM)
    starts = jnp.arange(n_blocks, dtype=jnp.int32) * MOE_BM
    block_e = jnp.sum(starts[:, None] >= pad_end[None, :], axis=-1).astype(jnp.int32)
    block_e = jnp.minimum(block_e, N_EXPERTS - 1)
    n_used = (pad_end[-1] // MOE_BM).astype(jnp.int32).reshape(1)
    last_blk = jnp.maximum(pad_end // MOE_BM - 1, 0).astype(jnp.int32)
    last_blk = jnp.concatenate([last_blk, n_used])
    return dest.reshape(-1).astype(jnp.int32), block_e, n_used, last_blk, n_blocks * MOE_BM


def kernel(x, c, ctx, c_ctx, w_ada, b_ada, norm_mix, norm_ffn, w_in, na_qk_gain, na_rpb, ret_decay, ret_gn,
           diff_qk_gain, diff_lam, diff_subln, dn_conv, dn_a_log, dn_dt_bias, dn_norm, w_branch, w_mgate,
           b_mgate, w_out, router_w, router_b, w_gu, b_gu, w_down, b_down):
    b, s, d = x.shape
    n_ctx = ctx.shape[1]
    seq = n_ctx + s
    depth = w_ada.shape[0]
    assert (d, n_ctx, s) == (D_MODEL, CTX_LEN, 2048) and b <= 8 and seq % TM == 0

    xs = jnp.concatenate([ctx, x], axis=1).reshape(b * seq, d)
    cc = jnp.zeros((16, d), F32).at[0:b].set(c).at[8].set(c_ctx)
    mods = _ada(cc, w_ada, b_ada)

    for l in range(depth):
        lam_init = 0.8 - 0.6 * math.exp(-0.3 * l)
        w_main = w_in[l, :, :D_IN_MAIN].astype(BF16)
        w_ab = jnp.pad(w_in[l, :, D_IN_MAIN:], ((0, 0), (0, 128 - (D_IN - D_IN_MAIN)))).astype(BF16)
        p, pab, hz = _in_proj(xs, mods[l], norm_mix[l], w_main, w_ab, seq)
        p3 = p.reshape(b, seq, D_IN_MAIN)
        ys = (_na_branch(p3, na_qk_gain[l], na_rpb[l]),
              _ret_branch(p3, ret_decay[l], ret_gn[l]),
              _diff_branch(p3, diff_qk_gain[l], diff_lam[l], diff_subln[l], lam_init),
              _dn_branch(p3, pab.reshape(b, seq, 128), dn_conv[l], dn_a_log[l], dn_dt_bias[l], dn_norm[l]))
        ys = tuple(y.reshape(b * seq, BRANCH_W) for y in ys)
        xs, hl = _combine(hz, ys, xs, mods[l], w_mgate[l].astype(BF16), b_mgate[l], w_branch[l].astype(BF16),
                          w_out[l].astype(BF16), norm_ffn[l], seq)
        last = l == depth - 1
        topi, gates, rank, cnt = _route(hl, router_w[l], router_b[l], seq, last)
        counts = cnt[0, :N_EXPERTS].astype(jnp.int32)
        dest, block_e, n_used, last_blk, n_rows = _moe_plan(topi[:, :TOP_K], rank[:, :TOP_K], counts)
        rows = _dispatch(last_blk, dest, hl, n_rows, seq, last)
        out_rows = _experts(block_e, n_used, rows, l, w_gu, b_gu, w_down, b_down)
        xs = _finish(dest, gates, xs, mods[l], out_rows, seq, last)

    return xs.reshape(b, s, d)
```
